```python
import math
import jax, jax.numpy as jnp
from jax import lax
import numpy as np

D_MODEL = 1024
BATCH = 8
SEQ = 2048
DEPTH = 2

GRID_W = 64
CTX_LEN = 256

ATTN_HEADS = 4
ATTN_DIM = 64
ATTN_VDIM = 2 * ATTN_DIM
Q_W = ATTN_HEADS * 2 * ATTN_DIM
K_W = ATTN_HEADS * 2 * ATTN_DIM
V_W = ATTN_HEADS * ATTN_VDIM
CONV_WIDTH = D_MODEL - V_W
CONV_K = 3
IN_EVEN = Q_W + K_W + V_W + 3 * CONV_WIDTH
Q_BLOCK = 128
ROPE_BASE = 10000.0

SGU_WIDTH = D_MODEL
SGU_GROUPS = 4
CHUNK = 128

D_FF = 2816
N_EXPERTS = 8
TOP_K = 2
D_FF_EXPERT = 3584

N_EVEN = (DEPTH + 1) // 2
N_ODD = DEPTH // 2
EPS = 1e-6

kernel_name = "hybrid_diffattn_shortconv_gmlp_moe_dit"


def rmsnorm(x, g):
    xf = x.astype(jnp.float32)
    y = xf * lax.rsqrt(jnp.mean(xf * xf, axis=-1, keepdims=True) + EPS)
    return (y * g.astype(jnp.float32)).astype(x.dtype)


def layernorm(x, g, b):
    xf = x.astype(jnp.float32)
    mu = jnp.mean(xf, axis=-1, keepdims=True)
    var = jnp.mean(jnp.square(xf - mu), axis=-1, keepdims=True)
    y = (xf - mu) * lax.rsqrt(var + EPS)
    return (y * g.astype(jnp.float32) + b.astype(jnp.float32)).astype(x.dtype)


def modulate(x, g, shift, scale):
    return rmsnorm(x, g) * (1 + scale) + shift


def ada_chunks(cond, w, b):
    m = jax.nn.silu(cond) @ w + b
    return jnp.split(m, 6, axis=-1)


def axial_rope_tables(rows, dim):
    row = jnp.repeat(jnp.arange(rows), GRID_W).astype(jnp.float32)
    col = jnp.tile(jnp.arange(GRID_W), rows).astype(jnp.float32)
    quarter = dim // 4
    inv = ROPE_BASE ** (-jnp.arange(quarter, dtype=jnp.float32) / quarter)
    ar = row[:, None] * inv
    ac = col[:, None] * inv
    ang = jnp.concatenate([ar, ar, ac, ac], axis=-1)
    return jnp.cos(ang), jnp.sin(ang)


def apply_rope(t, cos, sin):
    cos = cos[:, None, None, :].astype(t.dtype)
    sin = sin[:, None, None, :].astype(t.dtype)
    t1, t2, t3, t4 = jnp.split(t, 4, axis=-1)
    rot = jnp.concatenate([-t2, t1, -t4, t3], axis=-1)
    return t * cos + rot * sin


def diff_attention(q, k, v, lam):
    B, S, H, _, d = q.shape
    nb = S // Q_BLOCK
    qb = q.reshape(B, nb, Q_BLOCK, H, 2, d).swapaxes(0, 1)
    scale = d ** -0.5

    def block(qi):
        s = jnp.einsum('bqhcd,bkhcd->bhcqk', qi, k).astype(jnp.float32) * scale
        p = jax.nn.softmax(s, axis=-1)
        w = (p[:, :, 0] - lam * p[:, :, 1]).astype(v.dtype)
        return jnp.einsum('bhqk,bkhe->bqhe', w, v)

    out = lax.map(block, qb)
    return out.swapaxes(0, 1).reshape(B, S, H, v.shape[-1])


def short_conv(x, w):
    C = x.shape[-1]
    pad = CONV_K // 2
    return lax.conv_general_dilated(
        x, w[:, None, :].astype(x.dtype), window_strides=(1,), padding=((pad, pad),),
        dimension_numbers=('NWC', 'WIO', 'NWC'), feature_group_count=C)


def even_mixer(h, hc, w_in, q_g, k_g, lq1, lk1, lq2, lk2, subln_g, conv_w, w_out,
               cos, sin, lambda_init):
    B, S, _ = h.shape
    L = hc.shape[1]
    proj = h @ w_in
    o1 = Q_W
    o2 = o1 + K_W
    o3 = o2 + V_W
    o4 = o3 + CONV_WIDTH
    o5 = o4 + CONV_WIDTH
    q, k, v, gate_b, gate_c, u = jnp.split(proj, [o1, o2, o3, o4, o5], axis=-1)
    q = apply_rope(rmsnorm(q.reshape(B, S, ATTN_HEADS, 2, ATTN_DIM), q_g), cos, sin)
    k = apply_rope(rmsnorm(k.reshape(B, S, ATTN_HEADS, 2, ATTN_DIM), k_g), cos, sin)
    v = v.reshape(B, S, ATTN_HEADS, ATTN_VDIM)
    kc, vc = jnp.split(hc @ w_in[:, o1:o3], [K_W], axis=-1)
    kc = rmsnorm(kc.reshape(B, L, ATTN_HEADS, 2, ATTN_DIM), k_g)
    vc = vc.reshape(B, L, ATTN_HEADS, ATTN_VDIM)
    k_all = jnp.concatenate([kc, k], axis=1)
    v_all = jnp.concatenate([vc, v], axis=1)
    lam = (jnp.exp(jnp.sum(lq1.astype(jnp.float32) * lk1.astype(jnp.float32)))
           - jnp.exp(jnp.sum(lq2.astype(jnp.float32) * lk2.astype(jnp.float32)))
           + lambda_init)
    attn = diff_attention(q, k_all, v_all, lam)
    attn = rmsnorm(attn, subln_g) * (1.0 - lambda_init)
    y_conv = gate_b * short_conv(gate_c * u, conv_w)
    y = jnp.concatenate([attn.reshape(B, S, V_W), y_conv], axis=-1)
    return y @ w_out


def sgu_mixer(h, w_in, ln_g, ln_b, w_s, b_s, w_out):
    B, S, _ = h.shape
    z = jax.nn.gelu(h @ w_in)
    u, v = jnp.split(z, 2, axis=-1)
    v = layernorm(v, ln_g, ln_b)
    n = S // CHUNK
    vg = v.reshape(B, n, CHUNK, SGU_GROUPS, SGU_WIDTH // SGU_GROUPS)
    s = jnp.einsum('gpq,bnqgc->bnpgc', w_s, vg) + b_s.T[:, :, None]
    return (u * s.reshape(B, S, SGU_WIDTH)) @ w_out


def swiglu(h, wg, wu, wd):
    return (jax.nn.silu(h @ wg) * (h @ wu)) @ wd


def moe_swiglu(h, router_w, wg, wu, wd):
    logits = (h @ router_w).astype(jnp.float32)
    top_v, top_i = lax.top_k(logits, TOP_K)
    gates = jax.nn.softmax(top_v, axis=-1)
    dense_gate = jnp.sum(jax.nn.one_hot(top_i, N_EXPERTS, dtype=jnp.float32)
                         * gates[..., None], axis=-2).astype(h.dtype)
    out = jnp.zeros_like(h)
    for e in range(N_EXPERTS):
        out = out + dense_gate[..., e:e + 1] * swiglu(h, wg[e], wu[e], wd[e])
    return out


def setup_inputs(seed: int = 0) -> dict:
    key = jax.random.key(seed)
    ks = iter(jax.random.split(key, 40))
    D = D_MODEL

    def nrm(shape, scale):
        return jax.random.normal(next(ks), shape, jnp.float32) * scale

    def gain(shape):
        return 1.0 + 0.01 * jax.random.normal(next(ks), shape, jnp.float32)

    return {
        "x": nrm((BATCH, SEQ, D), 1.0),
        "c": nrm((BATCH, D), 1.0),
        "ctx": nrm((BATCH, CTX_LEN, D), 1.0),
        "c_ctx": nrm((D,), 1.0),
        "ada_w": nrm((DEPTH, D, 6 * D), D ** -0.5),
        "ada_b": nrm((DEPTH, 6 * D), 0.01),
        "norm_mix_g": gain((DEPTH, D)),
        "norm_ffn_g": gain((DEPTH, D)),
        "w_in_even": nrm((N_EVEN, D, IN_EVEN), D ** -0.5),
        "q_norm_g": gain((N_EVEN, ATTN_DIM)),
        "k_norm_g": gain((N_EVEN, ATTN_DIM)),
        "lam_q1": nrm((N_EVEN, ATTN_DIM), 0.1),
        "lam_k1": nrm((N_EVEN, ATTN_DIM), 0.1),
        "lam_q2": nrm((N_EVEN, ATTN_DIM), 0.1),
        "lam_k2": nrm((N_EVEN, ATTN_DIM), 0.1),
        "subln_g": gain((N_EVEN, ATTN_VDIM)),
        "conv_w": nrm((N_EVEN, CONV_K, CONV_WIDTH), CONV_K ** -0.5),
        "w_out_even": nrm((N_EVEN, V_W + CONV_WIDTH, D), (V_W + CONV_WIDTH) ** -0.5),
        "ffn_w_gate": nrm((N_EVEN, D, D_FF), D ** -0.5),
        "ffn_w_up": nrm((N_EVEN, D, D_FF), D ** -0.5),
        "ffn_w_down": nrm((N_EVEN, D_FF, D), D_FF ** -0.5),
        "sgu_w_in": nrm((N_ODD, D, 2 * SGU_WIDTH), D ** -0.5),
        "sgu_ln_g": gain((N_ODD, SGU_WIDTH)),
        "sgu_ln_b": nrm((N_ODD, SGU_WIDTH), 0.01),
        "sgu_w_s": nrm((N_ODD, SGU_GROUPS, CHUNK, CHUNK), CHUNK ** -0.5),
        "sgu_b_s": nrm((N_ODD, SGU_GROUPS, CHUNK), 0.01),
        "sgu_w_out": nrm((N_ODD, SGU_WIDTH, D), SGU_WIDTH ** -0.5),
        "router_w": nrm((N_ODD, D, N_EXPERTS), D ** -0.5),
        "moe_w_gate": nrm((N_ODD, N_EXPERTS, D, D_FF_EXPERT), D ** -0.5),
        "moe_w_up": nrm((N_ODD, N_EXPERTS, D, D_FF_EXPERT), D ** -0.5),
        "moe_w_down": nrm((N_ODD, N_EXPERTS, D_FF_EXPERT, D), D_FF_EXPERT ** -0.5),
    }


def reference(x, c, ctx, c_ctx, ada_w, ada_b, norm_mix_g, norm_ffn_g,
              w_in_even, q_norm_g, k_norm_g, lam_q1, lam_k1, lam_q2, lam_k2, subln_g,
              conv_w, w_out_even, ffn_w_gate, ffn_w_up, ffn_w_down,
              sgu_w_in, sgu_ln_g, sgu_ln_b, sgu_w_s, sgu_b_s, sgu_w_out,
              router_w, moe_w_gate, moe_w_up, moe_w_down):
    rows = x.shape[1] // GRID_W
    cos, sin = axial_rope_tables(rows, ATTN_DIM)
    h = x
    for i in range(DEPTH):
        sm, cm, gm, sf, cf, gf = [t[:, None, :] for t in ada_chunks(c, ada_w[i], ada_b[i])]
        hm = modulate(h, norm_mix_g[i], sm, cm)
        j = i // 2
        if i % 2 == 0:
            c_sm, c_cm, _, _, _, _ = ada_chunks(c_ctx, ada_w[i], ada_b[i])
            hc = modulate(ctx, norm_mix_g[i], c_sm, c_cm)
            lambda_init = 0.8 - 0.6 * math.exp(-0.3 * i)
            h = h + gm * even_mixer(hm, hc, w_in_even[j], q_norm_g[j], k_norm_g[j],
                                    lam_q1[j], lam_k1[j], lam_q2[j], lam_k2[j], subln_g[j],
                                    conv_w[j], w_out_even[j], cos, sin, lambda_init)
            hf = modulate(h, norm_ffn_g[i], sf, cf)
            h = h + gf * swiglu(hf, ffn_w_gate[j], ffn_w_up[j], ffn_w_down[j])
        else:
            h = h + gm * sgu_mixer(hm, sgu_w_in[j], sgu_ln_g[j], sgu_ln_b[j],
                                   sgu_w_s[j], sgu_b_s[j], sgu_w_out[j])
            hf = modulate(h, norm_ffn_g[i], sf, cf)
            h = h + gf * moe_swiglu(hf, router_w[j], moe_w_gate[j], moe_w_up[j], moe_w_down[j])
    return h
```

```python
import functools
import math

import jax
import jax.numpy as jnp
from jax import lax
from jax.experimental import pallas as pl
from jax.experimental.pallas import tpu as pltpu

F32 = jnp.float32
BF16 = jnp.bfloat16

D_MODEL = 1024
GRID_W = 64
ATTN_HEADS = 4
ATTN_DIM = 64
ATTN_VDIM = 2 * ATTN_DIM
QK_W = ATTN_HEADS * 2 * ATTN_DIM
CONV_K = 3
ROPE_BASE = 10000.0
SGU_GROUPS = 4
CHUNK = 128
N_EXPERTS = 8
TOP_K = 2
EPS = 1e-6
N_MODS = 6
COND_ROWS = 16

V7X_VMEM_LIMIT = 56 * 1024 * 1024
MXU_N = 256
TM_PROJ = 512
HALO = 16
TQ = 256
TM_FFN = 512
TM_SGU = 512
TM_EXP = 512
TF_EXP = 1792
TG = 256


def _params(n_axes):
    return pltpu.CompilerParams(
        dimension_semantics=("arbitrary",) * n_axes, vmem_limit_bytes=V7X_VMEM_LIMIT)


def _const_spec(shape):
    zeros = (0,) * len(shape)
    return pl.BlockSpec(shape, lambda *_: zeros, pipeline_mode=pl.Buffered(1))


def _dot(a, b):
    return jnp.dot(a, b, preferred_element_type=F32)


def _modulate(x, g, scale, shift):
    ms = jnp.mean(x * x, axis=-1, keepdims=True)
    y = x * lax.rsqrt(ms + EPS)
    return (y * g) * (1.0 + scale) + shift


def _split_bf16(x):
    hi = x.astype(BF16)
    lo = (x - hi.astype(F32)).astype(BF16)
    return hi, lo


def _ada_kernel(cond_ref, w_ref, b_ref, o_ref):
    a = jax.nn.silu(cond_ref[...]).astype(BF16)
    o_ref[0] = _dot(a, w_ref[0].astype(BF16)) + b_ref[0]


def _ada_mods(cond, ada_w, ada_b):
    depth = ada_w.shape[0]
    d = D_MODEL
    out = pl.pallas_call(
        _ada_kernel,
        grid=(depth, N_MODS),
        in_specs=[
            pl.BlockSpec((COND_ROWS, d), lambda l, n: (0, 0)),
            pl.BlockSpec((1, d, d), lambda l, n: (l, 0, n)),
            pl.BlockSpec((1, 1, d), lambda l, n: (l, 0, n)),
        ],
        out_specs=pl.BlockSpec((1, COND_ROWS, d), lambda l, n: (l, 0, n)),
        out_shape=jax.ShapeDtypeStruct((depth, COND_ROWS, N_MODS * d), F32),
        compiler_params=_params(2),
        name="ada_mods",
    )(cond, ada_w, ada_b.reshape(depth, 1, N_MODS * d))
    return out.reshape(depth * COND_ROWS, 1, N_MODS * d)


def _mod_spec(layer, chunk, row_of):
    return pl.BlockSpec((1, 1, D_MODEL),
                        lambda *ids: (layer * COND_ROWS + row_of(*ids), 0, chunk))


def _group_rms(t, bd, gain):
    hi, lo = _split_bf16(t * t)
    ss = _dot(hi, bd) + _dot(lo, bd)
    return (t * lax.rsqrt(ss * (1.0 / ATTN_DIM) + EPS)) * gain


def _rope(t, cos, sin_signed, first_half):
    width = t.shape[-1]
    quarter = ATTN_DIM // 4
    ahead = pltpu.roll(t, width - quarter, axis=1)
    behind = pltpu.roll(t, quarter, axis=1)
    return t * cos + jnp.where(first_half, ahead, behind) * sin_signed


def _inproj_kernel(x_ref, xp_ref, xn_ref, sc_ref, sh_ref, g_ref, w_ref, bd_ref, qg_ref, kg_ref,
                   cos_ref, sin_ref, cw_ref, q_ref, k_ref, v_ref, yc_ref):
    i = pl.program_id(1)
    last = pl.num_programs(1) - 1
    tm = x_ref.shape[1]
    w = QK_W
    xe = jnp.concatenate([xp_ref[0], x_ref[0], xn_ref[0]], axis=0)
    he = _modulate(xe, g_ref[...], sc_ref[0], sh_ref[0]).astype(BF16)
    hm = he[HALO:HALO + tm]

    qkv = _dot(hm, w_ref[:, :3 * w])
    bd = bd_ref[...]
    cos = cos_ref[...]
    sin = sin_ref[...]
    lane = lax.broadcasted_iota(jnp.int32, (1, w), 1)
    first_half = (lane % (ATTN_DIM // 2)) < (ATTN_DIM // 4)
    q = _rope(_group_rms(qkv[:, :w], bd, qg_ref[...]), cos, sin, first_half)
    k = _rope(_group_rms(qkv[:, w:2 * w], bd, kg_ref[...]), cos, sin, first_half)
    q_ref[0] = (q * (ATTN_DIM ** -0.5)).astype(BF16)
    k_ref[0] = k.astype(BF16)
    v_ref[0] = qkv[:, 2 * w:].astype(BF16)

    gcu = _dot(he, w_ref[:, 4 * w:])
    cu = gcu[:, :w] * gcu[:, w:]
    row = lax.broadcasted_iota(jnp.int32, (tm + 2 * HALO, 1), 0)
    first_row = jnp.where(i == 0, HALO, 0)
    end_row = jnp.where(i == last, HALO + tm, tm + 2 * HALO)
    cu = jnp.where((row >= first_row) & (row < end_row), cu, 0.0)
    prev = pltpu.roll(cu, 1, axis=0)[HALO:HALO + tm]
    nxt = pltpu.roll(cu, tm + 2 * HALO - 1, axis=0)[HALO:HALO + tm]
    cw = cw_ref[...]
    conv = prev * cw[0:1] + cu[HALO:HALO + tm] * cw[1:2] + nxt * cw[2:3]
    gate_b = _dot(hm, w_ref[:, 3 * w:4 * w])
    yc_ref[0] = (gate_b * conv).astype(BF16)


def _inproj(x, mods, g, w_in, bd, qg, kg, cos, sin, conv_w):
    b, s, d = x.shape
    tm = TM_PROJ
    nh = tm // HALO
    n_halo_blocks = s // HALO
    tok = pl.BlockSpec((1, tm, QK_W), lambda bi, i: (bi, i, 0))
    out_sds = jax.ShapeDtypeStruct((b, s, QK_W), BF16)
    return pl.pallas_call(
        _inproj_kernel,
        grid=(b, s // tm),
        in_specs=[
            pl.BlockSpec((1, tm, d), lambda bi, i: (bi, i, 0)),
            pl.BlockSpec((1, HALO, d), lambda bi, i: (bi, jnp.maximum(i * nh - 1, 0), 0)),
            pl.BlockSpec((1, HALO, d),
                         lambda bi, i: (bi, jnp.minimum((i + 1) * nh, n_halo_blocks - 1), 0)),
            _mod_spec(0, 1, lambda bi, i: bi),
            _mod_spec(0, 0, lambda bi, i: bi),
            _const_spec((1, d)),
            _const_spec(w_in.shape),
            _const_spec(bd.shape),
            _const_spec((1, QK_W)),
            _const_spec((1, QK_W)),
            pl.BlockSpec((tm, QK_W), lambda bi, i: (i, 0)),
            pl.BlockSpec((tm, QK_W), lambda bi, i: (i, 0)),
            _const_spec(conv_w.shape),
        ],
        out_specs=[tok, tok, tok, tok],
        out_shape=[out_sds, out_sds, out_sds, out_sds],
        compiler_params=_params(2),
        name="inproj",
    )(x, x, x, mods, mods, g, w_in, bd, qg, kg, cos, sin, conv_w)


def _ctxproj_kernel(x_ref, sc_ref, sh_ref, g_ref, wk_ref, wv_ref, bd_ref, kg_ref, k_ref, v_ref):
    hc = _modulate(x_ref[0], g_ref[...], sc_ref[0], sh_ref[0]).astype(BF16)
    k_ref[0] = _group_rms(_dot(hc, wk_ref[...]), bd_ref[...], kg_ref[...]).astype(BF16)
    v_ref[0] = _dot(hc, wv_ref[...]).astype(BF16)


def _ctxproj(ctx, mods, g, w_in, bd, kg, ctx_row):
    b, l, d = ctx.shape
    out = pl.BlockSpec((1, l, QK_W), lambda bi: (bi, 0, 0))
    out_sds = jax.ShapeDtypeStruct((b, l, QK_W), BF16)
    return pl.pallas_call(
        _ctxproj_kernel,
        grid=(b,),
        in_specs=[
            pl.BlockSpec((1, l, d), lambda bi: (bi, 0, 0)),
            _mod_spec(0, 1, lambda bi: ctx_row),
            _mod_spec(0, 0, lambda bi: ctx_row),
            _const_spec((1, d)),
            pl.BlockSpec((d, QK_W), lambda bi: (0, 1), pipeline_mode=pl.Buffered(1)),
            pl.BlockSpec((d, QK_W), lambda bi: (0, 2), pipeline_mode=pl.Buffered(1)),
            _const_spec(bd.shape),
            _const_spec((1, QK_W)),
        ],
        out_specs=[out, out],
        out_shape=[out_sds, out_sds],
        compiler_params=_params(1),
        name="ctxproj",
    )(ctx, mods, mods, g, w_in, w_in, bd, kg)


def _attn_kernel(q_ref, k_ref, kc_ref, v_ref, vc_ref, sg_ref, lq1_ref, lk1_ref, lq2_ref, lk2_ref,
                 o_ref, *, lambda_init):
    q = q_ref[0]
    k = k_ref[0]
    kc = kc_ref[0]
    lane = lax.broadcasted_iota(jnp.int32, (1, ATTN_VDIM), 1)
    first = lane < ATTN_DIM
    zero = jnp.zeros_like(q)
    lam = (jnp.exp(jnp.sum(lq1_ref[...] * lk1_ref[...], keepdims=True))
           - jnp.exp(jnp.sum(lq2_ref[...] * lk2_ref[...], keepdims=True)) + lambda_init)
    nt = (((1,), (1,)), ((), ()))

    def softmax_parts(qx):
        sl = lax.dot_general(qx, k, nt, preferred_element_type=F32)
        sc = lax.dot_general(qx, kc, nt, preferred_element_type=F32)
        m = jnp.maximum(jnp.max(sl, axis=-1, keepdims=True), jnp.max(sc, axis=-1, keepdims=True))
        el = jnp.exp(sl - m)
        ec = jnp.exp(sc - m)
        denom = jnp.sum(el, axis=-1, keepdims=True) + jnp.sum(ec, axis=-1, keepdims=True)
        return el, ec, 1.0 / denom

    e1l, e1c, inv1 = softmax_parts(jnp.where(first, q, zero))
    e2l, e2c, inv2 = softmax_parts(jnp.where(first, zero, q))
    c2 = lam * inv2
    wl = (e1l * inv1 - e2l * c2).astype(BF16)
    wc = (e1c * inv1 - e2c * c2).astype(BF16)
    o = _dot(wl, v_ref[0]) + _dot(wc, vc_ref[0])
    ms = jnp.mean(o * o, axis=-1, keepdims=True)
    y = (o * lax.rsqrt(ms + EPS)) * sg_ref[...]
    o_ref[0] = (y * (1.0 - lambda_init)).astype(BF16)


def _attention(q, k, kc, v, vc, subln_g, lq1, lk1, lq2, lk2, lambda_init):
    b, s, _ = q.shape
    l = kc.shape[1]
    hd = ATTN_VDIM
    vec = _const_spec((1, ATTN_DIM))
    return pl.pallas_call(
        functools.partial(_attn_kernel, lambda_init=lambda_init),
        grid=(b, ATTN_HEADS, s // TQ),
        in_specs=[
            pl.BlockSpec((1, TQ, hd), lambda bi, h, i: (bi, i, h)),
            pl.BlockSpec((1, s, hd), lambda bi, h, i: (bi, 0, h)),
            pl.BlockSpec((1, l, hd), lambda bi, h, i: (bi, 0, h)),
            pl.BlockSpec((1, s, hd), lambda bi, h, i: (bi, 0, h)),
            pl.BlockSpec((1, l, hd), lambda bi, h, i: (bi, 0, h)),
            _const_spec((1, hd)),
            vec, vec, vec, vec,
        ],
        out_specs=pl.BlockSpec((1, TQ, hd), lambda bi, h, i: (bi, i, h)),
        out_shape=jax.ShapeDtypeStruct((b, s, ATTN_HEADS * hd), BF16),
        compiler_params=_params(3),
        name="diff_attention",
    )(q, k, kc, v, vc, subln_g, lq1, lk1, lq2, lk2)


def _outproj_ffn_kernel(x_ref, at_ref, yc_ref, gm_ref, sf_ref, cf_ref, gf_ref, g_ref,
                        wo_ref, wg_ref, wu_ref, wd_ref, o_ref):
    half = wo_ref.shape[0] // 2
    y = _dot(at_ref[0], wo_ref[:half]) + _dot(yc_ref[0], wo_ref[half:])
    h1 = x_ref[0] + gm_ref[0] * y
    hf = _modulate(h1, g_ref[...], cf_ref[0], sf_ref[0]).astype(BF16)
    d_ff = wg_ref.shape[1]
    acc = jnp.zeros(h1.shape, F32)
    for f in range(d_ff // MXU_N):
        cols = slice(f * MXU_N, (f + 1) * MXU_N)
        a = jax.nn.silu(_dot(hf, wg_ref[:, cols])) * _dot(hf, wu_ref[:, cols])
        acc = acc + _dot(a.astype(BF16), wd_ref[cols, :])
    o_ref[0] = h1 + gf_ref[0] * acc


def _outproj_ffn(x, attn, yconv, mods, g, w_out, wg, wu, wd):
    b, s, d = x.shape
    tm = TM_FFN
    return pl.pallas_call(
        _outproj_ffn_kernel,
        grid=(b, s // tm),
        in_specs=[
            pl.BlockSpec((1, tm, d), lambda bi, i: (bi, i, 0)),
            pl.BlockSpec((1, tm, QK_W), lambda bi, i: (bi, i, 0)),
            pl.BlockSpec((1, tm, QK_W), lambda bi, i: (bi, i, 0)),
            _mod_spec(0, 2, lambda bi, i: bi),
            _mod_spec(0, 3, lambda bi, i: bi),
            _mod_spec(0, 4, lambda bi, i: bi),
            _mod_spec(0, 5, lambda bi, i: bi),
            _const_spec((1, d)),
            _const_spec(w_out.shape),
            _const_spec(wg.shape),
            _const_spec(wu.shape),
            _const_spec(wd.shape),
        ],
        out_specs=pl.BlockSpec((1, tm, d), lambda bi, i: (bi, i, 0)),
        out_shape=jax.ShapeDtypeStruct((b, s, d), F32),
        compiler_params=_params(2),
        name="outproj_ffn",
    )(x, attn, yconv, mods, mods, mods, mods, g, w_out, wg, wu, wd)


def _sgu_kernel(x_ref, sm_ref, cm_ref, gm_ref, sf_ref, cf_ref, g_ref, gff_ref, win_ref, lng_ref,
                lnb_ref, ws_ref, bs_ref, wo_ref, rwh_ref, rwl_ref, h_ref, hf_ref, lg_ref, us_ref):
    x = x_ref[0]
    tm, d = x.shape
    hm = _modulate(x, g_ref[...], cm_ref[0], sm_ref[0]).astype(BF16)
    z = jax.nn.gelu(_dot(hm, win_ref[...]))
    v = z[:, d:]
    mu = jnp.mean(v, axis=-1, keepdims=True)
    vc = v - mu
    var = jnp.mean(vc * vc, axis=-1, keepdims=True)
    vn = ((vc * lax.rsqrt(var + EPS)) * lng_ref[...] + lnb_ref[...]).astype(BF16)
    gw = d // SGU_GROUPS
    for n in range(tm // CHUNK):
        rows = slice(n * CHUNK, (n + 1) * CHUNK)
        for gi in range(SGU_GROUPS):
            cols = slice(gi * gw, (gi + 1) * gw)
            s = _dot(ws_ref[gi], vn[rows, cols]) + bs_ref[:, cols]
            us_ref[rows, cols] = (z[rows, cols] * s).astype(BF16)
    h = x + gm_ref[0] * _dot(us_ref[...], wo_ref[...])
    h_ref[0] = h
    hf = _modulate(h, gff_ref[...], cf_ref[0], sf_ref[0])
    hf_ref[0] = hf
    hi, lo = _split_bf16(hf)
    lg_ref[0] = _dot(hi, rwh_ref[...]) + (_dot(hi, rwl_ref[...]) + _dot(lo, rwh_ref[...]))


def _sgu(h, mods, g_mix, g_ffn, w_in, ln_g, ln_b, w_s, bias, w_out, rw_hi, rw_lo):
    b, s, d = h.shape
    tm = TM_SGU
    tok = pl.BlockSpec((1, tm, d), lambda bi, i: (bi, i, 0))
    return pl.pallas_call(
        _sgu_kernel,
        grid=(b, s // tm),
        in_specs=[
            tok,
            _mod_spec(1, 0, lambda bi, i: bi),
            _mod_spec(1, 1, lambda bi, i: bi),
            _mod_spec(1, 2, lambda bi, i: bi),
            _mod_spec(1, 3, lambda bi, i: bi),
            _mod_spec(1, 4, lambda bi, i: bi),
            _const_spec((1, d)),
            _const_spec((1, d)),
            _const_spec(w_in.shape),
            _const_spec((1, d)),
            _const_spec((1, d)),
            _const_spec(w_s.shape),
            _const_spec(bias.shape),
            _const_spec(w_out.shape),
            _const_spec(rw_hi.shape),
            _const_spec(rw_lo.shape),
        ],
        out_specs=[tok, tok, pl.BlockSpec((1, tm, N_EXPERTS), lambda bi, i: (bi, i, 0))],
        out_shape=[jax.ShapeDtypeStruct((b, s, d), F32), jax.ShapeDtypeStruct((b, s, d), F32),
                   jax.ShapeDtypeStruct((b, s, N_EXPERTS), F32)],
        scratch_shapes=[pltpu.VMEM((tm, d), BF16)],
        compiler_params=_params(2),
        name="sgu",
    )(h, mods, mods, mods, mods, mods, g_mix, g_ffn, w_in, ln_g, ln_b, w_s, bias, w_out,
      rw_hi, rw_lo)


def _route(logits, tile_rows, n_tiles):
    t = logits.shape[0]
    eid = lax.broadcasted_iota(jnp.int32, (t, N_EXPERTS), 1)
    i1 = jnp.argmax(logits, axis=-1).astype(jnp.int32)
    v1 = jnp.max(logits, axis=-1)
    rest = jnp.where(eid == i1[:, None], -jnp.inf, logits)
    i2 = jnp.argmax(rest, axis=-1).astype(jnp.int32)
    v2 = jnp.max(rest, axis=-1)
    e2 = jnp.exp(v2 - v1)
    gates = jnp.stack([1.0 / (1.0 + e2), e2 / (1.0 + e2)], axis=-1)
    onehot = ((eid == i1[:, None]) | (eid == i2[:, None])).astype(jnp.int32)
    csum = jnp.cumsum(onehot, axis=0)
    rank = csum - onehot
    counts = csum[-1]
    padded = ((counts + tile_rows - 1) // tile_rows) * tile_rows
    ends = jnp.cumsum(padded)
    starts = ends - padded
    idx = jnp.stack([i1, i2], axis=-1)
    pos = (starts[idx] + jnp.take_along_axis(rank, idx, axis=-1)).astype(jnp.int32)
    tile_start = jnp.arange(n_tiles, dtype=jnp.int32) * tile_rows
    valid = tile_start < ends[-1]
    tile_e = jnp.sum(tile_start[:, None] >= ends[None, :], axis=-1).astype(jnp.int32)
    last_e = jnp.max(jnp.where(counts > 0, jnp.arange(N_EXPERTS, dtype=jnp.int32), 0))
    tile_e = jnp.where(valid, tile_e, last_e)
    return pos.reshape(-1), gates, tile_e, valid.astype(jnp.int32)


def _scatter_kernel(pos_ref, hf_ref, xs_in_ref, xs_ref, sem):
    del xs_in_ref
    base = pl.program_id(0) * TG

    def copy(r, kk):
        p = pos_ref[(base + r) * TOP_K + kk]
        return pltpu.make_async_copy(hf_ref.at[pl.ds(r, 1)], xs_ref.at[pl.ds(p, 1)], sem)

    def issue(r, c):
        for kk in range(TOP_K):
            copy(r, kk).start()
        return c

    def drain(r, c):
        for kk in range(TOP_K):
            copy(r, kk).wait()
        return c

    lax.fori_loop(0, TG, issue, 0)
    lax.fori_loop(0, TG, drain, 0)


def _scatter_rows(pos, hf, n_rows):
    t, d = hf.shape
    xs0 = jnp.zeros((n_rows, d), F32)
    return pl.pallas_call(
        _scatter_kernel,
        grid_spec=pltpu.PrefetchScalarGridSpec(
            num_scalar_prefetch=1,
            grid=(t // TG,),
            in_specs=[pl.BlockSpec((TG, d), lambda i, pos: (i, 0)),
                      pl.BlockSpec(memory_space=pl.ANY)],
            out_specs=pl.BlockSpec(memory_space=pl.ANY),
            scratch_shapes=[pltpu.SemaphoreType.DMA(())],
        ),
        out_shape=jax.ShapeDtypeStruct((n_rows, d), F32),
        input_output_aliases={2: 0},
        compiler_params=_params(1),
        name="moe_scatter",
    )(pos, hf, xs0)


def _expert_kernel(te_ref, tv_ref, x_ref, wg_ref, wu_ref, wd_ref, y_ref, xb_ref, acc_ref):
    del te_ref
    j = pl.program_id(0)
    f = pl.program_id(1)
    nf = pl.num_programs(1)
    valid = tv_ref[j] > 0

    @pl.when(jnp.logical_and(valid, f == 0))
    def _():
        xb_ref[...] = x_ref[...].astype(BF16)
        acc_ref[...] = jnp.zeros_like(acc_ref)

    @pl.when(valid)
    def _():
        xb = xb_ref[...]
        a = jax.nn.silu(_dot(xb, wg_ref[0])) * _dot(xb, wu_ref[0])
        acc_ref[...] += _dot(a.astype(BF16), wd_ref[0])

    @pl.when(jnp.logical_and(valid, f == nf - 1))
    def _():
        y_ref[...] = acc_ref[...]

    @pl.when(jnp.logical_and(jnp.logical_not(valid), f == nf - 1))
    def _():
        y_ref[...] = jnp.zeros_like(y_ref)


def _expert_ffn(tile_e, tile_valid, xs, wg, wu, wd):
    n_rows, d = xs.shape
    n_tiles = n_rows // TM_EXP
    d_ff = wg.shape[2]
    nf = d_ff // TF_EXP

    def f_of(f, tv, j):
        return jnp.where(tv[j] > 0, f, nf - 1)

    return pl.pallas_call(
        _expert_kernel,
        grid_spec=pltpu.PrefetchScalarGridSpec(
            num_scalar_prefetch=2,
            grid=(n_tiles, nf),
            in_specs=[
                pl.BlockSpec((TM_EXP, d), lambda j, f, te, tv: (j, 0)),
                pl.BlockSpec((1, d, TF_EXP), lambda j, f, te, tv: (te[j], 0, f_of(f, tv, j))),
                pl.BlockSpec((1, d, TF_EXP), lambda j, f, te, tv: (te[j], 0, f_of(f, tv, j))),
                pl.BlockSpec((1, TF_EXP, d), lambda j, f, te, tv: (te[j], f_of(f, tv, j), 0)),
            ],
            out_specs=pl.BlockSpec((TM_EXP, d), lambda j, f, te, tv: (j, 0)),
            scratch_shapes=[pltpu.VMEM((TM_EXP, d), BF16), pltpu.VMEM((TM_EXP, d), F32)],
        ),
        out_shape=jax.ShapeDtypeStruct((n_rows, d), F32),
        compiler_params=_params(2),
        name="moe_experts",
    )(tile_e, tile_valid, xs, wg, wu, wd)


def _combine_kernel(pos_ref, y_ref, h_ref, gate_ref, gf_ref, o_ref, buf_ref, sem):
    base = pl.program_id(0) * TG

    def copy(r, kk):
        p = pos_ref[(base + r) * TOP_K + kk]
        return pltpu.make_async_copy(y_ref.at[pl.ds(p, 1)], buf_ref.at[kk, pl.ds(r, 1)], sem)

    def issue(r, c):
        for kk in range(TOP_K):
            copy(r, kk).start()
        return c

    def drain(r, c):
        for kk in range(TOP_K):
            copy(r, kk).wait()
        return c

    lax.fori_loop(0, TG, issue, 0)
    lax.fori_loop(0, TG, drain, 0)
    gates = gate_ref[...]
    mix = gates[:, 0:1] * buf_ref[0] + gates[:, 1:2] * buf_ref[1]
    o_ref[...] = h_ref[...] + gf_ref[0] * mix


def _combine_rows(pos, y, h, gates, mods, seq):
    t, d = h.shape
    steps_per_batch = seq // TG
    return pl.pallas_call(
        _combine_kernel,
        grid_spec=pltpu.PrefetchScalarGridSpec(
            num_scalar_prefetch=1,
            grid=(t // TG,),
            in_specs=[
                pl.BlockSpec(memory_space=pl.ANY),
                pl.BlockSpec((TG, d), lambda i, pos: (i, 0)),
                pl.BlockSpec((TG, TOP_K), lambda i, pos: (i, 0)),
                pl.BlockSpec((1, 1, d),
                             lambda i, pos: (COND_ROWS + i // steps_per_batch, 0, N_MODS - 1)),
            ],
            out_specs=pl.BlockSpec((TG, d), lambda i, pos: (i, 0)),
            scratch_shapes=[pltpu.VMEM((TOP_K, TG, d), F32), pltpu.SemaphoreType.DMA(())],
        ),
        out_shape=jax.ShapeDtypeStruct((t, d), F32),
        compiler_params=_params(1),
        name="moe_combine",
    )(pos, y, h, gates, mods)


def _rope_tables(seq):
    rows = seq // GRID_W
    row = jnp.repeat(jnp.arange(rows), GRID_W).astype(F32)
    col = jnp.tile(jnp.arange(GRID_W), rows).astype(F32)
    quarter = ATTN_DIM // 4
    inv = ROPE_BASE ** (-jnp.arange(quarter, dtype=F32) / quarter)
    ar = row[:, None] * inv
    ac = col[:, None] * inv
    ang = jnp.concatenate([ar, ar, ac, ac], axis=-1)
    sign = jnp.tile(jnp.concatenate([-jnp.ones(quarter, F32), jnp.ones(quarter, F32)]), 2)
    reps = QK_W // ATTN_DIM
    return jnp.tile(jnp.cos(ang), (1, reps)), jnp.tile(jnp.sin(ang) * sign, (1, reps))


def kernel(x, c, ctx, c_ctx, ada_w, ada_b, norm_mix_g, norm_ffn_g, w_in_even, q_norm_g, k_norm_g,
           lam_q1, lam_k1, lam_q2, lam_k2, subln_g, conv_w, w_out_even, ffn_w_gate, ffn_w_up,
           ffn_w_down, sgu_w_in, sgu_ln_g, sgu_ln_b, sgu_w_s, sgu_b_s, sgu_w_out, router_w,
           moe_w_gate, moe_w_up, moe_w_down):
    b, s, d = x.shape
    assert d == D_MODEL and b < COND_ROWS and ada_w.shape[0] == 2
    assert s % TM_PROJ == 0 and s % TQ == 0 and s % TM_FFN == 0 and s % TM_SGU == 0 and s % TG == 0

    cond = jnp.zeros((COND_ROWS, d), F32).at[:b].set(c).at[b].set(c_ctx)
    mods = _ada_mods(cond, ada_w, ada_b)

    lambda_init = 0.8 - 0.6 * math.exp(-0.3 * 0)
    group = jnp.arange(QK_W) // ATTN_DIM
    bd = (group[:, None] == group[None, :]).astype(BF16)
    reps = QK_W // ATTN_DIM
    qg = jnp.tile(q_norm_g[0], reps)[None, :]
    kg = jnp.tile(k_norm_g[0], reps)[None, :]
    cos, sin = _rope_tables(s)
    w_in = w_in_even[0].astype(BF16)
    g_mix0 = norm_mix_g[0][None, :]
    q, k, v, yconv = _inproj(x, mods, g_mix0, w_in, bd, qg, kg, cos, sin, conv_w[0])
    kc, vc = _ctxproj(ctx, mods, g_mix0, w_in, bd, kg, ctx_row=b)
    attn = _attention(q, k, kc, v, vc, subln_g[0][None, :], lam_q1[0][None, :], lam_k1[0][None, :],
                      lam_q2[0][None, :], lam_k2[0][None, :], lambda_init)
    h = _outproj_ffn(x, attn, yconv, mods, norm_ffn_g[0][None, :], w_out_even[0].astype(BF16),
                     ffn_w_gate[0].astype(BF16), ffn_w_up[0].astype(BF16),
                     ffn_w_down[0].astype(BF16))

    gw = d // SGU_GROUPS
    bias = jnp.repeat(sgu_b_s[0].T, gw, axis=1)
    rw_hi = router_w[0].astype(BF16)
    rw_lo = (router_w[0] - rw_hi.astype(F32)).astype(BF16)
    h, hf, logits = _sgu(h, mods, norm_mix_g[1][None, :], norm_ffn_g[1][None, :],
                         sgu_w_in[0].astype(BF16), sgu_ln_g[0][None, :], sgu_ln_b[0][None, :],
                         sgu_w_s[0].astype(BF16), bias, sgu_w_out[0].astype(BF16), rw_hi, rw_lo)

    t = b * s
    n_tiles = (t * TOP_K) // TM_EXP + N_EXPERTS
    pos, gates, tile_e, tile_valid = _route(logits.reshape(t, N_EXPERTS), TM_EXP, n_tiles)
    xs = _scatter_rows(pos, hf.reshape(t, d), n_tiles * TM_EXP)
    y = _expert_ffn(tile_e, tile_valid, xs, moe_w_gate[0].astype(BF16), moe_w_up[0].astype(BF16),
                    moe_w_down[0].astype(BF16))
    out = _combine_rows(pos, y, h.reshape(t, d), gates, mods, s)
    return out.reshape(b, s, d)
```

```python
import functools
import math

import jax
import jax.numpy as jnp
from jax import lax
from jax.experimental import pallas as pl
from jax.experimental.pallas import tpu as pltpu

F32 = jnp.float32
BF16 = jnp.bfloat16

D_MODEL = 1024
GRID_W = 64
ATTN_HEADS = 4
ATTN_DIM = 64
ATTN_VDIM = 2 * ATTN_DIM
QK_W = ATTN_HEADS * 2 * ATTN_DIM
CONV_K = 3
ROPE_BASE = 10000.0
SGU_GROUPS = 4
CHUNK = 128
N_EXPERTS = 8
TOP_K = 2
EPS = 1e-6
LOG2_E = math.log2(math.e)
N_MODS = 6
COND_ROWS = 16

V7X_VMEM_LIMIT = 56 * 1024 * 1024
MXU_N = 256
TM_PROJ = 512
HALO = 16
TQ = 2048
ATTN_RB = 256
ATTN_SLOTS = 8
ATTN_KC = 256
TM_FFN = 512
TM_SGU = 512
TM_EXP = 512
TF_EXP = 1792
TG = 256


def _params(n_axes):
    return pltpu.CompilerParams(
        dimension_semantics=("arbitrary",) * n_axes, vmem_limit_bytes=V7X_VMEM_LIMIT)


def _const_spec(shape):
    zeros = (0,) * len(shape)
    return pl.BlockSpec(shape, lambda *_: zeros, pipeline_mode=pl.Buffered(1))


def _dot(a, b):
    return jnp.dot(a, b, preferred_element_type=F32)


def _modulate(x, g, scale, shift):
    ms = jnp.mean(x * x, axis=-1, keepdims=True)
    y = x * lax.rsqrt(ms + EPS)
    return (y * g) * (1.0 + scale) + shift


def _split_bf16(x):
    hi = x.astype(BF16)
    lo = (x - hi.astype(F32)).astype(BF16)
    return hi, lo


def _ada_kernel(cond_ref, w_ref, b_ref, o_ref):
    a = jax.nn.silu(cond_ref[...]).astype(BF16)
    o_ref[0] = _dot(a, w_ref[0].astype(BF16)) + b_ref[0]


def _ada_mods(cond, ada_w, ada_b):
    depth = ada_w.shape[0]
    d = D_MODEL
    out = pl.pallas_call(
        _ada_kernel,
        grid=(depth, N_MODS),
        in_specs=[
            pl.BlockSpec((COND_ROWS, d), lambda l, n: (0, 0)),
            pl.BlockSpec((1, d, d), lambda l, n: (l, 0, n)),
            pl.BlockSpec((1, 1, d), lambda l, n: (l, 0, n)),
        ],
        out_specs=pl.BlockSpec((1, COND_ROWS, d), lambda l, n: (l, 0, n)),
        out_shape=jax.ShapeDtypeStruct((depth, COND_ROWS, N_MODS * d), F32),
        compiler_params=_params(2),
        name="ada_mods",
    )(cond, ada_w, ada_b.reshape(depth, 1, N_MODS * d))
    return out.reshape(depth * COND_ROWS, 1, N_MODS * d)


def _mod_spec(layer, chunk, row_of):
    return pl.BlockSpec((1, 1, D_MODEL),
                        lambda *ids: (layer * COND_ROWS + row_of(*ids), 0, chunk))


def _group_rms(t, bd, gain):
    hi, lo = _split_bf16(t * t)
    ss = _dot(hi, bd) + _dot(lo, bd)
    return (t * lax.rsqrt(ss * (1.0 / ATTN_DIM) + EPS)) * gain


def _rope(t, cos, sin_signed, first_half):
    width = t.shape[-1]
    quarter = ATTN_DIM // 4
    ahead = pltpu.roll(t, width - quarter, axis=1)
    behind = pltpu.roll(t, quarter, axis=1)
    return t * cos + jnp.where(first_half, ahead, behind) * sin_signed


def _with_ones_columns(v):
    rows = v.shape[0]
    lane = lax.broadcasted_iota(jnp.int32, (rows, ATTN_VDIM), 1)
    ones_col = jnp.where(lane == 0, 1.0, 0.0).astype(v.dtype)
    parts = []
    for h in range(ATTN_HEADS):
        parts += [v[:, h * ATTN_VDIM:(h + 1) * ATTN_VDIM], ones_col]
    return jnp.concatenate(parts, axis=1)


def _inproj_kernel(x_ref, xp_ref, xn_ref, sc_ref, sh_ref, g_ref, w_ref, bd_ref, qg_ref, kg_ref,
                   cos_ref, sin_ref, cw_ref, q_ref, k_ref, v_ref, yc_ref):
    i = pl.program_id(1)
    last = pl.num_programs(1) - 1
    tm = x_ref.shape[1]
    w = QK_W
    xe = jnp.concatenate([xp_ref[0], x_ref[0], xn_ref[0]], axis=0)
    he = _modulate(xe, g_ref[...], sc_ref[0], sh_ref[0]).astype(BF16)
    hm = he[HALO:HALO + tm]

    qkv = _dot(hm, w_ref[:, :3 * w])
    bd = bd_ref[...]
    cos = cos_ref[...]
    sin = sin_ref[...]
    lane = lax.broadcasted_iota(jnp.int32, (1, w), 1)
    first_half = (lane % (ATTN_DIM // 2)) < (ATTN_DIM // 4)
    q = _rope(_group_rms(qkv[:, :w], bd, qg_ref[...]), cos, sin, first_half)
    k = _rope(_group_rms(qkv[:, w:2 * w], bd, kg_ref[...]), cos, sin, first_half)
    q_ref[0] = (q * (ATTN_DIM ** -0.5 * LOG2_E)).astype(BF16)
    k_ref[0] = k.astype(BF16)
    v_ref[0] = _with_ones_columns(qkv[:, 2 * w:].astype(BF16))

    gcu = _dot(he, w_ref[:, 4 * w:])
    cu = gcu[:, :w] * gcu[:, w:]
    row = lax.broadcasted_iota(jnp.int32, (tm + 2 * HALO, 1), 0)
    first_row = jnp.where(i == 0, HALO, 0)
    end_row = jnp.where(i == last, HALO + tm, tm + 2 * HALO)
    cu = jnp.where((row >= first_row) & (row < end_row), cu, 0.0)
    prev = pltpu.roll(cu, 1, axis=0)[HALO:HALO + tm]
    nxt = pltpu.roll(cu, tm + 2 * HALO - 1, axis=0)[HALO:HALO + tm]
    cw = cw_ref[...]
    conv = prev * cw[0:1] + cu[HALO:HALO + tm] * cw[1:2] + nxt * cw[2:3]
    gate_b = _dot(hm, w_ref[:, 3 * w:4 * w])
    yc_ref[0] = (gate_b * conv).astype(BF16)


def _inproj(x, mods, g, w_in, bd, qg, kg, cos, sin, conv_w):
    b, s, d = x.shape
    tm = TM_PROJ
    nh = tm // HALO
    n_halo_blocks = s // HALO
    tok = pl.BlockSpec((1, tm, QK_W), lambda bi, i: (bi, i, 0))
    out_sds = jax.ShapeDtypeStruct((b, s, QK_W), BF16)
    return pl.pallas_call(
        _inproj_kernel,
        grid=(b, s // tm),
        in_specs=[
            pl.BlockSpec((1, tm, d), lambda bi, i: (bi, i, 0)),
            pl.BlockSpec((1, HALO, d), lambda bi, i: (bi, jnp.maximum(i * nh - 1, 0), 0)),
            pl.BlockSpec((1, HALO, d),
                         lambda bi, i: (bi, jnp.minimum((i + 1) * nh, n_halo_blocks - 1), 0)),
            _mod_spec(0, 1, lambda bi, i: bi),
            _mod_spec(0, 0, lambda bi, i: bi),
            _const_spec((1, d)),
            _const_spec(w_in.shape),
            _const_spec(bd.shape),
            _const_spec((1, QK_W)),
            _const_spec((1, QK_W)),
            pl.BlockSpec((tm, QK_W), lambda bi, i: (i, 0)),
            pl.BlockSpec((tm, QK_W), lambda bi, i: (i, 0)),
            _const_spec(conv_w.shape),
        ],
        out_specs=[tok, tok, pl.BlockSpec((1, tm, 2 * QK_W), lambda bi, i: (bi, i, 0)), tok],
        out_shape=[out_sds, out_sds, jax.ShapeDtypeStruct((b, s, 2 * QK_W), BF16), out_sds],
        compiler_params=_params(2),
        name="inproj",
    )(x, x, x, mods, mods, g, w_in, bd, qg, kg, cos, sin, conv_w)


def _ctxproj_kernel(x_ref, sc_ref, sh_ref, g_ref, wk_ref, wv_ref, bd_ref, kg_ref, k_ref, v_ref):
    hc = _modulate(x_ref[0], g_ref[...], sc_ref[0], sh_ref[0]).astype(BF16)
    k_ref[0] = _group_rms(_dot(hc, wk_ref[...]), bd_ref[...], kg_ref[...]).astype(BF16)
    v_ref[0] = _with_ones_columns(_dot(hc, wv_ref[...]).astype(BF16))


def _ctxproj(ctx, mods, g, w_in, bd, kg, ctx_row):
    b, l, d = ctx.shape
    out = pl.BlockSpec((1, l, QK_W), lambda bi: (bi, 0, 0))
    out_sds = jax.ShapeDtypeStruct((b, l, QK_W), BF16)
    return pl.pallas_call(
        _ctxproj_kernel,
        grid=(b,),
        in_specs=[
            pl.BlockSpec((1, l, d), lambda bi: (bi, 0, 0)),
            _mod_spec(0, 1, lambda bi: ctx_row),
            _mod_spec(0, 0, lambda bi: ctx_row),
            _const_spec((1, d)),
            pl.BlockSpec((d, QK_W), lambda bi: (0, 1), pipeline_mode=pl.Buffered(1)),
            pl.BlockSpec((d, QK_W), lambda bi: (0, 2), pipeline_mode=pl.Buffered(1)),
            _const_spec(bd.shape),
            _const_spec((1, QK_W)),
        ],
        out_specs=[out, pl.BlockSpec((1, l, 2 * QK_W), lambda bi: (bi, 0, 0))],
        out_shape=[out_sds, jax.ShapeDtypeStruct((b, l, 2 * QK_W), BF16)],
        compiler_params=_params(1),
        name="ctxproj",
    )(ctx, mods, mods, g, w_in, w_in, bd, kg)


def _attn_kernel(q_ref, k_ref, kc_ref, v_ref, vc_ref, sg_ref, lq1_ref, lk1_ref, lq2_ref, lk2_ref,
                 o_ref, s_ref, e_ref, *, lambda_init):
    tq = q_ref.shape[1]
    n_ctx = kc_ref.shape[1] // ATTN_KC
    n_lat = k_ref.shape[1] // ATTN_KC
    chunks = [(kc_ref, vc_ref, c) for c in range(n_ctx)] + [(k_ref, v_ref, c) for c in range(n_lat)]
    lane = lax.broadcasted_iota(jnp.int32, (1, ATTN_VDIM), 1)
    first = lane < ATTN_DIM
    lam = (jnp.exp(jnp.sum(lq1_ref[...] * lk1_ref[...], keepdims=True))
           - jnp.exp(jnp.sum(lq2_ref[...] * lk2_ref[...], keepdims=True)) + lambda_init)
    nt = (((1,), (1,)), ((), ()))

    for rb in range(tq // ATTN_RB):
        rows = slice(rb * ATTN_RB, (rb + 1) * ATTN_RB)
        q = q_ref[0, rows, :]
        zero = jnp.zeros_like(q)
        outs = []
        for part in range(2):
            qx = jnp.where(first, q, zero) if part == 0 else jnp.where(first, zero, q)
            slot = (rb * 2 + part) % ATTN_SLOTS
            mpart = None
            for ci, (kr, _, c) in enumerate(chunks):
                sc = lax.dot_general(qx, kr[0, c * ATTN_KC:(c + 1) * ATTN_KC, :], nt,
                                     preferred_element_type=F32)
                s_ref[slot, ci] = sc
                cm = sc[:, :ATTN_VDIM]
                for j in range(1, ATTN_KC // ATTN_VDIM):
                    cm = jnp.maximum(cm, sc[:, j * ATTN_VDIM:(j + 1) * ATTN_VDIM])
                mpart = cm if mpart is None else jnp.maximum(mpart, cm)
            m = jnp.max(mpart, axis=-1, keepdims=True)
            for ci in range(len(chunks)):
                e_ref[slot, :, ci * ATTN_KC:(ci + 1) * ATTN_KC] = (
                    jnp.exp2(s_ref[slot, ci] - m).astype(BF16))
            n_c = n_ctx * ATTN_KC
            acc = _dot(e_ref[slot, :, :n_c], vc_ref[0]) + _dot(e_ref[slot, :, n_c:], v_ref[0])
            outs.append(acc[:, :ATTN_VDIM] * (1.0 / acc[:, ATTN_VDIM:ATTN_VDIM + 1]))
        o = outs[0] - lam * outs[1]
        ms = jnp.mean(o * o, axis=-1, keepdims=True)
        y = (o * lax.rsqrt(ms + EPS)) * sg_ref[...]
        o_ref[0, rows, :] = (y * (1.0 - lambda_init)).astype(BF16)


def _attention(q, k, kc, v, vc, subln_g, lq1, lk1, lq2, lk2, lambda_init):
    b, s, _ = q.shape
    l = kc.shape[1]
    hd = ATTN_VDIM
    va = 2 * ATTN_VDIM
    assert l % ATTN_KC == 0 and s % ATTN_KC == 0
    n_chunks = (l + s) // ATTN_KC
    vec = _const_spec((1, ATTN_DIM))
    return pl.pallas_call(
        functools.partial(_attn_kernel, lambda_init=lambda_init),
        grid=(b, ATTN_HEADS, s // TQ),
        in_specs=[
            pl.BlockSpec((1, TQ, hd), lambda bi, h, i: (bi, i, h)),
            pl.BlockSpec((1, s, hd), lambda bi, h, i: (bi, 0, h)),
            pl.BlockSpec((1, l, hd), lambda bi, h, i: (bi, 0, h)),
            pl.BlockSpec((1, s, va), lambda bi, h, i: (bi, 0, h)),
            pl.BlockSpec((1, l, va), lambda bi, h, i: (bi, 0, h)),
            _const_spec((1, hd)),
            vec, vec, vec, vec,
        ],
        out_specs=pl.BlockSpec((1, TQ, hd), lambda bi, h, i: (bi, i, h)),
        out_shape=jax.ShapeDtypeStruct((b, s, ATTN_HEADS * hd), BF16),
        scratch_shapes=[pltpu.VMEM((ATTN_SLOTS, n_chunks, ATTN_RB, ATTN_KC), F32),
                        pltpu.VMEM((ATTN_SLOTS, ATTN_RB, l + s), BF16)],
        compiler_params=_params(3),
        name="diff_attention",
    )(q, k, kc, v, vc, subln_g, lq1, lk1, lq2, lk2)


def _outproj_ffn_kernel(x_ref, at_ref, yc_ref, gm_ref, sf_ref, cf_ref, gf_ref, g_ref,
                        wo_ref, wg_ref, wu_ref, wd_ref, o_ref):
    half = wo_ref.shape[0] // 2
    y = _dot(at_ref[0], wo_ref[:half]) + _dot(yc_ref[0], wo_ref[half:])
    h1 = x_ref[0] + gm_ref[0] * y
    hf = _modulate(h1, g_ref[...], cf_ref[0], sf_ref[0]).astype(BF16)
    d_ff = wg_ref.shape[1]
    acc = jnp.zeros(h1.shape, F32)
    for f in range(d_ff // MXU_N):
        cols = slice(f * MXU_N, (f + 1) * MXU_N)
        a = jax.nn.silu(_dot(hf, wg_ref[:, cols])) * _dot(hf, wu_ref[:, cols])
        acc = acc + _dot(a.astype(BF16), wd_ref[cols, :])
    o_ref[0] = h1 + gf_ref[0] * acc


def _outproj_ffn(x, attn, yconv, mods, g, w_out, wg, wu, wd):
    b, s, d = x.shape
    tm = TM_FFN
    return pl.pallas_call(
        _outproj_ffn_kernel,
        grid=(b, s // tm),
        in_specs=[
            pl.BlockSpec((1, tm, d), lambda bi, i: (bi, i, 0)),
            pl.BlockSpec((1, tm, QK_W), lambda bi, i: (bi, i, 0)),
            pl.BlockSpec((1, tm, QK_W), lambda bi, i: (bi, i, 0)),
            _mod_spec(0, 2, lambda bi, i: bi),
            _mod_spec(0, 3, lambda bi, i: bi),
            _mod_spec(0, 4, lambda bi, i: bi),
            _mod_spec(0, 5, lambda bi, i: bi),
            _const_spec((1, d)),
            _const_spec(w_out.shape),
            _const_spec(wg.shape),
            _const_spec(wu.shape),
            _const_spec(wd.shape),
        ],
        out_specs=pl.BlockSpec((1, tm, d), lambda bi, i: (bi, i, 0)),
        out_shape=jax.ShapeDtypeStruct((b, s, d), F32),
        compiler_params=_params(2),
        name="outproj_ffn",
    )(x, attn, yconv, mods, mods, mods, mods, g, w_out, wg, wu, wd)


def _sgu_kernel(x_ref, sm_ref, cm_ref, gm_ref, sf_ref, cf_ref, g_ref, gff_ref, win_ref, lng_ref,
                lnb_ref, ws_ref, bs_ref, wo_ref, rwh_ref, rwl_ref, h_ref, hf_ref, lg_ref, us_ref):
    x = x_ref[0]
    tm, d = x.shape
    hm = _modulate(x, g_ref[...], cm_ref[0], sm_ref[0]).astype(BF16)
    z = jax.nn.gelu(_dot(hm, win_ref[...]))
    v = z[:, d:]
    mu = jnp.mean(v, axis=-1, keepdims=True)
    vc = v - mu
    var = jnp.mean(vc * vc, axis=-1, keepdims=True)
    vn = ((vc * lax.rsqrt(var + EPS)) * lng_ref[...] + lnb_ref[...]).astype(BF16)
    gw = d // SGU_GROUPS
    for n in range(tm // CHUNK):
        rows = slice(n * CHUNK, (n + 1) * CHUNK)
        for gi in range(SGU_GROUPS):
            cols = slice(gi * gw, (gi + 1) * gw)
            s = _dot(ws_ref[gi], vn[rows, cols]) + bs_ref[:, cols]
            us_ref[rows, cols] = (z[rows, cols] * s).astype(BF16)
    h = x + gm_ref[0] * _dot(us_ref[...], wo_ref[...])
    h_ref[0] = h
    hf = _modulate(h, gff_ref[...], cf_ref[0], sf_ref[0])
    hf_ref[0] = hf
    hi, lo = _split_bf16(hf)
    lg_ref[0] = _dot(hi, rwh_ref[...]) + (_dot(hi, rwl_ref[...]) + _dot(lo, rwh_ref[...]))


def _sgu(h, mods, g_mix, g_ffn, w_in, ln_g, ln_b, w_s, bias, w_out, rw_hi, rw_lo):
    b, s, d = h.shape
    tm = TM_SGU
    tok = pl.BlockSpec((1, tm, d), lambda bi, i: (bi, i, 0))
    return pl.pallas_call(
        _sgu_kernel,
        grid=(b, s // tm),
        in_specs=[
            tok,
            _mod_spec(1, 0, lambda bi, i: bi),
            _mod_spec(1, 1, lambda bi, i: bi),
            _mod_spec(1, 2, lambda bi, i: bi),
            _mod_spec(1, 3, lambda bi, i: bi),
            _mod_spec(1, 4, lambda bi, i: bi),
            _const_spec((1, d)),
            _const_spec((1, d)),
            _const_spec(w_in.shape),
            _const_spec((1, d)),
            _const_spec((1, d)),
            _const_spec(w_s.shape),
            _const_spec(bias.shape),
            _const_spec(w_out.shape),
            _const_spec(rw_hi.shape),
            _const_spec(rw_lo.shape),
        ],
        out_specs=[tok, tok, pl.BlockSpec((1, tm, N_EXPERTS), lambda bi, i: (bi, i, 0))],
        out_shape=[jax.ShapeDtypeStruct((b, s, d), F32), jax.ShapeDtypeStruct((b, s, d), F32),
                   jax.ShapeDtypeStruct((b, s, N_EXPERTS), F32)],
        scratch_shapes=[pltpu.VMEM((tm, d), BF16)],
        compiler_params=_params(2),
        name="sgu",
    )(h, mods, mods, mods, mods, mods, g_mix, g_ffn, w_in, ln_g, ln_b, w_s, bias, w_out,
      rw_hi, rw_lo)


def _route(logits, tile_rows, n_tiles):
    t = logits.shape[0]
    eid = lax.broadcasted_iota(jnp.int32, (t, N_EXPERTS), 1)
    i1 = jnp.argmax(logits, axis=-1).astype(jnp.int32)
    v1 = jnp.max(logits, axis=-1)
    rest = jnp.where(eid == i1[:, None], -jnp.inf, logits)
    i2 = jnp.argmax(rest, axis=-1).astype(jnp.int32)
    v2 = jnp.max(rest, axis=-1)
    e2 = jnp.exp(v2 - v1)
    gates = jnp.stack([1.0 / (1.0 + e2), e2 / (1.0 + e2)], axis=-1)
    onehot = ((eid == i1[:, None]) | (eid == i2[:, None])).astype(jnp.int32)
    csum = jnp.cumsum(onehot, axis=0)
    rank = csum - onehot
    counts = csum[-1]
    padded = ((counts + tile_rows - 1) // tile_rows) * tile_rows
    ends = jnp.cumsum(padded)
    starts = ends - padded
    idx = jnp.stack([i1, i2], axis=-1)
    pos = (starts[idx] + jnp.take_along_axis(rank, idx, axis=-1)).astype(jnp.int32)
    tile_start = jnp.arange(n_tiles, dtype=jnp.int32) * tile_rows
    valid = tile_start < ends[-1]
    tile_e = jnp.sum(tile_start[:, None] >= ends[None, :], axis=-1).astype(jnp.int32)
    last_e = jnp.max(jnp.where(counts > 0, jnp.arange(N_EXPERTS, dtype=jnp.int32), 0))
    tile_e = jnp.where(valid, tile_e, last_e)
    return pos.reshape(-1), gates, tile_e, valid.astype(jnp.int32)


def _row_tables(pos, n_tokens, n_tiles):
    n_rows = n_tiles * TM_EXP
    pair = jnp.full((n_rows,), -1, jnp.int32).at[pos].set(
        jnp.arange(n_tokens * TOP_K, dtype=jnp.int32))
    tok = pair >> 1
    scratch_row = n_tokens * TOP_K + jnp.arange(n_rows, dtype=jnp.int32) % TM_EXP
    src = jnp.where(pair < 0, 0, tok)
    dst = jnp.where(pair < 0, scratch_row, (pair & 1) * n_tokens + tok)
    return src, jnp.concatenate([scratch_row[:TM_EXP], dst])


def _expert_kernel(te_ref, tv_ref, src_ref, dst_ref, hf_ref, wg_ref, wu_ref, wd_ref, y_ref,
                   x_ref, xb_ref, acc_ref, yt_ref, sem_in, sem_out):
    del te_ref
    tm = TM_EXP
    j = pl.program_id(0)
    f = pl.program_id(1)
    valid = tv_ref[j] > 0
    prev_valid = jnp.logical_and(j > 0, tv_ref[jnp.maximum(j - 1, 0)] > 0)

    def gather(tile, r):
        t = src_ref[tile * tm + r]
        return pltpu.make_async_copy(hf_ref.at[pl.ds(t, 1)], x_ref.at[pl.ds(r, 1)], sem_in)

    def gather_wait(r):
        return pltpu.make_async_copy(hf_ref.at[pl.ds(0, 1)], x_ref.at[pl.ds(r, 1)], sem_in)

    def scatter(tile, r):
        d = dst_ref[(tile + 1) * tm + r]
        return pltpu.make_async_copy(yt_ref.at[pl.ds(r, 1)], y_ref.at[pl.ds(d, 1)], sem_out)

    def scatter_wait(r):
        return pltpu.make_async_copy(yt_ref.at[pl.ds(r, 1)], y_ref.at[pl.ds(0, 1)], sem_out)

    def ffn_step():
        xb = xb_ref[...]
        a = jax.nn.silu(_dot(xb, wg_ref[0])) * _dot(xb, wu_ref[0])
        return _dot(a.astype(BF16), wd_ref[0])

    @pl.when(jnp.logical_and(j == 0, f == 0))
    def _():
        yt_ref[...] = jnp.zeros_like(yt_ref)
        for r in range(tm):
            gather(0, r).start()

    @pl.when(jnp.logical_and(f == 0, jnp.logical_or(j == 0, prev_valid)))
    def _():
        for r in range(tm):
            gather_wait(r).wait()

    @pl.when(jnp.logical_and(valid, f == 0))
    def _():
        xb_ref[...] = x_ref[...].astype(BF16)
        for r in range(tm):
            scatter(j - 1, r).start()
        acc_ref[...] = ffn_step()

    @pl.when(jnp.logical_and(jnp.logical_and(jnp.logical_not(valid), prev_valid), f == 0))
    def _():
        for r in range(tm):
            scatter(j - 1, r).start()

    @pl.when(jnp.logical_and(f == 1, jnp.logical_or(valid, prev_valid)))
    def _():
        for r in range(tm):
            scatter_wait(r).wait()

    @pl.when(jnp.logical_and(valid, f == 1))
    def _():
        for r in range(tm):
            gather(j + 1, r).start()
        yt_ref[...] = acc_ref[...] + ffn_step()


def _expert_ffn(tile_e, tile_valid, src, dst, hf, wg, wu, wd):
    t, d = hf.shape
    n_tiles = tile_e.shape[0]
    d_ff = wg.shape[2]
    nf = d_ff // TF_EXP
    assert nf == 2

    def f_of(f, tv, j):
        return jnp.where(tv[j] > 0, f, nf - 1)

    return pl.pallas_call(
        _expert_kernel,
        grid_spec=pltpu.PrefetchScalarGridSpec(
            num_scalar_prefetch=4,
            grid=(n_tiles, nf),
            in_specs=[
                pl.BlockSpec(memory_space=pl.ANY),
                pl.BlockSpec((1, d, TF_EXP), lambda j, f, te, tv, *_: (te[j], 0, f_of(f, tv, j))),
                pl.BlockSpec((1, d, TF_EXP), lambda j, f, te, tv, *_: (te[j], 0, f_of(f, tv, j))),
                pl.BlockSpec((1, TF_EXP, d), lambda j, f, te, tv, *_: (te[j], f_of(f, tv, j), 0)),
            ],
            out_specs=pl.BlockSpec(memory_space=pl.ANY),
            scratch_shapes=[pltpu.VMEM((TM_EXP, d), F32), pltpu.VMEM((TM_EXP, d), BF16),
                            pltpu.VMEM((TM_EXP, d), F32), pltpu.VMEM((TM_EXP, d), F32),
                            pltpu.SemaphoreType.DMA(()), pltpu.SemaphoreType.DMA(())],
        ),
        out_shape=jax.ShapeDtypeStruct((TOP_K * t + TM_EXP, d), F32),
        compiler_params=_params(2),
        name="moe_experts",
    )(tile_e, tile_valid, src, dst, hf, wg, wu, wd)


def _combine_kernel(y0_ref, y1_ref, h_ref, gate_ref, gf_ref, o_ref):
    gates = gate_ref[...]
    mix = gates[:, 0:1] * y0_ref[...] + gates[:, 1:2] * y1_ref[...]
    o_ref[...] = h_ref[...] + gf_ref[0] * mix


def _combine_rows(y, h, gates, mods, seq):
    t, d = h.shape
    steps_per_batch = seq // TG
    second = t // TG
    tok = pl.BlockSpec((TG, d), lambda i: (i, 0))
    return pl.pallas_call(
        _combine_kernel,
        grid=(t // TG,),
        in_specs=[
            tok,
            pl.BlockSpec((TG, d), lambda i: (second + i, 0)),
            tok,
            pl.BlockSpec((TG, TOP_K), lambda i: (i, 0)),
            pl.BlockSpec((1, 1, d), lambda i: (COND_ROWS + i // steps_per_batch, 0, N_MODS - 1)),
        ],
        out_specs=tok,
        out_shape=jax.ShapeDtypeStruct((t, d), F32),
        compiler_params=_params(1),
        name="moe_combine",
    )(y, y, h, gates, mods)


def _rope_tables(seq):
    rows = seq // GRID_W
    row = jnp.repeat(jnp.arange(rows), GRID_W).astype(F32)
    col = jnp.tile(jnp.arange(GRID_W), rows).astype(F32)
    quarter = ATTN_DIM // 4
    inv = ROPE_BASE ** (-jnp.arange(quarter, dtype=F32) / quarter)
    ar = row[:, None] * inv
    ac = col[:, None] * inv
    ang = jnp.concatenate([ar, ar, ac, ac], axis=-1)
    sign = jnp.tile(jnp.concatenate([-jnp.ones(quarter, F32), jnp.ones(quarter, F32)]), 2)
    reps = QK_W // ATTN_DIM
    return jnp.tile(jnp.cos(ang), (1, reps)), jnp.tile(jnp.sin(ang) * sign, (1, reps))


def kernel(x, c, ctx, c_ctx, ada_w, ada_b, norm_mix_g, norm_ffn_g, w_in_even, q_norm_g, k_norm_g,
           lam_q1, lam_k1, lam_q2, lam_k2, subln_g, conv_w, w_out_even, ffn_w_gate, ffn_w_up,
           ffn_w_down, sgu_w_in, sgu_ln_g, sgu_ln_b, sgu_w_s, sgu_b_s, sgu_w_out, router_w,
           moe_w_gate, moe_w_up, moe_w_down):
    b, s, d = x.shape
    assert d == D_MODEL and b < COND_ROWS and ada_w.shape[0] == 2
    assert s % TM_PROJ == 0 and s % TQ == 0 and s % TM_FFN == 0 and s % TM_SGU == 0 and s % TG == 0

    cond = jnp.zeros((COND_ROWS, d), F32).at[:b].set(c).at[b].set(c_ctx)
    mods = _ada_mods(cond, ada_w, ada_b)

    lambda_init = 0.8 - 0.6 * math.exp(-0.3 * 0)
    group = jnp.arange(QK_W) // ATTN_DIM
    bd = (group[:, None] == group[None, :]).astype(BF16)
    reps = QK_W // ATTN_DIM
    qg = jnp.tile(q_norm_g[0], reps)[None, :]
    kg = jnp.tile(k_norm_g[0], reps)[None, :]
    cos, sin = _rope_tables(s)
    w_in = w_in_even[0].astype(BF16)
    g_mix0 = norm_mix_g[0][None, :]
    q, k, v, yconv = _inproj(x, mods, g_mix0, w_in, bd, qg, kg, cos, sin, conv_w[0])
    kc, vc = _ctxproj(ctx, mods, g_mix0, w_in, bd, kg, ctx_row=b)
    attn = _attention(q, k, kc, v, vc, subln_g[0][None, :], lam_q1[0][None, :], lam_k1[0][None, :],
                      lam_q2[0][None, :], lam_k2[0][None, :], lambda_init)
    h = _outproj_ffn(x, attn, yconv, mods, norm_ffn_g[0][None, :], w_out_even[0].astype(BF16),
                     ffn_w_gate[0].astype(BF16), ffn_w_up[0].astype(BF16),
                     ffn_w_down[0].astype(BF16))

    gw = d // SGU_GROUPS
    bias = jnp.repeat(sgu_b_s[0].T, gw, axis=1)
    rw_hi = router_w[0].astype(BF16)
    rw_lo = (router_w[0] - rw_hi.astype(F32)).astype(BF16)
    h, hf, logits = _sgu(h, mods, norm_mix_g[1][None, :], norm_ffn_g[1][None, :],
                         sgu_w_in[0].astype(BF16), sgu_ln_g[0][None, :], sgu_ln_b[0][None, :],
                         sgu_w_s[0].astype(BF16), bias, sgu_w_out[0].astype(BF16), rw_hi, rw_lo)

    t = b * s
    n_tiles = (t * TOP_K) // TM_EXP + N_EXPERTS + 1
    pos, gates, tile_e, tile_valid = _route(logits.reshape(t, N_EXPERTS), TM_EXP, n_tiles)
    src, dst = _row_tables(pos, t, n_tiles)
    y = _expert_ffn(tile_e, tile_valid, src, dst, hf.reshape(t, d), moe_w_gate[0].astype(BF16),
                    moe_w_up[0].astype(BF16), moe_w_down[0].astype(BF16))
    out = _combine_rows(y, h.reshape(t, d), gates, mods, s)
    return out.reshape(b, s, d)
```

```python
import functools
import math

import jax
import jax.numpy as jnp
from jax import lax
from jax.experimental import pallas as pl
from jax.experimental.pallas import tpu as pltpu

F32 = jnp.float32
BF16 = jnp.bfloat16

D_MODEL = 1024
GRID_W = 64
ATTN_HEADS = 4
ATTN_DIM = 64
ATTN_VDIM = 2 * ATTN_DIM
QK_W = ATTN_HEADS * 2 * ATTN_DIM
CONV_K = 3
ROPE_BASE = 10000.0
SGU_GROUPS = 4
CHUNK = 128
N_EXPERTS = 8
TOP_K = 2
EPS = 1e-6
LOG2_E = math.log2(math.e)
N_MODS = 6
COND_ROWS = 16

V7X_VMEM_LIMIT = 56 * 1024 * 1024
MXU_N = 256
TM_PROJ = 512
HALO = 16
TQ = 2048
ATTN_RB = 256
ATTN_SLOTS = 8
ATTN_KC = 256
TM_FFN = 512
TM_SGU = 512
TM_EXP = 512
TF_EXP = 1792
TG = 256


def _params(n_axes):
    return pltpu.CompilerParams(
        dimension_semantics=("arbitrary",) * n_axes, vmem_limit_bytes=V7X_VMEM_LIMIT)


def _const_spec(shape):
    zeros = (0,) * len(shape)
    return pl.BlockSpec(shape, lambda *_: zeros, pipeline_mode=pl.Buffered(1))


def _dot(a, b):
    return jnp.dot(a, b, preferred_element_type=F32)


def _modulate(x, g, scale, shift):
    ms = jnp.mean(x * x, axis=-1, keepdims=True)
    y = x * lax.rsqrt(ms + EPS)
    return (y * g) * (1.0 + scale) + shift


def _split_bf16(x):
    hi = x.astype(BF16)
    lo = (x - hi.astype(F32)).astype(BF16)
    return hi, lo


assert D_MODEL == 8 * 128
SUBLANES = 8
LANES = 128


def _rows_to_tiles(ref, val):
    n = val.shape[0]
    for c in range(SUBLANES):
        ref[pl.ds(c, n, stride=SUBLANES), :] = val[:, c * LANES:(c + 1) * LANES]


def _tiles_to_rows(ref, n):
    return jnp.concatenate(
        [ref[pl.ds(c, n, stride=SUBLANES), :] for c in range(SUBLANES)], axis=1)


def _ada_kernel(cond_ref, w_ref, b_ref, o_ref):
    a = jax.nn.silu(cond_ref[...]).astype(BF16)
    o_ref[0] = _dot(a, w_ref[0].astype(BF16)) + b_ref[0]


def _ada_mods(cond, ada_w, ada_b):
    depth = ada_w.shape[0]
    d = D_MODEL
    out = pl.pallas_call(
        _ada_kernel,
        grid=(depth, N_MODS),
        in_specs=[
            pl.BlockSpec((COND_ROWS, d), lambda l, n: (0, 0)),
            pl.BlockSpec((1, d, d), lambda l, n: (l, 0, n)),
            pl.BlockSpec((1, 1, d), lambda l, n: (l, 0, n)),
        ],
        out_specs=pl.BlockSpec((1, COND_ROWS, d), lambda l, n: (l, 0, n)),
        out_shape=jax.ShapeDtypeStruct((depth, COND_ROWS, N_MODS * d), F32),
        compiler_params=_params(2),
        name="ada_mods",
    )(cond, ada_w, ada_b.reshape(depth, 1, N_MODS * d))
    return out.reshape(depth * COND_ROWS, 1, N_MODS * d)


def _mod_spec(layer, chunk, row_of):
    return pl.BlockSpec((1, 1, D_MODEL),
                        lambda *ids: (layer * COND_ROWS + row_of(*ids), 0, chunk))


def _group_rms(t, bd, gain):
    hi, lo = _split_bf16(t * t)
    ss = _dot(hi, bd) + _dot(lo, bd)
    return (t * lax.rsqrt(ss * (1.0 / ATTN_DIM) + EPS)) * gain


def _rope(t, cos, sin_signed, first_half):
    width = t.shape[-1]
    quarter = ATTN_DIM // 4
    ahead = pltpu.roll(t, width - quarter, axis=1)
    behind = pltpu.roll(t, quarter, axis=1)
    return t * cos + jnp.where(first_half, ahead, behind) * sin_signed


def _with_ones_columns(v):
    rows = v.shape[0]
    lane = lax.broadcasted_iota(jnp.int32, (rows, ATTN_VDIM), 1)
    ones_col = jnp.where(lane == 0, 1.0, 0.0).astype(v.dtype)
    parts = []
    for h in range(ATTN_HEADS):
        parts += [v[:, h * ATTN_VDIM:(h + 1) * ATTN_VDIM], ones_col]
    return jnp.concatenate(parts, axis=1)


def _inproj_kernel(x_ref, xp_ref, xn_ref, sc_ref, sh_ref, g_ref, w_ref, bd_ref, qg_ref, kg_ref,
                   cos_ref, sin_ref, cw_ref, q_ref, k_ref, v_ref, yc_ref):
    i = pl.program_id(1)
    last = pl.num_programs(1) - 1
    tm = x_ref.shape[1]
    w = QK_W
    xe = jnp.concatenate([xp_ref[0], x_ref[0], xn_ref[0]], axis=0)
    he = _modulate(xe, g_ref[...], sc_ref[0], sh_ref[0]).astype(BF16)
    hm = he[HALO:HALO + tm]

    qkv = _dot(hm, w_ref[:, :3 * w])
    bd = bd_ref[...]
    cos = cos_ref[...]
    sin = sin_ref[...]
    lane = lax.broadcasted_iota(jnp.int32, (1, w), 1)
    first_half = (lane % (ATTN_DIM // 2)) < (ATTN_DIM // 4)
    q = _rope(_group_rms(qkv[:, :w], bd, qg_ref[...]), cos, sin, first_half)
    k = _rope(_group_rms(qkv[:, w:2 * w], bd, kg_ref[...]), cos, sin, first_half)
    q_ref[0] = (q * (ATTN_DIM ** -0.5 * LOG2_E)).astype(BF16)
    k_ref[0] = k.astype(BF16)
    v_ref[0] = _with_ones_columns(qkv[:, 2 * w:].astype(BF16))

    gcu = _dot(he, w_ref[:, 4 * w:])
    cu = gcu[:, :w] * gcu[:, w:]
    row = lax.broadcasted_iota(jnp.int32, (tm + 2 * HALO, 1), 0)
    first_row = jnp.where(i == 0, HALO, 0)
    end_row = jnp.where(i == last, HALO + tm, tm + 2 * HALO)
    cu = jnp.where((row >= first_row) & (row < end_row), cu, 0.0)
    prev = pltpu.roll(cu, 1, axis=0)[HALO:HALO + tm]
    nxt = pltpu.roll(cu, tm + 2 * HALO - 1, axis=0)[HALO:HALO + tm]
    cw = cw_ref[...]
    conv = prev * cw[0:1] + cu[HALO:HALO + tm] * cw[1:2] + nxt * cw[2:3]
    gate_b = _dot(hm, w_ref[:, 3 * w:4 * w])
    yc_ref[0] = (gate_b * conv).astype(BF16)


def _inproj(x, mods, g, w_in, bd, qg, kg, cos, sin, conv_w):
    b, s, d = x.shape
    tm = TM_PROJ
    nh = tm // HALO
    n_halo_blocks = s // HALO
    tok = pl.BlockSpec((1, tm, QK_W), lambda bi, i: (bi, i, 0))
    out_sds = jax.ShapeDtypeStruct((b, s, QK_W), BF16)
    return pl.pallas_call(
        _inproj_kernel,
        grid=(b, s // tm),
        in_specs=[
            pl.BlockSpec((1, tm, d), lambda bi, i: (bi, i, 0)),
            pl.BlockSpec((1, HALO, d), lambda bi, i: (bi, jnp.maximum(i * nh - 1, 0), 0)),
            pl.BlockSpec((1, HALO, d),
                         lambda bi, i: (bi, jnp.minimum((i + 1) * nh, n_halo_blocks - 1), 0)),
            _mod_spec(0, 1, lambda bi, i: bi),
            _mod_spec(0, 0, lambda bi, i: bi),
            _const_spec((1, d)),
            _const_spec(w_in.shape),
            _const_spec(bd.shape),
            _const_spec((1, QK_W)),
            _const_spec((1, QK_W)),
            pl.BlockSpec((tm, QK_W), lambda bi, i: (i, 0)),
            pl.BlockSpec((tm, QK_W), lambda bi, i: (i, 0)),
            _const_spec(conv_w.shape),
        ],
        out_specs=[tok, tok, pl.BlockSpec((1, tm, 2 * QK_W), lambda bi, i: (bi, i, 0)), tok],
        out_shape=[out_sds, out_sds, jax.ShapeDtypeStruct((b, s, 2 * QK_W), BF16), out_sds],
        compiler_params=_params(2),
        name="inproj",
    )(x, x, x, mods, mods, g, w_in, bd, qg, kg, cos, sin, conv_w)


def _ctxproj_kernel(x_ref, sc_ref, sh_ref, g_ref, wk_ref, wv_ref, bd_ref, kg_ref, k_ref, v_ref):
    hc = _modulate(x_ref[0], g_ref[...], sc_ref[0], sh_ref[0]).astype(BF16)
    k_ref[0] = _group_rms(_dot(hc, wk_ref[...]), bd_ref[...], kg_ref[...]).astype(BF16)
    v_ref[0] = _with_ones_columns(_dot(hc, wv_ref[...]).astype(BF16))


def _ctxproj(ctx, mods, g, w_in, bd, kg, ctx_row):
    b, l, d = ctx.shape
    out = pl.BlockSpec((1, l, QK_W), lambda bi: (bi, 0, 0))
    out_sds = jax.ShapeDtypeStruct((b, l, QK_W), BF16)
    return pl.pallas_call(
        _ctxproj_kernel,
        grid=(b,),
        in_specs=[
            pl.BlockSpec((1, l, d), lambda bi: (bi, 0, 0)),
            _mod_spec(0, 1, lambda bi: ctx_row),
            _mod_spec(0, 0, lambda bi: ctx_row),
            _const_spec((1, d)),
            pl.BlockSpec((d, QK_W), lambda bi: (0, 1), pipeline_mode=pl.Buffered(1)),
            pl.BlockSpec((d, QK_W), lambda bi: (0, 2), pipeline_mode=pl.Buffered(1)),
            _const_spec(bd.shape),
            _const_spec((1, QK_W)),
        ],
        out_specs=[out, pl.BlockSpec((1, l, 2 * QK_W), lambda bi: (bi, 0, 0))],
        out_shape=[out_sds, jax.ShapeDtypeStruct((b, l, 2 * QK_W), BF16)],
        compiler_params=_params(1),
        name="ctxproj",
    )(ctx, mods, mods, g, w_in, w_in, bd, kg)


def _attn_kernel(q_ref, k_ref, kc_ref, v_ref, vc_ref, sg_ref, lq1_ref, lk1_ref, lq2_ref, lk2_ref,
                 o_ref, s_ref, e_ref, *, lambda_init):
    tq = q_ref.shape[1]
    n_ctx = kc_ref.shape[1] // ATTN_KC
    n_lat = k_ref.shape[1] // ATTN_KC
    chunks = [(kc_ref, vc_ref, c) for c in range(n_ctx)] + [(k_ref, v_ref, c) for c in range(n_lat)]
    lane = lax.broadcasted_iota(jnp.int32, (1, ATTN_VDIM), 1)
    first = lane < ATTN_DIM
    lam = (jnp.exp(jnp.sum(lq1_ref[...] * lk1_ref[...], keepdims=True))
           - jnp.exp(jnp.sum(lq2_ref[...] * lk2_ref[...], keepdims=True)) + lambda_init)
    nt = (((1,), (1,)), ((), ()))

    for rb in range(tq // ATTN_RB):
        rows = slice(rb * ATTN_RB, (rb + 1) * ATTN_RB)
        q = q_ref[0, rows, :]
        zero = jnp.zeros_like(q)
        outs = []
        for part in range(2):
            qx = jnp.where(first, q, zero) if part == 0 else jnp.where(first, zero, q)
            slot = (rb * 2 + part) % ATTN_SLOTS
            mpart = None
            for ci, (kr, _, c) in enumerate(chunks):
                sc = lax.dot_general(qx, kr[0, c * ATTN_KC:(c + 1) * ATTN_KC, :], nt,
                                     preferred_element_type=F32)
                s_ref[slot, ci] = sc
                cm = sc[:, :ATTN_VDIM]
                for j in range(1, ATTN_KC // ATTN_VDIM):
                    cm = jnp.maximum(cm, sc[:, j * ATTN_VDIM:(j + 1) * ATTN_VDIM])
                mpart = cm if mpart is None else jnp.maximum(mpart, cm)
            m = jnp.max(mpart, axis=-1, keepdims=True)
            for ci in range(len(chunks)):
                e_ref[slot, :, ci * ATTN_KC:(ci + 1) * ATTN_KC] = (
                    jnp.exp2(s_ref[slot, ci] - m).astype(BF16))
            n_c = n_ctx * ATTN_KC
            acc = _dot(e_ref[slot, :, :n_c], vc_ref[0]) + _dot(e_ref[slot, :, n_c:], v_ref[0])
            outs.append(acc[:, :ATTN_VDIM] * (1.0 / acc[:, ATTN_VDIM:ATTN_VDIM + 1]))
        o = outs[0] - lam * outs[1]
        ms = jnp.mean(o * o, axis=-1, keepdims=True)
        y = (o * lax.rsqrt(ms + EPS)) * sg_ref[...]
        o_ref[0, rows, :] = (y * (1.0 - lambda_init)).astype(BF16)


def _attention(q, k, kc, v, vc, subln_g, lq1, lk1, lq2, lk2, lambda_init):
    b, s, _ = q.shape
    l = kc.shape[1]
    hd = ATTN_VDIM
    va = 2 * ATTN_VDIM
    assert l % ATTN_KC == 0 and s % ATTN_KC == 0
    n_chunks = (l + s) // ATTN_KC
    vec = _const_spec((1, ATTN_DIM))
    return pl.pallas_call(
        functools.partial(_attn_kernel, lambda_init=lambda_init),
        grid=(b, ATTN_HEADS, s // TQ),
        in_specs=[
            pl.BlockSpec((1, TQ, hd), lambda bi, h, i: (bi, i, h)),
            pl.BlockSpec((1, s, hd), lambda bi, h, i: (bi, 0, h)),
            pl.BlockSpec((1, l, hd), lambda bi, h, i: (bi, 0, h)),
            pl.BlockSpec((1, s, va), lambda bi, h, i: (bi, 0, h)),
            pl.BlockSpec((1, l, va), lambda bi, h, i: (bi, 0, h)),
            _const_spec((1, hd)),
            vec, vec, vec, vec,
        ],
        out_specs=pl.BlockSpec((1, TQ, hd), lambda bi, h, i: (bi, i, h)),
        out_shape=jax.ShapeDtypeStruct((b, s, ATTN_HEADS * hd), BF16),
        scratch_shapes=[pltpu.VMEM((ATTN_SLOTS, n_chunks, ATTN_RB, ATTN_KC), F32),
                        pltpu.VMEM((ATTN_SLOTS, ATTN_RB, l + s), BF16)],
        compiler_params=_params(3),
        name="diff_attention",
    )(q, k, kc, v, vc, subln_g, lq1, lk1, lq2, lk2)


def _outproj_ffn_kernel(x_ref, at_ref, yc_ref, gm_ref, sf_ref, cf_ref, gf_ref, g_ref,
                        wo_ref, wg_ref, wu_ref, wd_ref, o_ref):
    half = wo_ref.shape[0] // 2
    y = _dot(at_ref[0], wo_ref[:half]) + _dot(yc_ref[0], wo_ref[half:])
    h1 = x_ref[0] + gm_ref[0] * y
    hf = _modulate(h1, g_ref[...], cf_ref[0], sf_ref[0]).astype(BF16)
    d_ff = wg_ref.shape[1]
    acc = jnp.zeros(h1.shape, F32)
    for f in range(d_ff // MXU_N):
        cols = slice(f * MXU_N, (f + 1) * MXU_N)
        a = jax.nn.silu(_dot(hf, wg_ref[:, cols])) * _dot(hf, wu_ref[:, cols])
        acc = acc + _dot(a.astype(BF16), wd_ref[cols, :])
    o_ref[0] = h1 + gf_ref[0] * acc


def _outproj_ffn(x, attn, yconv, mods, g, w_out, wg, wu, wd):
    b, s, d = x.shape
    tm = TM_FFN
    return pl.pallas_call(
        _outproj_ffn_kernel,
        grid=(b, s // tm),
        in_specs=[
            pl.BlockSpec((1, tm, d), lambda bi, i: (bi, i, 0)),
            pl.BlockSpec((1, tm, QK_W), lambda bi, i: (bi, i, 0)),
            pl.BlockSpec((1, tm, QK_W), lambda bi, i: (bi, i, 0)),
            _mod_spec(0, 2, lambda bi, i: bi),
            _mod_spec(0, 3, lambda bi, i: bi),
            _mod_spec(0, 4, lambda bi, i: bi),
            _mod_spec(0, 5, lambda bi, i: bi),
            _const_spec((1, d)),
            _const_spec(w_out.shape),
            _const_spec(wg.shape),
            _const_spec(wu.shape),
            _const_spec(wd.shape),
        ],
        out_specs=pl.BlockSpec((1, tm, d), lambda bi, i: (bi, i, 0)),
        out_shape=jax.ShapeDtypeStruct((b, s, d), F32),
        compiler_params=_params(2),
        name="outproj_ffn",
    )(x, attn, yconv, mods, mods, mods, mods, g, w_out, wg, wu, wd)


def _sgu_kernel(x_ref, sm_ref, cm_ref, gm_ref, sf_ref, cf_ref, g_ref, gff_ref, win_ref, lng_ref,
                lnb_ref, ws_ref, bs_ref, wo_ref, rwh_ref, rwl_ref, h_ref, hf_ref, lg_ref, us_ref):
    x = x_ref[0]
    tm, d = x.shape
    hm = _modulate(x, g_ref[...], cm_ref[0], sm_ref[0]).astype(BF16)
    z = jax.nn.gelu(_dot(hm, win_ref[...]))
    v = z[:, d:]
    mu = jnp.mean(v, axis=-1, keepdims=True)
    vc = v - mu
    var = jnp.mean(vc * vc, axis=-1, keepdims=True)
    vn = ((vc * lax.rsqrt(var + EPS)) * lng_ref[...] + lnb_ref[...]).astype(BF16)
    gw = d // SGU_GROUPS
    for n in range(tm // CHUNK):
        rows = slice(n * CHUNK, (n + 1) * CHUNK)
        for gi in range(SGU_GROUPS):
            cols = slice(gi * gw, (gi + 1) * gw)
            s = _dot(ws_ref[gi], vn[rows, cols]) + bs_ref[:, cols]
            us_ref[rows, cols] = (z[rows, cols] * s).astype(BF16)
    h = x + gm_ref[0] * _dot(us_ref[...], wo_ref[...])
    h_ref[0] = h
    hf = _modulate(h, gff_ref[...], cf_ref[0], sf_ref[0])
    _rows_to_tiles(hf_ref, hf)
    hi, lo = _split_bf16(hf)
    lg_ref[0] = _dot(hi, rwh_ref[...]) + (_dot(hi, rwl_ref[...]) + _dot(lo, rwh_ref[...]))


def _sgu(h, mods, g_mix, g_ffn, w_in, ln_g, ln_b, w_s, bias, w_out, rw_hi, rw_lo):
    b, s, d = h.shape
    tm = TM_SGU
    tok = pl.BlockSpec((1, tm, d), lambda bi, i: (bi, i, 0))
    return pl.pallas_call(
        _sgu_kernel,
        grid=(b, s // tm),
        in_specs=[
            tok,
            _mod_spec(1, 0, lambda bi, i: bi),
            _mod_spec(1, 1, lambda bi, i: bi),
            _mod_spec(1, 2, lambda bi, i: bi),
            _mod_spec(1, 3, lambda bi, i: bi),
            _mod_spec(1, 4, lambda bi, i: bi),
            _const_spec((1, d)),
            _const_spec((1, d)),
            _const_spec(w_in.shape),
            _const_spec((1, d)),
            _const_spec((1, d)),
            _const_spec(w_s.shape),
            _const_spec(bias.shape),
            _const_spec(w_out.shape),
            _const_spec(rw_hi.shape),
            _const_spec(rw_lo.shape),
        ],
        out_specs=[tok,
                   pl.BlockSpec((tm * SUBLANES, LANES), lambda bi, i: (bi * (s // tm) + i, 0)),
                   pl.BlockSpec((1, tm, N_EXPERTS), lambda bi, i: (bi, i, 0))],
        out_shape=[jax.ShapeDtypeStruct((b, s, d), F32),
                   jax.ShapeDtypeStruct((b * s * SUBLANES, LANES), F32),
                   jax.ShapeDtypeStruct((b, s, N_EXPERTS), F32)],
        scratch_shapes=[pltpu.VMEM((tm, d), BF16)],
        compiler_params=_params(2),
        name="sgu",
    )(h, mods, mods, mods, mods, mods, g_mix, g_ffn, w_in, ln_g, ln_b, w_s, bias, w_out,
      rw_hi, rw_lo)


def _route(logits, tile_rows, n_tiles):
    t = logits.shape[0]
    eid = lax.broadcasted_iota(jnp.int32, (t, N_EXPERTS), 1)
    i1 = jnp.argmax(logits, axis=-1).astype(jnp.int32)
    v1 = jnp.max(logits, axis=-1)
    rest = jnp.where(eid == i1[:, None], -jnp.inf, logits)
    i2 = jnp.argmax(rest, axis=-1).astype(jnp.int32)
    v2 = jnp.max(rest, axis=-1)
    e2 = jnp.exp(v2 - v1)
    gates = jnp.stack([1.0 / (1.0 + e2), e2 / (1.0 + e2)], axis=-1)
    onehot = ((eid == i1[:, None]) | (eid == i2[:, None])).astype(jnp.int32)
    csum = jnp.cumsum(onehot, axis=0)
    rank = csum - onehot
    counts = csum[-1]
    padded = ((counts + tile_rows - 1) // tile_rows) * tile_rows
    ends = jnp.cumsum(padded)
    starts = ends - padded
    idx = jnp.stack([i1, i2], axis=-1)
    pos = (starts[idx] + jnp.take_along_axis(rank, idx, axis=-1)).astype(jnp.int32)
    tile_start = jnp.arange(n_tiles, dtype=jnp.int32) * tile_rows
    valid = tile_start < ends[-1]
    tile_e = jnp.sum(tile_start[:, None] >= ends[None, :], axis=-1).astype(jnp.int32)
    last_e = jnp.max(jnp.where(counts > 0, jnp.arange(N_EXPERTS, dtype=jnp.int32), 0))
    tile_e = jnp.where(valid, tile_e, last_e)
    return pos.reshape(-1), gates, tile_e, valid.astype(jnp.int32)


def _row_tables(pos, n_tokens, n_tiles):
    n_rows = n_tiles * TM_EXP
    pair = jnp.full((n_rows,), -1, jnp.int32).at[pos].set(
        jnp.arange(n_tokens * TOP_K, dtype=jnp.int32), unique_indices=True)
    tok = pair >> 1
    scratch_row = n_tokens * TOP_K + jnp.arange(n_rows, dtype=jnp.int32) % TM_EXP
    src = jnp.where(pair < 0, 0, tok)
    dst = jnp.where(pair < 0, scratch_row, (pair & 1) * n_tokens + tok)
    return src * SUBLANES, jnp.concatenate([scratch_row[:TM_EXP], dst]) * SUBLANES


def _expert_kernel(te_ref, tv_ref, tf_ref, src_ref, dst_ref, hf_ref, wg_ref, wu_ref, wd_ref, y_ref,
                   x_ref, xb_ref, acc_ref, yt_ref, sem_in, sem_out):
    del te_ref, tf_ref
    tm = TM_EXP
    j = pl.program_id(0)
    step = pl.program_id(1)
    valid = tv_ref[j] > 0
    prev_valid = jnp.logical_and(j > 0, tv_ref[jnp.maximum(j - 1, 0)] > 0)
    prev2_valid = jnp.logical_and(j > 1, tv_ref[jnp.maximum(j - 2, 0)] > 0)
    slot = j % 2
    x_rows = lambda r: x_ref.at[pl.ds(r * SUBLANES, SUBLANES)]
    yt_rows = lambda sl, r: yt_ref.at[sl, pl.ds(r * SUBLANES, SUBLANES)]

    def gather(tile, r):
        t8 = pl.multiple_of(src_ref[tile * tm + r], SUBLANES)
        return pltpu.make_async_copy(hf_ref.at[pl.ds(t8, SUBLANES)], x_rows(r), sem_in)

    def gather_wait(r):
        return pltpu.make_async_copy(hf_ref.at[pl.ds(0, SUBLANES)], x_rows(r), sem_in)

    def scatter(tile, sl, r):
        d8 = pl.multiple_of(dst_ref[(tile + 1) * tm + r], SUBLANES)
        return pltpu.make_async_copy(yt_rows(sl, r), y_ref.at[pl.ds(d8, SUBLANES)], sem_out)

    def scatter_wait(sl, r):
        return pltpu.make_async_copy(yt_rows(sl, r), y_ref.at[pl.ds(0, SUBLANES)], sem_out)

    def ffn_step():
        xb = xb_ref[...]
        a = jax.nn.silu(_dot(xb, wg_ref[0])) * _dot(xb, wu_ref[0])
        return _dot(a.astype(BF16), wd_ref[0])

    @pl.when(jnp.logical_and(j == 0, step == 0))
    def _():
        yt_ref[...] = jnp.zeros_like(yt_ref)
        for r in range(tm):
            gather(0, r).start()

    @pl.when(jnp.logical_and(step == 0, jnp.logical_or(j == 0, prev_valid)))
    def _():
        for r in range(tm):
            gather_wait(r).wait()

    @pl.when(jnp.logical_and(valid, step == 0))
    def _():
        xb_ref[...] = _tiles_to_rows(x_ref, tm).astype(BF16)
        for r in range(tm):
            gather(j + 1, r).start()
        acc_ref[...] = ffn_step()

    @pl.when(jnp.logical_and(step == 1, jnp.logical_or(j == 1, prev2_valid)))
    def _():
        for r in range(tm):
            scatter_wait(slot, r).wait()

    @pl.when(jnp.logical_and(valid, step == 1))
    def _():
        for r in range(tm):
            scatter(j - 1, 1 - slot, r).start()
        _rows_to_tiles(yt_ref.at[slot], acc_ref[...] + ffn_step())

    @pl.when(jnp.logical_and(jnp.logical_and(jnp.logical_not(valid), prev_valid), step == 1))
    def _():
        for r in range(tm):
            scatter(j - 1, 1 - slot, r).start()


def _expert_ffn(tile_e, tile_valid, src, dst, hf, wg, wu, wd):
    n_tok = hf.shape[0] // SUBLANES
    d = D_MODEL
    n_tiles = tile_e.shape[0]
    d_ff = wg.shape[2]
    nf = d_ff // TF_EXP
    assert nf == 2
    j = jnp.arange(n_tiles, dtype=jnp.int32)
    first = j % 2
    n_valid = jnp.sum(tile_valid)
    last_f = 1 - (n_valid - 1) % 2
    half = jnp.where(tile_valid[:, None] > 0, jnp.stack([first, 1 - first], axis=1), last_f)
    tile_f = half.reshape(-1).astype(jnp.int32)

    return pl.pallas_call(
        _expert_kernel,
        grid_spec=pltpu.PrefetchScalarGridSpec(
            num_scalar_prefetch=5,
            grid=(n_tiles, nf),
            in_specs=[
                pl.BlockSpec(memory_space=pl.ANY),
                pl.BlockSpec((1, d, TF_EXP), lambda j, s, te, tv, tf, *_: (te[j], 0, tf[2 * j + s])),
                pl.BlockSpec((1, d, TF_EXP), lambda j, s, te, tv, tf, *_: (te[j], 0, tf[2 * j + s])),
                pl.BlockSpec((1, TF_EXP, d), lambda j, s, te, tv, tf, *_: (te[j], tf[2 * j + s], 0)),
            ],
            out_specs=pl.BlockSpec(memory_space=pl.ANY),
            scratch_shapes=[pltpu.VMEM((TM_EXP * SUBLANES, LANES), F32),
                            pltpu.VMEM((TM_EXP, d), BF16),
                            pltpu.VMEM((TM_EXP, d), F32),
                            pltpu.VMEM((2, TM_EXP * SUBLANES, LANES), F32),
                            pltpu.SemaphoreType.DMA(()), pltpu.SemaphoreType.DMA(())],
        ),
        out_shape=jax.ShapeDtypeStruct(((TOP_K * n_tok + TM_EXP) * SUBLANES, LANES), F32),
        compiler_params=_params(2),
        name="moe_experts",
    )(tile_e, tile_valid, tile_f, src, dst, hf, wg, wu, wd)


def _combine_kernel(y0_ref, y1_ref, h_ref, gate_ref, gf_ref, o_ref):
    gates = gate_ref[...]
    n = h_ref.shape[0]
    mix = gates[:, 0:1] * _tiles_to_rows(y0_ref, n) + gates[:, 1:2] * _tiles_to_rows(y1_ref, n)
    o_ref[...] = h_ref[...] + gf_ref[0] * mix


def _combine_rows(y, h, gates, mods, seq):
    t, d = h.shape
    steps_per_batch = seq // TG
    second = t // TG
    tok = pl.BlockSpec((TG, d), lambda i: (i, 0))
    return pl.pallas_call(
        _combine_kernel,
        grid=(t // TG,),
        in_specs=[
            pl.BlockSpec((TG * SUBLANES, LANES), lambda i: (i, 0)),
            pl.BlockSpec((TG * SUBLANES, LANES), lambda i: (second + i, 0)),
            tok,
            pl.BlockSpec((TG, TOP_K), lambda i: (i, 0)),
            pl.BlockSpec((1, 1, d), lambda i: (COND_ROWS + i // steps_per_batch, 0, N_MODS - 1)),
        ],
        out_specs=tok,
        out_shape=jax.ShapeDtypeStruct((t, d), F32),
        compiler_params=_params(1),
        name="moe_combine",
    )(y, y, h, gates, mods)


def _rope_tables(seq):
    rows = seq // GRID_W
    row = jnp.repeat(jnp.arange(rows), GRID_W).astype(F32)
    col = jnp.tile(jnp.arange(GRID_W), rows).astype(F32)
    quarter = ATTN_DIM // 4
    inv = ROPE_BASE ** (-jnp.arange(quarter, dtype=F32) / quarter)
    ar = row[:, None] * inv
    ac = col[:, None] * inv
    ang = jnp.concatenate([ar, ar, ac, ac], axis=-1)
    sign = jnp.tile(jnp.concatenate([-jnp.ones(quarter, F32), jnp.ones(quarter, F32)]), 2)
    reps = QK_W // ATTN_DIM
    return jnp.tile(jnp.cos(ang), (1, reps)), jnp.tile(jnp.sin(ang) * sign, (1, reps))


def kernel(x, c, ctx, c_ctx, ada_w, ada_b, norm_mix_g, norm_ffn_g, w_in_even, q_norm_g, k_norm_g,
           lam_q1, lam_k1, lam_q2, lam_k2, subln_g, conv_w, w_out_even, ffn_w_gate, ffn_w_up,
           ffn_w_down, sgu_w_in, sgu_ln_g, sgu_ln_b, sgu_w_s, sgu_b_s, sgu_w_out, router_w,
           moe_w_gate, moe_w_up, moe_w_down):
    b, s, d = x.shape
    assert d == D_MODEL and b < COND_ROWS and ada_w.shape[0] == 2
    assert s % TM_PROJ == 0 and s % TQ == 0 and s % TM_FFN == 0 and s % TM_SGU == 0 and s % TG == 0

    cond = jnp.zeros((COND_ROWS, d), F32).at[:b].set(c).at[b].set(c_ctx)
    mods = _ada_mods(cond, ada_w, ada_b)

    lambda_init = 0.8 - 0.6 * math.exp(-0.3 * 0)
    group = jnp.arange(QK_W) // ATTN_DIM
    bd = (group[:, None] == group[None, :]).astype(BF16)
    reps = QK_W // ATTN_DIM
    qg = jnp.tile(q_norm_g[0], reps)[None, :]
    kg = jnp.tile(k_norm_g[0], reps)[None, :]
    cos, sin = _rope_tables(s)
    w_in = w_in_even[0].astype(BF16)
    g_mix0 = norm_mix_g[0][None, :]
    q, k, v, yconv = _inproj(x, mods, g_mix0, w_in, bd, qg, kg, cos, sin, conv_w[0])
    kc, vc = _ctxproj(ctx, mods, g_mix0, w_in, bd, kg, ctx_row=b)
    attn = _attention(q, k, kc, v, vc, subln_g[0][None, :], lam_q1[0][None, :], lam_k1[0][None, :],
                      lam_q2[0][None, :], lam_k2[0][None, :], lambda_init)
    h = _outproj_ffn(x, attn, yconv, mods, norm_ffn_g[0][None, :], w_out_even[0].astype(BF16),
                     ffn_w_gate[0].astype(BF16), ffn_w_up[0].astype(BF16),
                     ffn_w_down[0].astype(BF16))

    gw = d // SGU_GROUPS
    bias = jnp.repeat(sgu_b_s[0].T, gw, axis=1)
    rw_hi = router_w[0].astype(BF16)
    rw_lo = (router_w[0] - rw_hi.astype(F32)).astype(BF16)
    h, hf, logits = _sgu(h, mods, norm_mix_g[1][None, :], norm_ffn_g[1][None, :],
                         sgu_w_in[0].astype(BF16), sgu_ln_g[0][None, :], sgu_ln_b[0][None, :],
                         sgu_w_s[0].astype(BF16), bias, sgu_w_out[0].astype(BF16), rw_hi, rw_lo)

    t = b * s
    n_tiles = (t * TOP_K) // TM_EXP + N_EXPERTS + 2
    pos, gates, tile_e, tile_valid = _route(logits.reshape(t, N_EXPERTS), TM_EXP, n_tiles)
    src, dst = _row_tables(pos, t, n_tiles)
    y = _expert_ffn(tile_e, tile_valid, src, dst, hf, moe_w_gate[0].astype(BF16),
                    moe_w_up[0].astype(BF16), moe_w_down[0].astype(BF16))
    out = _combine_rows(y, h.reshape(t, d), gates, mods, s)
    return out.reshape(b, s, d)
```

```python
import functools
import math

import jax
import jax.numpy as jnp
from jax import lax
from jax.experimental import pallas as pl
from jax.experimental.pallas import tpu as pltpu

F32 = jnp.float32
BF16 = jnp.bfloat16

D_MODEL = 1024
GRID_W = 64
ATTN_HEADS = 4
ATTN_DIM = 64
ATTN_VDIM = 2 * ATTN_DIM
QK_W = ATTN_HEADS * 2 * ATTN_DIM
CONV_K = 3
ROPE_BASE = 10000.0
SGU_GROUPS = 4
CHUNK = 128
N_EXPERTS = 8
TOP_K = 2
EPS = 1e-6
LOG2_E = math.log2(math.e)
N_MODS = 6
COND_ROWS = 16

V7X_VMEM_LIMIT = 56 * 1024 * 1024
MXU_N = 256
TM_PROJ = 512
HALO = 16
TQ = 2048
ATTN_RB = 256
ATTN_SLOTS = 8
ATTN_KC = 256
TM_FFN = 512
FFN_RB = 256
TM_SGU = 512
SGU_RB = 256
TM_EXP = 512
TF_EXP = 1792
TG = 256
GATHER_DMA_QUEUE = 1
SCATTER_DMA_QUEUE = 1


def _params(n_axes):
    return pltpu.CompilerParams(
        dimension_semantics=("arbitrary",) * n_axes, vmem_limit_bytes=V7X_VMEM_LIMIT)


def _const_spec(shape):
    zeros = (0,) * len(shape)
    return pl.BlockSpec(shape, lambda *_: zeros, pipeline_mode=pl.Buffered(1))


def _dot(a, b):
    return jnp.dot(a, b, preferred_element_type=F32)


def _modulate(x, g, scale, shift):
    ms = jnp.mean(x * x, axis=-1, keepdims=True)
    y = x * lax.rsqrt(ms + EPS)
    return (y * g) * (1.0 + scale) + shift


def _split_bf16(x):
    hi = x.astype(BF16)
    lo = (x - hi.astype(F32)).astype(BF16)
    return hi, lo


assert D_MODEL == 8 * 128
SUBLANES = 8
LANES = 128


def _rows_to_tiles(ref, val, row0=0):
    n = val.shape[0]
    for c in range(SUBLANES):
        ref[pl.ds(row0 * SUBLANES + c, n, stride=SUBLANES), :] = val[:, c * LANES:(c + 1) * LANES]


def _tiles_to_rows(ref, n):
    return jnp.concatenate(
        [ref[pl.ds(c, n, stride=SUBLANES), :] for c in range(SUBLANES)], axis=1)


def _ada_kernel(cond_ref, w_ref, b_ref, o_ref):
    a = jax.nn.silu(cond_ref[...]).astype(BF16)
    o_ref[0] = _dot(a, w_ref[0].astype(BF16)) + b_ref[0]


def _ada_mods(cond, ada_w, ada_b):
    depth = ada_w.shape[0]
    d = D_MODEL
    out = pl.pallas_call(
        _ada_kernel,
        grid=(depth, N_MODS),
        in_specs=[
            pl.BlockSpec((COND_ROWS, d), lambda l, n: (0, 0)),
            pl.BlockSpec((1, d, d), lambda l, n: (l, 0, n)),
            pl.BlockSpec((1, 1, d), lambda l, n: (l, 0, n)),
        ],
        out_specs=pl.BlockSpec((1, COND_ROWS, d), lambda l, n: (l, 0, n)),
        out_shape=jax.ShapeDtypeStruct((depth, COND_ROWS, N_MODS * d), F32),
        compiler_params=_params(2),
        name="ada_mods",
    )(cond, ada_w, ada_b.reshape(depth, 1, N_MODS * d))
    return out.reshape(depth * COND_ROWS, 1, N_MODS * d)


def _mod_spec(layer, chunk, row_of):
    return pl.BlockSpec((1, 1, D_MODEL),
                        lambda *ids: (layer * COND_ROWS + row_of(*ids), 0, chunk))


def _group_rms(t, bd, gain):
    hi, lo = _split_bf16(t * t)
    ss = _dot(hi, bd) + _dot(lo, bd)
    return (t * lax.rsqrt(ss * (1.0 / ATTN_DIM) + EPS)) * gain


def _rope(t, cos, sin_signed, first_half):
    width = t.shape[-1]
    quarter = ATTN_DIM // 4
    ahead = pltpu.roll(t, width - quarter, axis=1)
    behind = pltpu.roll(t, quarter, axis=1)
    return t * cos + jnp.where(first_half, ahead, behind) * sin_signed


def _with_ones_columns(v):
    rows = v.shape[0]
    lane = lax.broadcasted_iota(jnp.int32, (rows, ATTN_VDIM), 1)
    ones_col = jnp.where(lane == 0, 1.0, 0.0).astype(v.dtype)
    parts = []
    for h in range(ATTN_HEADS):
        parts += [v[:, h * ATTN_VDIM:(h + 1) * ATTN_VDIM], ones_col]
    return jnp.concatenate(parts, axis=1)


def _inproj_kernel(x_ref, xp_ref, xn_ref, sc_ref, sh_ref, g_ref, w_ref, bd_ref, qg_ref, kg_ref,
                   cos_ref, sin_ref, cw_ref, q_ref, k_ref, v_ref, yc_ref):
    i = pl.program_id(1)
    last = pl.num_programs(1) - 1
    tm = x_ref.shape[1]
    w = QK_W
    xe = jnp.concatenate([xp_ref[0], x_ref[0], xn_ref[0]], axis=0)
    he = _modulate(xe, g_ref[...], sc_ref[0], sh_ref[0]).astype(BF16)
    hm = he[HALO:HALO + tm]

    qkv = _dot(hm, w_ref[:, :3 * w])
    bd = bd_ref[...]
    cos = cos_ref[...]
    sin = sin_ref[...]
    lane = lax.broadcasted_iota(jnp.int32, (1, w), 1)
    first_half = (lane % (ATTN_DIM // 2)) < (ATTN_DIM // 4)
    q = _rope(_group_rms(qkv[:, :w], bd, qg_ref[...]), cos, sin, first_half)
    k = _rope(_group_rms(qkv[:, w:2 * w], bd, kg_ref[...]), cos, sin, first_half)
    q_ref[0] = (q * (ATTN_DIM ** -0.5 * LOG2_E)).astype(BF16)
    k_ref[0] = k.astype(BF16)
    v_ref[0] = _with_ones_columns(qkv[:, 2 * w:].astype(BF16))

    gcu = _dot(he, w_ref[:, 4 * w:])
    cu = gcu[:, :w] * gcu[:, w:]
    row = lax.broadcasted_iota(jnp.int32, (tm + 2 * HALO, 1), 0)
    first_row = jnp.where(i == 0, HALO, 0)
    end_row = jnp.where(i == last, HALO + tm, tm + 2 * HALO)
    cu = jnp.where((row >= first_row) & (row < end_row), cu, 0.0)
    prev = pltpu.roll(cu, 1, axis=0)[HALO:HALO + tm]
    nxt = pltpu.roll(cu, tm + 2 * HALO - 1, axis=0)[HALO:HALO + tm]
    cw = cw_ref[...]
    conv = prev * cw[0:1] + cu[HALO:HALO + tm] * cw[1:2] + nxt * cw[2:3]
    gate_b = _dot(hm, w_ref[:, 3 * w:4 * w])
    yc_ref[0] = (gate_b * conv).astype(BF16)


def _inproj(x, mods, g, w_in, bd, qg, kg, cos, sin, conv_w):
    b, s, d = x.shape
    tm = TM_PROJ
    nh = tm // HALO
    n_halo_blocks = s // HALO
    tok = pl.BlockSpec((1, tm, QK_W), lambda bi, i: (bi, i, 0))
    out_sds = jax.ShapeDtypeStruct((b, s, QK_W), BF16)
    return pl.pallas_call(
        _inproj_kernel,
        grid=(b, s // tm),
        in_specs=[
            pl.BlockSpec((1, tm, d), lambda bi, i: (bi, i, 0)),
            pl.BlockSpec((1, HALO, d), lambda bi, i: (bi, jnp.maximum(i * nh - 1, 0), 0)),
            pl.BlockSpec((1, HALO, d),
                         lambda bi, i: (bi, jnp.minimum((i + 1) * nh, n_halo_blocks - 1), 0)),
            _mod_spec(0, 1, lambda bi, i: bi),
            _mod_spec(0, 0, lambda bi, i: bi),
            _const_spec((1, d)),
            _const_spec(w_in.shape),
            _const_spec(bd.shape),
            _const_spec((1, QK_W)),
            _const_spec((1, QK_W)),
            pl.BlockSpec((tm, QK_W), lambda bi, i: (i, 0)),
            pl.BlockSpec((tm, QK_W), lambda bi, i: (i, 0)),
            _const_spec(conv_w.shape),
        ],
        out_specs=[tok, tok, pl.BlockSpec((1, tm, 2 * QK_W), lambda bi, i: (bi, i, 0)), tok],
        out_shape=[out_sds, out_sds, jax.ShapeDtypeStruct((b, s, 2 * QK_W), BF16), out_sds],
        compiler_params=_params(2),
        name="inproj",
    )(x, x, x, mods, mods, g, w_in, bd, qg, kg, cos, sin, conv_w)


def _ctxproj_kernel(x_ref, sc_ref, sh_ref, g_ref, wk_ref, wv_ref, bd_ref, kg_ref, k_ref, v_ref):
    hc = _modulate(x_ref[0], g_ref[...], sc_ref[0], sh_ref[0]).astype(BF16)
    k_ref[0] = _group_rms(_dot(hc, wk_ref[...]), bd_ref[...], kg_ref[...]).astype(BF16)
    v_ref[0] = _with_ones_columns(_dot(hc, wv_ref[...]).astype(BF16))


def _ctxproj(ctx, mods, g, w_in, bd, kg, ctx_row):
    b, l, d = ctx.shape
    out = pl.BlockSpec((1, l, QK_W), lambda bi: (bi, 0, 0))
    out_sds = jax.ShapeDtypeStruct((b, l, QK_W), BF16)
    return pl.pallas_call(
        _ctxproj_kernel,
        grid=(b,),
        in_specs=[
            pl.BlockSpec((1, l, d), lambda bi: (bi, 0, 0)),
            _mod_spec(0, 1, lambda bi: ctx_row),
            _mod_spec(0, 0, lambda bi: ctx_row),
            _const_spec((1, d)),
            pl.BlockSpec((d, QK_W), lambda bi: (0, 1), pipeline_mode=pl.Buffered(1)),
            pl.BlockSpec((d, QK_W), lambda bi: (0, 2), pipeline_mode=pl.Buffered(1)),
            _const_spec(bd.shape),
            _const_spec((1, QK_W)),
        ],
        out_specs=[out, pl.BlockSpec((1, l, 2 * QK_W), lambda bi: (bi, 0, 0))],
        out_shape=[out_sds, jax.ShapeDtypeStruct((b, l, 2 * QK_W), BF16)],
        compiler_params=_params(1),
        name="ctxproj",
    )(ctx, mods, mods, g, w_in, w_in, bd, kg)


def _attn_kernel(q_ref, k_ref, kc_ref, v_ref, vc_ref, sg_ref, lq1_ref, lk1_ref, lq2_ref, lk2_ref,
                 o_ref, s_ref, e_ref, *, lambda_init):
    tq = q_ref.shape[1]
    n_ctx = kc_ref.shape[1] // ATTN_KC
    n_lat = k_ref.shape[1] // ATTN_KC
    chunks = [(kc_ref, vc_ref, c) for c in range(n_ctx)] + [(k_ref, v_ref, c) for c in range(n_lat)]
    lane = lax.broadcasted_iota(jnp.int32, (1, ATTN_VDIM), 1)
    first = lane < ATTN_DIM
    lam = (jnp.exp(jnp.sum(lq1_ref[...] * lk1_ref[...], keepdims=True))
           - jnp.exp(jnp.sum(lq2_ref[...] * lk2_ref[...], keepdims=True)) + lambda_init)
    nt = (((1,), (1,)), ((), ()))

    for rb in range(tq // ATTN_RB):
        rows = slice(rb * ATTN_RB, (rb + 1) * ATTN_RB)
        q = q_ref[0, rows, :]
        zero = jnp.zeros_like(q)
        outs = []
        for part in range(2):
            qx = jnp.where(first, q, zero) if part == 0 else jnp.where(first, zero, q)
            slot = (rb * 2 + part) % ATTN_SLOTS
            mpart = None
            for ci, (kr, _, c) in enumerate(chunks):
                sc = lax.dot_general(qx, kr[0, c * ATTN_KC:(c + 1) * ATTN_KC, :], nt,
                                     preferred_element_type=F32)
                s_ref[slot, ci] = sc
                cm = sc[:, :ATTN_VDIM]
                for j in range(1, ATTN_KC // ATTN_VDIM):
                    cm = jnp.maximum(cm, sc[:, j * ATTN_VDIM:(j + 1) * ATTN_VDIM])
                mpart = cm if mpart is None else jnp.maximum(mpart, cm)
            m = jnp.max(mpart, axis=-1, keepdims=True)
            for ci in range(len(chunks)):
                e_ref[slot, :, ci * ATTN_KC:(ci + 1) * ATTN_KC] = (
                    jnp.exp2(s_ref[slot, ci] - m).astype(BF16))
            n_c = n_ctx * ATTN_KC
            acc = _dot(e_ref[slot, :, :n_c], vc_ref[0]) + _dot(e_ref[slot, :, n_c:], v_ref[0])
            outs.append(acc[:, :ATTN_VDIM] * (1.0 / acc[:, ATTN_VDIM:ATTN_VDIM + 1]))
        o = outs[0] - lam * outs[1]
        ms = jnp.mean(o * o, axis=-1, keepdims=True)
        y = (o * lax.rsqrt(ms + EPS)) * sg_ref[...]
        o_ref[0, rows, :] = (y * (1.0 - lambda_init)).astype(BF16)


def _attention(q, k, kc, v, vc, subln_g, lq1, lk1, lq2, lk2, lambda_init):
    b, s, _ = q.shape
    l = kc.shape[1]
    hd = ATTN_VDIM
    va = 2 * ATTN_VDIM
    assert l % ATTN_KC == 0 and s % ATTN_KC == 0
    n_chunks = (l + s) // ATTN_KC
    vec = _const_spec((1, ATTN_DIM))
    return pl.pallas_call(
        functools.partial(_attn_kernel, lambda_init=lambda_init),
        grid=(b, ATTN_HEADS, s // TQ),
        in_specs=[
            pl.BlockSpec((1, TQ, hd), lambda bi, h, i: (bi, i, h)),
            pl.BlockSpec((1, s, hd), lambda bi, h, i: (bi, 0, h)),
            pl.BlockSpec((1, l, hd), lambda bi, h, i: (bi, 0, h)),
            pl.BlockSpec((1, s, va), lambda bi, h, i: (bi, 0, h)),
            pl.BlockSpec((1, l, va), lambda bi, h, i: (bi, 0, h)),
            _const_spec((1, hd)),
            vec, vec, vec, vec,
        ],
        out_specs=pl.BlockSpec((1, TQ, hd), lambda bi, h, i: (bi, i, h)),
        out_shape=jax.ShapeDtypeStruct((b, s, ATTN_HEADS * hd), BF16),
        scratch_shapes=[pltpu.VMEM((ATTN_SLOTS, n_chunks, ATTN_RB, ATTN_KC), F32),
                        pltpu.VMEM((ATTN_SLOTS, ATTN_RB, l + s), BF16)],
        compiler_params=_params(3),
        name="diff_attention",
    )(q, k, kc, v, vc, subln_g, lq1, lk1, lq2, lk2)


def _outproj_ffn_kernel(x_ref, at_ref, yc_ref, gm_ref, sf_ref, cf_ref, gf_ref, g_ref,
                        wo_ref, wg_ref, wu_ref, wd_ref, o_ref):
    half = wo_ref.shape[0] // 2
    d_ff = wg_ref.shape[1]
    for rb in range(x_ref.shape[1] // FFN_RB):
        rows = slice(rb * FFN_RB, (rb + 1) * FFN_RB)
        y = _dot(at_ref[0, rows, :], wo_ref[:half]) + _dot(yc_ref[0, rows, :], wo_ref[half:])
        h1 = x_ref[0, rows, :] + gm_ref[0] * y
        hf = _modulate(h1, g_ref[...], cf_ref[0], sf_ref[0]).astype(BF16)
        acc = jnp.zeros(h1.shape, F32)
        for f in range(d_ff // MXU_N):
            cols = slice(f * MXU_N, (f + 1) * MXU_N)
            a = jax.nn.silu(_dot(hf, wg_ref[:, cols])) * _dot(hf, wu_ref[:, cols])
            acc = acc + _dot(a.astype(BF16), wd_ref[cols, :])
        o_ref[0, rows, :] = h1 + gf_ref[0] * acc


def _outproj_ffn(x, attn, yconv, mods, g, w_out, wg, wu, wd):
    b, s, d = x.shape
    tm = TM_FFN
    return pl.pallas_call(
        _outproj_ffn_kernel,
        grid=(b, s // tm),
        in_specs=[
            pl.BlockSpec((1, tm, d), lambda bi, i: (bi, i, 0)),
            pl.BlockSpec((1, tm, QK_W), lambda bi, i: (bi, i, 0)),
            pl.BlockSpec((1, tm, QK_W), lambda bi, i: (bi, i, 0)),
            _mod_spec(0, 2, lambda bi, i: bi),
            _mod_spec(0, 3, lambda bi, i: bi),
            _mod_spec(0, 4, lambda bi, i: bi),
            _mod_spec(0, 5, lambda bi, i: bi),
            _const_spec((1, d)),
            _const_spec(w_out.shape),
            _const_spec(wg.shape),
            _const_spec(wu.shape),
            _const_spec(wd.shape),
        ],
        out_specs=pl.BlockSpec((1, tm, d), lambda bi, i: (bi, i, 0)),
        out_shape=jax.ShapeDtypeStruct((b, s, d), F32),
        compiler_params=_params(2),
        name="outproj_ffn",
    )(x, attn, yconv, mods, mods, mods, mods, g, w_out, wg, wu, wd)


def _sgu_kernel(x_ref, sm_ref, cm_ref, gm_ref, sf_ref, cf_ref, g_ref, gff_ref, win_ref, lng_ref,
                lnb_ref, ws_ref, bs_ref, wo_ref, rwh_ref, rwl_ref, h_ref, hf_ref, lg_ref, us_ref):
    tm, d = x_ref.shape[1:]
    gw = d // SGU_GROUPS
    for rb in range(tm // SGU_RB):
        r0 = rb * SGU_RB
        x = x_ref[0, r0:r0 + SGU_RB, :]
        hm = _modulate(x, g_ref[...], cm_ref[0], sm_ref[0]).astype(BF16)
        z = jax.nn.gelu(_dot(hm, win_ref[...]))
        v = z[:, d:]
        mu = jnp.mean(v, axis=-1, keepdims=True)
        vc = v - mu
        var = jnp.mean(vc * vc, axis=-1, keepdims=True)
        vn = ((vc * lax.rsqrt(var + EPS)) * lng_ref[...] + lnb_ref[...]).astype(BF16)
        for n in range(SGU_RB // CHUNK):
            rows = slice(n * CHUNK, (n + 1) * CHUNK)
            for gi in range(SGU_GROUPS):
                cols = slice(gi * gw, (gi + 1) * gw)
                s = _dot(ws_ref[gi], vn[rows, cols]) + bs_ref[:, cols]
                us_ref[r0 + n * CHUNK:r0 + (n + 1) * CHUNK, cols] = (z[rows, cols] * s).astype(BF16)
        h = x + gm_ref[0] * _dot(us_ref[r0:r0 + SGU_RB, :], wo_ref[...])
        h_ref[0, r0:r0 + SGU_RB, :] = h
        hf = _modulate(h, gff_ref[...], cf_ref[0], sf_ref[0])
        _rows_to_tiles(hf_ref, hf, row0=r0)
        hi, lo = _split_bf16(hf)
        lg_ref[0, r0:r0 + SGU_RB, :] = (
            _dot(hi, rwh_ref[...]) + (_dot(hi, rwl_ref[...]) + _dot(lo, rwh_ref[...])))


def _sgu(h, mods, g_mix, g_ffn, w_in, ln_g, ln_b, w_s, bias, w_out, rw_hi, rw_lo):
    b, s, d = h.shape
    tm = TM_SGU
    tok = pl.BlockSpec((1, tm, d), lambda bi, i: (bi, i, 0))
    return pl.pallas_call(
        _sgu_kernel,
        grid=(b, s // tm),
        in_specs=[
            tok,
            _mod_spec(1, 0, lambda bi, i: bi),
            _mod_spec(1, 1, lambda bi, i: bi),
            _mod_spec(1, 2, lambda bi, i: bi),
            _mod_spec(1, 3, lambda bi, i: bi),
            _mod_spec(1, 4, lambda bi, i: bi),
            _const_spec((1, d)),
            _const_spec((1, d)),
            _const_spec(w_in.shape),
            _const_spec((1, d)),
            _const_spec((1, d)),
            _const_spec(w_s.shape),
            _const_spec(bias.shape),
            _const_spec(w_out.shape),
            _const_spec(rw_hi.shape),
            _const_spec(rw_lo.shape),
        ],
        out_specs=[tok,
                   pl.BlockSpec((tm * SUBLANES, LANES), lambda bi, i: (bi * (s // tm) + i, 0)),
                   pl.BlockSpec((1, tm, N_EXPERTS), lambda bi, i: (bi, i, 0))],
        out_shape=[jax.ShapeDtypeStruct((b, s, d), F32),
                   jax.ShapeDtypeStruct((b * s * SUBLANES, LANES), F32),
                   jax.ShapeDtypeStruct((b, s, N_EXPERTS), F32)],
        scratch_shapes=[pltpu.VMEM((tm, d), BF16)],
        compiler_params=_params(2),
        name="sgu",
    )(h, mods, mods, mods, mods, mods, g_mix, g_ffn, w_in, ln_g, ln_b, w_s, bias, w_out,
      rw_hi, rw_lo)


def _route(logits, tile_rows, n_tiles):
    t = logits.shape[0]
    eid = lax.broadcasted_iota(jnp.int32, (t, N_EXPERTS), 1)
    i1 = jnp.argmax(logits, axis=-1).astype(jnp.int32)
    v1 = jnp.max(logits, axis=-1)
    rest = jnp.where(eid == i1[:, None], -jnp.inf, logits)
    i2 = jnp.argmax(rest, axis=-1).astype(jnp.int32)
    v2 = jnp.max(rest, axis=-1)
    e2 = jnp.exp(v2 - v1)
    gates = jnp.stack([1.0 / (1.0 + e2), e2 / (1.0 + e2)], axis=-1)
    onehot = ((eid == i1[:, None]) | (eid == i2[:, None])).astype(jnp.int32)
    csum = jnp.cumsum(onehot, axis=0)
    rank = csum - onehot
    counts = csum[-1]
    padded = ((counts + tile_rows - 1) // tile_rows) * tile_rows
    ends = jnp.cumsum(padded)
    starts = ends - padded
    idx = jnp.stack([i1, i2], axis=-1)
    pos = (starts[idx] + jnp.take_along_axis(rank, idx, axis=-1)).astype(jnp.int32)
    tile_start = jnp.arange(n_tiles, dtype=jnp.int32) * tile_rows
    valid = tile_start < ends[-1]
    tile_e = jnp.sum(tile_start[:, None] >= ends[None, :], axis=-1).astype(jnp.int32)
    last_e = jnp.max(jnp.where(counts > 0, jnp.arange(N_EXPERTS, dtype=jnp.int32), 0))
    tile_e = jnp.where(valid, tile_e, last_e)
    return pos.reshape(-1), gates, tile_e, valid.astype(jnp.int32)


def _row_tables(pos, n_tokens, n_tiles):
    n_rows = n_tiles * TM_EXP
    pair = jnp.full((n_rows,), -1, jnp.int32).at[pos].set(
        jnp.arange(n_tokens * TOP_K, dtype=jnp.int32), unique_indices=True)
    tok = pair >> 1
    scratch_row = n_tokens * TOP_K + jnp.arange(n_rows, dtype=jnp.int32) % TM_EXP
    src = jnp.where(pair < 0, 0, tok)
    dst = jnp.where(pair < 0, scratch_row, (pair & 1) * n_tokens + tok)
    return src * SUBLANES, jnp.concatenate([scratch_row[:TM_EXP], dst]) * SUBLANES


def _expert_kernel(te_ref, tv_ref, tf_ref, src_ref, dst_ref, hf_ref, wg_ref, wu_ref, wd_ref, y_ref,
                   x_ref, xb_ref, acc_ref, yt_ref, sem_in, sem_out):
    del te_ref, tf_ref
    tm = TM_EXP
    j = pl.program_id(0)
    step = pl.program_id(1)
    valid = tv_ref[j] > 0
    prev_valid = jnp.logical_and(j > 0, tv_ref[jnp.maximum(j - 1, 0)] > 0)
    prev2_valid = jnp.logical_and(j > 1, tv_ref[jnp.maximum(j - 2, 0)] > 0)
    slot = j % 2
    x_rows = lambda r: x_ref.at[pl.ds(r * SUBLANES, SUBLANES)]
    yt_rows = lambda sl, r: yt_ref.at[sl, pl.ds(r * SUBLANES, SUBLANES)]

    def gather(tile, r):
        t8 = pl.multiple_of(src_ref[tile * tm + r], SUBLANES)
        return pltpu.make_async_copy(hf_ref.at[pl.ds(t8, SUBLANES)], x_rows(r), sem_in)

    def gather_wait(r):
        return pltpu.make_async_copy(hf_ref.at[pl.ds(0, SUBLANES)], x_rows(r), sem_in)

    def scatter(tile, sl, r):
        d8 = pl.multiple_of(dst_ref[(tile + 1) * tm + r], SUBLANES)
        return pltpu.make_async_copy(yt_rows(sl, r), y_ref.at[pl.ds(d8, SUBLANES)], sem_out)

    def scatter_wait(sl, r):
        return pltpu.make_async_copy(yt_rows(sl, r), y_ref.at[pl.ds(0, SUBLANES)], sem_out)

    def ffn_step():
        xb = xb_ref[...]
        a = jax.nn.silu(_dot(xb, wg_ref[0])) * _dot(xb, wu_ref[0])
        return _dot(a.astype(BF16), wd_ref[0])

    @pl.when(jnp.logical_and(j == 0, step == 0))
    def _():
        yt_ref[...] = jnp.zeros_like(yt_ref)
        for r in range(tm):
            gather(0, r).start()

    @pl.when(jnp.logical_and(step == 0, jnp.logical_or(j == 0, prev_valid)))
    def _():
        for r in range(tm):
            gather_wait(r).wait()

    @pl.when(jnp.logical_and(valid, step == 0))
    def _():
        xb_ref[...] = _tiles_to_rows(x_ref, tm).astype(BF16)
        for r in range(tm):
            gather(j + 1, r).start(priority=GATHER_DMA_QUEUE)
        acc_ref[...] = ffn_step()

    @pl.when(jnp.logical_and(step == 1, jnp.logical_or(j == 1, prev2_valid)))
    def _():
        for r in range(tm):
            scatter_wait(slot, r).wait()

    @pl.when(jnp.logical_and(valid, step == 1))
    def _():
        for r in range(tm):
            scatter(j - 1, 1 - slot, r).start(priority=SCATTER_DMA_QUEUE)
        _rows_to_tiles(yt_ref.at[slot], acc_ref[...] + ffn_step())

    @pl.when(jnp.logical_and(jnp.logical_and(jnp.logical_not(valid), prev_valid), step == 1))
    def _():
        for r in range(tm):
            scatter(j - 1, 1 - slot, r).start(priority=SCATTER_DMA_QUEUE)


def _expert_ffn(tile_e, tile_valid, src, dst, hf, wg, wu, wd):
    n_tok = hf.shape[0] // SUBLANES
    d = D_MODEL
    n_tiles = tile_e.shape[0]
    d_ff = wg.shape[2]
    nf = d_ff // TF_EXP
    assert nf == 2
    j = jnp.arange(n_tiles, dtype=jnp.int32)
    first = j % 2
    n_valid = jnp.sum(tile_valid)
    last_f = 1 - (n_valid - 1) % 2
    half = jnp.where(tile_valid[:, None] > 0, jnp.stack([first, 1 - first], axis=1), last_f)
    tile_f = half.reshape(-1).astype(jnp.int32)

    return pl.pallas_call(
        _expert_kernel,
        grid_spec=pltpu.PrefetchScalarGridSpec(
            num_scalar_prefetch=5,
            grid=(n_tiles, nf),
            in_specs=[
                pl.BlockSpec(memory_space=pl.ANY),
                pl.BlockSpec((1, d, TF_EXP), lambda j, s, te, tv, tf, *_: (te[j], 0, tf[2 * j + s])),
                pl.BlockSpec((1, d, TF_EXP), lambda j, s, te, tv, tf, *_: (te[j], 0, tf[2 * j + s])),
                pl.BlockSpec((1, TF_EXP, d), lambda j, s, te, tv, tf, *_: (te[j], tf[2 * j + s], 0)),
            ],
            out_specs=pl.BlockSpec(memory_space=pl.ANY),
            scratch_shapes=[pltpu.VMEM((TM_EXP * SUBLANES, LANES), F32),
                            pltpu.VMEM((TM_EXP, d), BF16),
                            pltpu.VMEM((TM_EXP, d), F32),
                            pltpu.VMEM((2, TM_EXP * SUBLANES, LANES), F32),
                            pltpu.SemaphoreType.DMA(()), pltpu.SemaphoreType.DMA(())],
        ),
        out_shape=jax.ShapeDtypeStruct(((TOP_K * n_tok + TM_EXP) * SUBLANES, LANES), F32),
        compiler_params=_params(2),
        name="moe_experts",
    )(tile_e, tile_valid, tile_f, src, dst, hf, wg, wu, wd)


def _combine_kernel(y0_ref, y1_ref, h_ref, gate_ref, gf_ref, o_ref):
    gates = gate_ref[...]
    n = h_ref.shape[0]
    mix = gates[:, 0:1] * _tiles_to_rows(y0_ref, n) + gates[:, 1:2] * _tiles_to_rows(y1_ref, n)
    o_ref[...] = h_ref[...] + gf_ref[0] * mix


def _combine_rows(y, h, gates, mods, seq):
    t, d = h.shape
    steps_per_batch = seq // TG
    second = t // TG
    tok = pl.BlockSpec((TG, d), lambda i: (i, 0))
    return pl.pallas_call(
        _combine_kernel,
        grid=(t // TG,),
        in_specs=[
            pl.BlockSpec((TG * SUBLANES, LANES), lambda i: (i, 0)),
            pl.BlockSpec((TG * SUBLANES, LANES), lambda i: (second + i, 0)),
            tok,
            pl.BlockSpec((TG, TOP_K), lambda i: (i, 0)),
            pl.BlockSpec((1, 1, d), lambda i: (COND_ROWS + i // steps_per_batch, 0, N_MODS - 1)),
        ],
        out_specs=tok,
        out_shape=jax.ShapeDtypeStruct((t, d), F32),
        compiler_params=_params(1),
        name="moe_combine",
    )(y, y, h, gates, mods)


def _rope_tables(seq):
    rows = seq // GRID_W
    row = jnp.repeat(jnp.arange(rows), GRID_W).astype(F32)
    col = jnp.tile(jnp.arange(GRID_W), rows).astype(F32)
    quarter = ATTN_DIM // 4
    inv = ROPE_BASE ** (-jnp.arange(quarter, dtype=F32) / quarter)
    ar = row[:, None] * inv
    ac = col[:, None] * inv
    ang = jnp.concatenate([ar, ar, ac, ac], axis=-1)
    sign = jnp.tile(jnp.concatenate([-jnp.ones(quarter, F32), jnp.ones(quarter, F32)]), 2)
    reps = QK_W // ATTN_DIM
    return jnp.tile(jnp.cos(ang), (1, reps)), jnp.tile(jnp.sin(ang) * sign, (1, reps))


def kernel(x, c, ctx, c_ctx, ada_w, ada_b, norm_mix_g, norm_ffn_g, w_in_even, q_norm_g, k_norm_g,
           lam_q1, lam_k1, lam_q2, lam_k2, subln_g, conv_w, w_out_even, ffn_w_gate, ffn_w_up,
           ffn_w_down, sgu_w_in, sgu_ln_g, sgu_ln_b, sgu_w_s, sgu_b_s, sgu_w_out, router_w,
           moe_w_gate, moe_w_up, moe_w_down):
    b, s, d = x.shape
    assert d == D_MODEL and b < COND_ROWS and ada_w.shape[0] == 2
    assert s % TM_PROJ == 0 and s % TQ == 0 and s % TM_FFN == 0 and s % TM_SGU == 0 and s % TG == 0

    cond = jnp.zeros((COND_ROWS, d), F32).at[:b].set(c).at[b].set(c_ctx)
    mods = _ada_mods(cond, ada_w, ada_b)

    lambda_init = 0.8 - 0.6 * math.exp(-0.3 * 0)
    group = jnp.arange(QK_W) // ATTN_DIM
    bd = (group[:, None] == group[None, :]).astype(BF16)
    reps = QK_W // ATTN_DIM
    qg = jnp.tile(q_norm_g[0], reps)[None, :]
    kg = jnp.tile(k_norm_g[0], reps)[None, :]
    cos, sin = _rope_tables(s)
    w_in = w_in_even[0].astype(BF16)
    g_mix0 = norm_mix_g[0][None, :]
    q, k, v, yconv = _inproj(x, mods, g_mix0, w_in, bd, qg, kg, cos, sin, conv_w[0])
    kc, vc = _ctxproj(ctx, mods, g_mix0, w_in, bd, kg, ctx_row=b)
    attn = _attention(q, k, kc, v, vc, subln_g[0][None, :], lam_q1[0][None, :], lam_k1[0][None, :],
                      lam_q2[0][None, :], lam_k2[0][None, :], lambda_init)
    h = _outproj_ffn(x, attn, yconv, mods, norm_ffn_g[0][None, :], w_out_even[0].astype(BF16),
                     ffn_w_gate[0].astype(BF16), ffn_w_up[0].astype(BF16),
                     ffn_w_down[0].astype(BF16))

    gw = d // SGU_GROUPS
    bias = jnp.repeat(sgu_b_s[0].T, gw, axis=1)
    rw_hi = router_w[0].astype(BF16)
    rw_lo = (router_w[0] - rw_hi.astype(F32)).astype(BF16)
    h, hf, logits = _sgu(h, mods, norm_mix_g[1][None, :], norm_ffn_g[1][None, :],
                         sgu_w_in[0].astype(BF16), sgu_ln_g[0][None, :], sgu_ln_b[0][None, :],
                         sgu_w_s[0].astype(BF16), bias, sgu_w_out[0].astype(BF16), rw_hi, rw_lo)

    t = b * s
    n_tiles = (t * TOP_K) // TM_EXP + N_EXPERTS + 2
    pos, gates, tile_e, tile_valid = _route(logits.reshape(t, N_EXPERTS), TM_EXP, n_tiles)
    src, dst = _row_tables(pos, t, n_tiles)
    y = _expert_ffn(tile_e, tile_valid, src, dst, hf, moe_w_gate[0].astype(BF16),
                    moe_w_up[0].astype(BF16), moe_w_down[0].astype(BF16))
    out = _combine_rows(y, h.reshape(t, d), gates, mods, s)
    return out.reshape(b, s, d)
```

```python
import functools
import math

import jax
import jax.numpy as jnp
from jax import lax
from jax.experimental import pallas as pl
from jax.experimental.pallas import tpu as pltpu

F32 = jnp.float32
BF16 = jnp.bfloat16

D_MODEL = 1024
GRID_W = 64
ATTN_HEADS = 4
ATTN_DIM = 64
ATTN_VDIM = 2 * ATTN_DIM
QK_W = ATTN_HEADS * 2 * ATTN_DIM
CONV_K = 3
ROPE_BASE = 10000.0
SGU_GROUPS = 4
CHUNK = 128
N_EXPERTS = 8
TOP_K = 2
EPS = 1e-6
LOG2_E = math.log2(math.e)
N_MODS = 6
COND_ROWS = 16

V7X_VMEM_LIMIT = 56 * 1024 * 1024
MXU_N = 256
TM_PROJ = 512
HALO = 16
TQ = 2048
ATTN_RB = 512
ATTN_SLOTS = 4
ATTN_KC = 256
TM_FFN = 1024
FFN_RB = 1024
TM_SGU = 512
SGU_RB = 256
TM_EXP = 512
TF_EXP = 1792
TG = 256
ROW_DMA_PRIORITY = 1


def _params(n_axes):
    return pltpu.CompilerParams(
        dimension_semantics=("arbitrary",) * n_axes, vmem_limit_bytes=V7X_VMEM_LIMIT)


def _const_spec(shape):
    zeros = (0,) * len(shape)
    return pl.BlockSpec(shape, lambda *_: zeros, pipeline_mode=pl.Buffered(1))


def _dot(a, b):
    return jnp.dot(a, b, preferred_element_type=F32)


def _modulate(x, g, scale, shift):
    ms = jnp.mean(x * x, axis=-1, keepdims=True)
    y = x * lax.rsqrt(ms + EPS)
    return (y * g) * (1.0 + scale) + shift


def _split_bf16(x):
    hi = x.astype(BF16)
    lo = (x - hi.astype(F32)).astype(BF16)
    return hi, lo


assert D_MODEL == 8 * 128
SUBLANES = 8
LANES = 128


def _rows_to_tiles(ref, val, row0=0):
    n = val.shape[0]
    for c in range(SUBLANES):
        ref[pl.ds(row0 * SUBLANES + c, n, stride=SUBLANES), :] = val[:, c * LANES:(c + 1) * LANES]


def _tiles_to_rows(ref, n):
    return jnp.concatenate(
        [ref[pl.ds(c, n, stride=SUBLANES), :] for c in range(SUBLANES)], axis=1)


def _ada_kernel(cond_ref, w_ref, b_ref, o_ref):
    a = jax.nn.silu(cond_ref[...]).astype(BF16)
    o_ref[0] = _dot(a, w_ref[0].astype(BF16)) + b_ref[0]


def _ada_mods(cond, ada_w, ada_b):
    depth = ada_w.shape[0]
    d = D_MODEL
    out = pl.pallas_call(
        _ada_kernel,
        grid=(depth, N_MODS),
        in_specs=[
            pl.BlockSpec((COND_ROWS, d), lambda l, n: (0, 0)),
            pl.BlockSpec((1, d, d), lambda l, n: (l, 0, n)),
            pl.BlockSpec((1, 1, d), lambda l, n: (l, 0, n)),
        ],
        out_specs=pl.BlockSpec((1, COND_ROWS, d), lambda l, n: (l, 0, n)),
        out_shape=jax.ShapeDtypeStruct((depth, COND_ROWS, N_MODS * d), F32),
        compiler_params=_params(2),
        name="ada_mods",
    )(cond, ada_w, ada_b.reshape(depth, 1, N_MODS * d))
    return out.reshape(depth * COND_ROWS, 1, N_MODS * d)


def _mod_spec(layer, chunk, row_of):
    return pl.BlockSpec((1, 1, D_MODEL),
                        lambda *ids: (layer * COND_ROWS + row_of(*ids), 0, chunk))


def _group_rms(t, bd, gain):
    hi, lo = _split_bf16(t * t)
    ss = _dot(hi, bd) + _dot(lo, bd)
    return (t * lax.rsqrt(ss * (1.0 / ATTN_DIM) + EPS)) * gain


def _rope(t, cos, sin_signed, first_half):
    width = t.shape[-1]
    quarter = ATTN_DIM // 4
    ahead = pltpu.roll(t, width - quarter, axis=1)
    behind = pltpu.roll(t, quarter, axis=1)
    return t * cos + jnp.where(first_half, ahead, behind) * sin_signed


def _with_ones_columns(v):
    rows = v.shape[0]
    lane = lax.broadcasted_iota(jnp.int32, (rows, ATTN_VDIM), 1)
    ones_col = jnp.where(lane == 0, 1.0, 0.0).astype(v.dtype)
    parts = []
    for h in range(ATTN_HEADS):
        parts += [v[:, h * ATTN_VDIM:(h + 1) * ATTN_VDIM], ones_col]
    return jnp.concatenate(parts, axis=1)


def _inproj_kernel(x_ref, xp_ref, xn_ref, sc_ref, sh_ref, g_ref, w_ref, bd_ref, qg_ref, kg_ref,
                   cos_ref, sin_ref, cw_ref, q_ref, k_ref, v_ref, yc_ref):
    i = pl.program_id(1)
    last = pl.num_programs(1) - 1
    tm = x_ref.shape[1]
    w = QK_W
    xe = jnp.concatenate([xp_ref[0], x_ref[0], xn_ref[0]], axis=0)
    he = _modulate(xe, g_ref[...], sc_ref[0], sh_ref[0]).astype(BF16)
    hm = he[HALO:HALO + tm]

    qkv = _dot(hm, w_ref[:, :3 * w])
    bd = bd_ref[...]
    cos = cos_ref[...]
    sin = sin_ref[...]
    lane = lax.broadcasted_iota(jnp.int32, (1, w), 1)
    first_half = (lane % (ATTN_DIM // 2)) < (ATTN_DIM // 4)
    q = _rope(_group_rms(qkv[:, :w], bd, qg_ref[...]), cos, sin, first_half)
    k = _rope(_group_rms(qkv[:, w:2 * w], bd, kg_ref[...]), cos, sin, first_half)
    q_ref[0] = (q * (ATTN_DIM ** -0.5 * LOG2_E)).astype(BF16)
    k_ref[0] = k.astype(BF16)
    v_ref[0] = _with_ones_columns(qkv[:, 2 * w:].astype(BF16))

    gcu = _dot(he, w_ref[:, 4 * w:])
    cu = gcu[:, :w] * gcu[:, w:]
    row = lax.broadcasted_iota(jnp.int32, (tm + 2 * HALO, 1), 0)
    first_row = jnp.where(i == 0, HALO, 0)
    end_row = jnp.where(i == last, HALO + tm, tm + 2 * HALO)
    cu = jnp.where((row >= first_row) & (row < end_row), cu, 0.0)
    prev = pltpu.roll(cu, 1, axis=0)[HALO:HALO + tm]
    nxt = pltpu.roll(cu, tm + 2 * HALO - 1, axis=0)[HALO:HALO + tm]
    cw = cw_ref[...]
    conv = prev * cw[0:1] + cu[HALO:HALO + tm] * cw[1:2] + nxt * cw[2:3]
    gate_b = _dot(hm, w_ref[:, 3 * w:4 * w])
    yc_ref[0] = (gate_b * conv).astype(BF16)


def _inproj(x, mods, g, w_in, bd, qg, kg, cos, sin, conv_w):
    b, s, d = x.shape
    tm = TM_PROJ
    nh = tm // HALO
    n_halo_blocks = s // HALO
    tok = pl.BlockSpec((1, tm, QK_W), lambda bi, i: (bi, i, 0))
    out_sds = jax.ShapeDtypeStruct((b, s, QK_W), BF16)
    return pl.pallas_call(
        _inproj_kernel,
        grid=(b, s // tm),
        in_specs=[
            pl.BlockSpec((1, tm, d), lambda bi, i: (bi, i, 0)),
            pl.BlockSpec((1, HALO, d), lambda bi, i: (bi, jnp.maximum(i * nh - 1, 0), 0)),
            pl.BlockSpec((1, HALO, d),
                         lambda bi, i: (bi, jnp.minimum((i + 1) * nh, n_halo_blocks - 1), 0)),
            _mod_spec(0, 1, lambda bi, i: bi),
            _mod_spec(0, 0, lambda bi, i: bi),
            _const_spec((1, d)),
            _const_spec(w_in.shape),
            _const_spec(bd.shape),
            _const_spec((1, QK_W)),
            _const_spec((1, QK_W)),
            pl.BlockSpec((tm, QK_W), lambda bi, i: (i, 0)),
            pl.BlockSpec((tm, QK_W), lambda bi, i: (i, 0)),
            _const_spec(conv_w.shape),
        ],
        out_specs=[tok, tok, pl.BlockSpec((1, tm, 2 * QK_W), lambda bi, i: (bi, i, 0)), tok],
        out_shape=[out_sds, out_sds, jax.ShapeDtypeStruct((b, s, 2 * QK_W), BF16), out_sds],
        compiler_params=_params(2),
        name="inproj",
    )(x, x, x, mods, mods, g, w_in, bd, qg, kg, cos, sin, conv_w)


def _ctxproj_kernel(x_ref, sc_ref, sh_ref, g_ref, wk_ref, wv_ref, bd_ref, kg_ref, k_ref, v_ref):
    hc = _modulate(x_ref[0], g_ref[...], sc_ref[0], sh_ref[0]).astype(BF16)
    k_ref[0] = _group_rms(_dot(hc, wk_ref[...]), bd_ref[...], kg_ref[...]).astype(BF16)
    v_ref[0] = _with_ones_columns(_dot(hc, wv_ref[...]).astype(BF16))


def _ctxproj(ctx, mods, g, w_in, bd, kg, ctx_row):
    b, l, d = ctx.shape
    out = pl.BlockSpec((1, l, QK_W), lambda bi: (bi, 0, 0))
    out_sds = jax.ShapeDtypeStruct((b, l, QK_W), BF16)
    return pl.pallas_call(
        _ctxproj_kernel,
        grid=(b,),
        in_specs=[
            pl.BlockSpec((1, l, d), lambda bi: (bi, 0, 0)),
            _mod_spec(0, 1, lambda bi: ctx_row),
            _mod_spec(0, 0, lambda bi: ctx_row),
            _const_spec((1, d)),
            pl.BlockSpec((d, QK_W), lambda bi: (0, 1), pipeline_mode=pl.Buffered(1)),
            pl.BlockSpec((d, QK_W), lambda bi: (0, 2), pipeline_mode=pl.Buffered(1)),
            _const_spec(bd.shape),
            _const_spec((1, QK_W)),
        ],
        out_specs=[out, pl.BlockSpec((1, l, 2 * QK_W), lambda bi: (bi, 0, 0))],
        out_shape=[out_sds, jax.ShapeDtypeStruct((b, l, 2 * QK_W), BF16)],
        compiler_params=_params(1),
        name="ctxproj",
    )(ctx, mods, mods, g, w_in, w_in, bd, kg)


def _attn_kernel(q_ref, k_ref, kc_ref, v_ref, vc_ref, sg_ref, lq1_ref, lk1_ref, lq2_ref, lk2_ref,
                 o_ref, s_ref, e_ref, *, lambda_init):
    tq = q_ref.shape[1]
    n_ctx = kc_ref.shape[1] // ATTN_KC
    n_lat = k_ref.shape[1] // ATTN_KC
    chunks = [(kc_ref, vc_ref, c) for c in range(n_ctx)] + [(k_ref, v_ref, c) for c in range(n_lat)]
    lane = lax.broadcasted_iota(jnp.int32, (1, ATTN_VDIM), 1)
    first = lane < ATTN_DIM
    lam = (jnp.exp(jnp.sum(lq1_ref[...] * lk1_ref[...], keepdims=True))
           - jnp.exp(jnp.sum(lq2_ref[...] * lk2_ref[...], keepdims=True)) + lambda_init)
    nt = (((1,), (1,)), ((), ()))

    for rb in range(tq // ATTN_RB):
        rows = slice(rb * ATTN_RB, (rb + 1) * ATTN_RB)
        q = q_ref[0, rows, :]
        zero = jnp.zeros_like(q)
        outs = []
        for part in range(2):
            qx = jnp.where(first, q, zero) if part == 0 else jnp.where(first, zero, q)
            slot = (rb * 2 + part) % ATTN_SLOTS
            mpart = None
            for ci, (kr, _, c) in enumerate(chunks):
                sc = lax.dot_general(qx, kr[0, c * ATTN_KC:(c + 1) * ATTN_KC, :], nt,
                                     preferred_element_type=F32)
                s_ref[slot, ci] = sc
                cm = sc[:, :ATTN_VDIM]
                for j in range(1, ATTN_KC // ATTN_VDIM):
                    cm = jnp.maximum(cm, sc[:, j * ATTN_VDIM:(j + 1) * ATTN_VDIM])
                mpart = cm if mpart is None else jnp.maximum(mpart, cm)
            m = jnp.max(mpart, axis=-1, keepdims=True)
            for ci in range(len(chunks)):
                e_ref[slot, :, ci * ATTN_KC:(ci + 1) * ATTN_KC] = (
                    jnp.exp2(s_ref[slot, ci] - m).astype(BF16))
            n_c = n_ctx * ATTN_KC
            acc = _dot(e_ref[slot, :, :n_c], vc_ref[0]) + _dot(e_ref[slot, :, n_c:], v_ref[0])
            outs.append(acc[:, :ATTN_VDIM] * (1.0 / acc[:, ATTN_VDIM:ATTN_VDIM + 1]))
        o = outs[0] - lam * outs[1]
        ms = jnp.mean(o * o, axis=-1, keepdims=True)
        y = (o * lax.rsqrt(ms + EPS)) * sg_ref[...]
        o_ref[0, rows, :] = (y * (1.0 - lambda_init)).astype(BF16)


def _attention(q, k, kc, v, vc, subln_g, lq1, lk1, lq2, lk2, lambda_init):
    b, s, _ = q.shape
    l = kc.shape[1]
    hd = ATTN_VDIM
    va = 2 * ATTN_VDIM
    assert l % ATTN_KC == 0 and s % ATTN_KC == 0
    n_chunks = (l + s) // ATTN_KC
    vec = _const_spec((1, ATTN_DIM))
    return pl.pallas_call(
        functools.partial(_attn_kernel, lambda_init=lambda_init),
        grid=(b, ATTN_HEADS, s // TQ),
        in_specs=[
            pl.BlockSpec((1, TQ, hd), lambda bi, h, i: (bi, i, h)),
            pl.BlockSpec((1, s, hd), lambda bi, h, i: (bi, 0, h)),
            pl.BlockSpec((1, l, hd), lambda bi, h, i: (bi, 0, h)),
            pl.BlockSpec((1, s, va), lambda bi, h, i: (bi, 0, h)),
            pl.BlockSpec((1, l, va), lambda bi, h, i: (bi, 0, h)),
            _const_spec((1, hd)),
            vec, vec, vec, vec,
        ],
        out_specs=pl.BlockSpec((1, TQ, hd), lambda bi, h, i: (bi, i, h)),
        out_shape=jax.ShapeDtypeStruct((b, s, ATTN_HEADS * hd), BF16),
        scratch_shapes=[pltpu.VMEM((ATTN_SLOTS, n_chunks, ATTN_RB, ATTN_KC), F32),
                        pltpu.VMEM((ATTN_SLOTS, ATTN_RB, l + s), BF16)],
        compiler_params=_params(3),
        name="diff_attention",
    )(q, k, kc, v, vc, subln_g, lq1, lk1, lq2, lk2)


def _outproj_ffn_kernel(x_ref, at_ref, yc_ref, gm_ref, sf_ref, cf_ref, gf_ref, g_ref,
                        wo_ref, wg_ref, wu_ref, wd_ref, o_ref):
    half = wo_ref.shape[0] // 2
    d_ff = wg_ref.shape[1]
    for rb in range(x_ref.shape[1] // FFN_RB):
        rows = slice(rb * FFN_RB, (rb + 1) * FFN_RB)
        y = _dot(at_ref[0, rows, :], wo_ref[:half]) + _dot(yc_ref[0, rows, :], wo_ref[half:])
        h1 = x_ref[0, rows, :] + gm_ref[0] * y
        hf = _modulate(h1, g_ref[...], cf_ref[0], sf_ref[0]).astype(BF16)
        acc = jnp.zeros(h1.shape, F32)
        for f in range(d_ff // MXU_N):
            cols = slice(f * MXU_N, (f + 1) * MXU_N)
            a = jax.nn.silu(_dot(hf, wg_ref[:, cols])) * _dot(hf, wu_ref[:, cols])
            acc = acc + _dot(a.astype(BF16), wd_ref[cols, :])
        o_ref[0, rows, :] = h1 + gf_ref[0] * acc


def _outproj_ffn(x, attn, yconv, mods, g, w_out, wg, wu, wd):
    b, s, d = x.shape
    tm = TM_FFN
    return pl.pallas_call(
        _outproj_ffn_kernel,
        grid=(b, s // tm),
        in_specs=[
            pl.BlockSpec((1, tm, d), lambda bi, i: (bi, i, 0)),
            pl.BlockSpec((1, tm, QK_W), lambda bi, i: (bi, i, 0)),
            pl.BlockSpec((1, tm, QK_W), lambda bi, i: (bi, i, 0)),
            _mod_spec(0, 2, lambda bi, i: bi),
            _mod_spec(0, 3, lambda bi, i: bi),
            _mod_spec(0, 4, lambda bi, i: bi),
            _mod_spec(0, 5, lambda bi, i: bi),
            _const_spec((1, d)),
            _const_spec(w_out.shape),
            _const_spec(wg.shape),
            _const_spec(wu.shape),
            _const_spec(wd.shape),
        ],
        out_specs=pl.BlockSpec((1, tm, d), lambda bi, i: (bi, i, 0)),
        out_shape=jax.ShapeDtypeStruct((b, s, d), F32),
        compiler_params=_params(2),
        name="outproj_ffn",
    )(x, attn, yconv, mods, mods, mods, mods, g, w_out, wg, wu, wd)


def _sgu_kernel(x_ref, sm_ref, cm_ref, gm_ref, sf_ref, cf_ref, g_ref, gff_ref, win_ref, lng_ref,
                lnb_ref, ws_ref, bs_ref, wo_ref, rwh_ref, rwl_ref, h_ref, hf_ref, lg_ref, us_ref):
    tm, d = x_ref.shape[1:]
    gw = d // SGU_GROUPS
    for rb in range(tm // SGU_RB):
        r0 = rb * SGU_RB
        x = x_ref[0, r0:r0 + SGU_RB, :]
        hm = _modulate(x, g_ref[...], cm_ref[0], sm_ref[0]).astype(BF16)
        z = jax.nn.gelu(_dot(hm, win_ref[...]))
        v = z[:, d:]
        mu = jnp.mean(v, axis=-1, keepdims=True)
        vc = v - mu
        var = jnp.mean(vc * vc, axis=-1, keepdims=True)
        vn = ((vc * lax.rsqrt(var + EPS)) * lng_ref[...] + lnb_ref[...]).astype(BF16)
        for n in range(SGU_RB // CHUNK):
            rows = slice(n * CHUNK, (n + 1) * CHUNK)
            for gi in range(SGU_GROUPS):
                cols = slice(gi * gw, (gi + 1) * gw)
                s = _dot(ws_ref[gi], vn[rows, cols]) + bs_ref[:, cols]
                us_ref[r0 + n * CHUNK:r0 + (n + 1) * CHUNK, cols] = (z[rows, cols] * s).astype(BF16)
        h = x + gm_ref[0] * _dot(us_ref[r0:r0 + SGU_RB, :], wo_ref[...])
        h_ref[0, r0:r0 + SGU_RB, :] = h
        hf = _modulate(h, gff_ref[...], cf_ref[0], sf_ref[0])
        _rows_to_tiles(hf_ref, hf, row0=r0)
        hi, lo = _split_bf16(hf)
        lg_ref[0, r0:r0 + SGU_RB, :] = (
            _dot(hi, rwh_ref[...]) + (_dot(hi, rwl_ref[...]) + _dot(lo, rwh_ref[...])))


def _sgu(h, mods, g_mix, g_ffn, w_in, ln_g, ln_b, w_s, bias, w_out, rw_hi, rw_lo):
    b, s, d = h.shape
    tm = TM_SGU
    tok = pl.BlockSpec((1, tm, d), lambda bi, i: (bi, i, 0))
    return pl.pallas_call(
        _sgu_kernel,
        grid=(b, s // tm),
        in_specs=[
            tok,
            _mod_spec(1, 0, lambda bi, i: bi),
            _mod_spec(1, 1, lambda bi, i: bi),
            _mod_spec(1, 2, lambda bi, i: bi),
            _mod_spec(1, 3, lambda bi, i: bi),
            _mod_spec(1, 4, lambda bi, i: bi),
            _const_spec((1, d)),
            _const_spec((1, d)),
            _const_spec(w_in.shape),
            _const_spec((1, d)),
            _const_spec((1, d)),
            _const_spec(w_s.shape),
            _const_spec(bias.shape),
            _const_spec(w_out.shape),
            _const_spec(rw_hi.shape),
            _const_spec(rw_lo.shape),
        ],
        out_specs=[tok,
                   pl.BlockSpec((tm * SUBLANES, LANES), lambda bi, i: (bi * (s // tm) + i, 0)),
                   pl.BlockSpec((1, tm, N_EXPERTS), lambda bi, i: (bi, i, 0))],
        out_shape=[jax.ShapeDtypeStruct((b, s, d), F32),
                   jax.ShapeDtypeStruct((b * s * SUBLANES, LANES), F32),
                   jax.ShapeDtypeStruct((b, s, N_EXPERTS), F32)],
        scratch_shapes=[pltpu.VMEM((tm, d), BF16)],
        compiler_params=_params(2),
        name="sgu",
    )(h, mods, mods, mods, mods, mods, g_mix, g_ffn, w_in, ln_g, ln_b, w_s, bias, w_out,
      rw_hi, rw_lo)


def _route(logits, tile_rows, n_tiles):
    t = logits.shape[0]
    eid = lax.broadcasted_iota(jnp.int32, (t, N_EXPERTS), 1)
    i1 = jnp.argmax(logits, axis=-1).astype(jnp.int32)
    v1 = jnp.max(logits, axis=-1)
    rest = jnp.where(eid == i1[:, None], -jnp.inf, logits)
    i2 = jnp.argmax(rest, axis=-1).astype(jnp.int32)
    v2 = jnp.max(rest, axis=-1)
    e2 = jnp.exp(v2 - v1)
    gates = jnp.stack([1.0 / (1.0 + e2), e2 / (1.0 + e2)], axis=-1)
    onehot = ((eid == i1[:, None]) | (eid == i2[:, None])).astype(jnp.int32)
    csum = jnp.cumsum(onehot, axis=0)
    rank = csum - onehot
    counts = csum[-1]
    padded = ((counts + tile_rows - 1) // tile_rows) * tile_rows
    ends = jnp.cumsum(padded)
    starts = ends - padded
    idx = jnp.stack([i1, i2], axis=-1)
    pos = (starts[idx] + jnp.take_along_axis(rank, idx, axis=-1)).astype(jnp.int32)
    tile_start = jnp.arange(n_tiles, dtype=jnp.int32) * tile_rows
    valid = tile_start < ends[-1]
    tile_e = jnp.sum(tile_start[:, None] >= ends[None, :], axis=-1).astype(jnp.int32)
    last_e = jnp.max(jnp.where(counts > 0, jnp.arange(N_EXPERTS, dtype=jnp.int32), 0))
    tile_e = jnp.where(valid, tile_e, last_e)
    return pos.reshape(-1), gates, tile_e, valid.astype(jnp.int32)


def _row_tables(pos, n_tokens, n_tiles):
    n_rows = n_tiles * TM_EXP
    pair = jnp.full((n_rows,), -1, jnp.int32).at[pos].set(
        jnp.arange(n_tokens * TOP_K, dtype=jnp.int32), unique_indices=True)
    tok = pair >> 1
    scratch_row = n_tokens * TOP_K + jnp.arange(n_rows, dtype=jnp.int32) % TM_EXP
    src = jnp.where(pair < 0, 0, tok)
    dst = jnp.where(pair < 0, scratch_row, (pair & 1) * n_tokens + tok)
    return src * SUBLANES, jnp.concatenate([scratch_row[:TM_EXP], dst]) * SUBLANES


def _expert_kernel(te_ref, tv_ref, tf_ref, src_ref, dst_ref, hf_ref, wg_ref, wu_ref, wd_ref, y_ref,
                   x_ref, xb_ref, acc_ref, yt_ref, sem_in, sem_out):
    del te_ref, tf_ref
    tm = TM_EXP
    j = pl.program_id(0)
    step = pl.program_id(1)
    valid = tv_ref[j] > 0
    prev_valid = jnp.logical_and(j > 0, tv_ref[jnp.maximum(j - 1, 0)] > 0)
    prev2_valid = jnp.logical_and(j > 1, tv_ref[jnp.maximum(j - 2, 0)] > 0)
    slot = j % 2
    x_rows = lambda r: x_ref.at[pl.ds(r * SUBLANES, SUBLANES)]
    yt_rows = lambda sl, r: yt_ref.at[sl, pl.ds(r * SUBLANES, SUBLANES)]

    def start_gathers(tile):
        chain = 0
        for r in range(tm):
            t8 = src_ref[tile * tm + r + chain]
            chain = t8 >> 31
            pltpu.make_async_copy(hf_ref.at[pl.ds(pl.multiple_of(t8, SUBLANES), SUBLANES)],
                                  x_rows(r), sem_in).start(priority=ROW_DMA_PRIORITY)

    def gather_wait(r):
        return pltpu.make_async_copy(hf_ref.at[pl.ds(0, SUBLANES)], x_rows(r), sem_in)

    def start_scatters(tile, sl):
        chain = 0
        for r in range(tm):
            d8 = dst_ref[(tile + 1) * tm + r + chain]
            chain = d8 >> 31
            pltpu.make_async_copy(yt_rows(sl, r),
                                  y_ref.at[pl.ds(pl.multiple_of(d8, SUBLANES), SUBLANES)],
                                  sem_out).start(priority=ROW_DMA_PRIORITY)

    def scatter_wait(sl, r):
        return pltpu.make_async_copy(yt_rows(sl, r), y_ref.at[pl.ds(0, SUBLANES)], sem_out)

    def ffn_step():
        xb = xb_ref[...]
        a = jax.nn.silu(_dot(xb, wg_ref[0])) * _dot(xb, wu_ref[0])
        return _dot(a.astype(BF16), wd_ref[0])

    @pl.when(jnp.logical_and(j == 0, step == 0))
    def _():
        yt_ref[...] = jnp.zeros_like(yt_ref)
        start_gathers(0)

    @pl.when(jnp.logical_and(step == 0, jnp.logical_or(j == 0, prev_valid)))
    def _():
        for r in range(tm):
            gather_wait(r).wait()

    @pl.when(jnp.logical_and(valid, step == 0))
    def _():
        xb_ref[...] = _tiles_to_rows(x_ref, tm).astype(BF16)
        start_gathers(j + 1)
        acc_ref[...] = ffn_step()

    @pl.when(jnp.logical_and(step == 1, jnp.logical_or(j == 1, prev2_valid)))
    def _():
        for r in range(tm):
            scatter_wait(slot, r).wait()

    @pl.when(jnp.logical_and(valid, step == 1))
    def _():
        start_scatters(j - 1, 1 - slot)
        _rows_to_tiles(yt_ref.at[slot], acc_ref[...] + ffn_step())

    @pl.when(jnp.logical_and(jnp.logical_and(jnp.logical_not(valid), prev_valid), step == 1))
    def _():
        start_scatters(j - 1, 1 - slot)


def _expert_ffn(tile_e, tile_valid, src, dst, hf, wg, wu, wd):
    n_tok = hf.shape[0] // SUBLANES
    d = D_MODEL
    n_tiles = tile_e.shape[0]
    d_ff = wg.shape[2]
    nf = d_ff // TF_EXP
    assert nf == 2
    j = jnp.arange(n_tiles, dtype=jnp.int32)
    first = j % 2
    n_valid = jnp.sum(tile_valid)
    last_f = 1 - (n_valid - 1) % 2
    half = jnp.where(tile_valid[:, None] > 0, jnp.stack([first, 1 - first], axis=1), last_f)
    tile_f = half.reshape(-1).astype(jnp.int32)

    return pl.pallas_call(
        _expert_kernel,
        grid_spec=pltpu.PrefetchScalarGridSpec(
            num_scalar_prefetch=5,
            grid=(n_tiles, nf),
            in_specs=[
                pl.BlockSpec(memory_space=pl.ANY),
                pl.BlockSpec((1, d, TF_EXP), lambda j, s, te, tv, tf, *_: (te[j], 0, tf[2 * j + s])),
                pl.BlockSpec((1, d, TF_EXP), lambda j, s, te, tv, tf, *_: (te[j], 0, tf[2 * j + s])),
                pl.BlockSpec((1, TF_EXP, d), lambda j, s, te, tv, tf, *_: (te[j], tf[2 * j + s], 0)),
            ],
            out_specs=pl.BlockSpec(memory_space=pl.ANY),
            scratch_shapes=[pltpu.VMEM((TM_EXP * SUBLANES, LANES), F32),
                            pltpu.VMEM((TM_EXP, d), BF16),
                            pltpu.VMEM((TM_EXP, d), F32),
                            pltpu.VMEM((2, TM_EXP * SUBLANES, LANES), F32),
                            pltpu.SemaphoreType.DMA(()), pltpu.SemaphoreType.DMA(())],
        ),
        out_shape=jax.ShapeDtypeStruct(((TOP_K * n_tok + TM_EXP) * SUBLANES, LANES), F32),
        compiler_params=_params(2),
        name="moe_experts",
    )(tile_e, tile_valid, tile_f, src, dst, hf, wg, wu, wd)


def _combine_kernel(y0_ref, y1_ref, h_ref, gate_ref, gf_ref, o_ref):
    gates = gate_ref[...]
    n = h_ref.shape[0]
    mix = gates[:, 0:1] * _tiles_to_rows(y0_ref, n) + gates[:, 1:2] * _tiles_to_rows(y1_ref, n)
    o_ref[...] = h_ref[...] + gf_ref[0] * mix


def _combine_rows(y, h, gates, mods, seq):
    t, d = h.shape
    steps_per_batch = seq // TG
    second = t // TG
    tok = pl.BlockSpec((TG, d), lambda i: (i, 0))
    return pl.pallas_call(
        _combine_kernel,
        grid=(t // TG,),
        in_specs=[
            pl.BlockSpec((TG * SUBLANES, LANES), lambda i: (i, 0)),
            pl.BlockSpec((TG * SUBLANES, LANES), lambda i: (second + i, 0)),
            tok,
            pl.BlockSpec((TG, TOP_K), lambda i: (i, 0)),
            pl.BlockSpec((1, 1, d), lambda i: (COND_ROWS + i // steps_per_batch, 0, N_MODS - 1)),
        ],
        out_specs=tok,
        out_shape=jax.ShapeDtypeStruct((t, d), F32),
        compiler_params=_params(1),
        name="moe_combine",
    )(y, y, h, gates, mods)


def _rope_tables(seq):
    rows = seq // GRID_W
    row = jnp.repeat(jnp.arange(rows), GRID_W).astype(F32)
    col = jnp.tile(jnp.arange(GRID_W), rows).astype(F32)
    quarter = ATTN_DIM // 4
    inv = ROPE_BASE ** (-jnp.arange(quarter, dtype=F32) / quarter)
    ar = row[:, None] * inv
    ac = col[:, None] * inv
    ang = jnp.concatenate([ar, ar, ac, ac], axis=-1)
    sign = jnp.tile(jnp.concatenate([-jnp.ones(quarter, F32), jnp.ones(quarter, F32)]), 2)
    reps = QK_W // ATTN_DIM
    return jnp.tile(jnp.cos(ang), (1, reps)), jnp.tile(jnp.sin(ang) * sign, (1, reps))


def kernel(x, c, ctx, c_ctx, ada_w, ada_b, norm_mix_g, norm_ffn_g, w_in_even, q_norm_g, k_norm_g,
           lam_q1, lam_k1, lam_q2, lam_k2, subln_g, conv_w, w_out_even, ffn_w_gate, ffn_w_up,
           ffn_w_down, sgu_w_in, sgu_ln_g, sgu_ln_b, sgu_w_s, sgu_b_s, sgu_w_out, router_w,
           moe_w_gate, moe_w_up, moe_w_down):
    b, s, d = x.shape
    assert d == D_MODEL and b < COND_ROWS and ada_w.shape[0] == 2
    assert s % TM_PROJ == 0 and s % TQ == 0 and s % TM_FFN == 0 and s % TM_SGU == 0 and s % TG == 0

    cond = jnp.zeros((COND_ROWS, d), F32).at[:b].set(c).at[b].set(c_ctx)
    mods = _ada_mods(cond, ada_w, ada_b)

    lambda_init = 0.8 - 0.6 * math.exp(-0.3 * 0)
    group = jnp.arange(QK_W) // ATTN_DIM
    bd = (group[:, None] == group[None, :]).astype(BF16)
    reps = QK_W // ATTN_DIM
    qg = jnp.tile(q_norm_g[0], reps)[None, :]
    kg = jnp.tile(k_norm_g[0], reps)[None, :]
    cos, sin = _rope_tables(s)
    w_in = w_in_even[0].astype(BF16)
    g_mix0 = norm_mix_g[0][None, :]
    q, k, v, yconv = _inproj(x, mods, g_mix0, w_in, bd, qg, kg, cos, sin, conv_w[0])
    kc, vc = _ctxproj(ctx, mods, g_mix0, w_in, bd, kg, ctx_row=b)
    attn = _attention(q, k, kc, v, vc, subln_g[0][None, :], lam_q1[0][None, :], lam_k1[0][None, :],
                      lam_q2[0][None, :], lam_k2[0][None, :], lambda_init)
    h = _outproj_ffn(x, attn, yconv, mods, norm_ffn_g[0][None, :], w_out_even[0].astype(BF16),
                     ffn_w_gate[0].astype(BF16), ffn_w_up[0].astype(BF16),
                     ffn_w_down[0].astype(BF16))

    gw = d // SGU_GROUPS
    bias = jnp.repeat(sgu_b_s[0].T, gw, axis=1)
    rw_hi = router_w[0].astype(BF16)
    rw_lo = (router_w[0] - rw_hi.astype(F32)).astype(BF16)
    h, hf, logits = _sgu(h, mods, norm_mix_g[1][None, :], norm_ffn_g[1][None, :],
                         sgu_w_in[0].astype(BF16), sgu_ln_g[0][None, :], sgu_ln_b[0][None, :],
                         sgu_w_s[0].astype(BF16), bias, sgu_w_out[0].astype(BF16), rw_hi, rw_lo)

    t = b * s
    n_tiles = (t * TOP_K) // TM_EXP + N_EXPERTS + 2
    pos, gates, tile_e, tile_valid = _route(logits.reshape(t, N_EXPERTS), TM_EXP, n_tiles)
    src, dst = _row_tables(pos, t, n_tiles)
    y = _expert_ffn(tile_e, tile_valid, src, dst, hf, moe_w_gate[0].astype(BF16),
                    moe_w_up[0].astype(BF16), moe_w_down[0].astype(BF16))
    out = _combine_rows(y, h.reshape(t, d), gates, mods, s)
    return out.reshape(b, s, d)
```

```python
import functools
import math

import jax
import jax.numpy as jnp
from jax import lax
from jax.experimental import pallas as pl
from jax.experimental.pallas import tpu as pltpu

F32 = jnp.float32
BF16 = jnp.bfloat16

D_MODEL = 1024
GRID_W = 64
ATTN_HEADS = 4
ATTN_DIM = 64
ATTN_VDIM = 2 * ATTN_DIM
QK_W = ATTN_HEADS * 2 * ATTN_DIM
CONV_K = 3
ROPE_BASE = 10000.0
SGU_GROUPS = 4
CHUNK = 128
N_EXPERTS = 8
TOP_K = 2
EPS = 1e-6
LOG2_E = math.log2(math.e)
N_MODS = 6
COND_ROWS = 16

V7X_VMEM_LIMIT = 56 * 1024 * 1024
MXU_N = 256
TM_PROJ = 512
HALO = 16
TQ = 2048
ATTN_RB = 256
ATTN_SLOTS = 8
ATTN_KC = 256
TM_FFN = 512
FFN_RB = 512
TM_SGU = 512
SGU_RB = 256
TM_EXP = 512
TF_EXP = 1792
TG = 256
ROW_DMA_PRIORITY = 1


def _params(n_axes):
    return pltpu.CompilerParams(
        dimension_semantics=("arbitrary",) * n_axes, vmem_limit_bytes=V7X_VMEM_LIMIT)


def _const_spec(shape):
    zeros = (0,) * len(shape)
    return pl.BlockSpec(shape, lambda *_: zeros, pipeline_mode=pl.Buffered(1))


def _dot(a, b):
    return jnp.dot(a, b, preferred_element_type=F32)


def _modulate(x, g, scale, shift):
    ms = jnp.mean(x * x, axis=-1, keepdims=True)
    y = x * lax.rsqrt(ms + EPS)
    return (y * g) * (1.0 + scale) + shift


def _split_bf16(x):
    hi = x.astype(BF16)
    lo = (x - hi.astype(F32)).astype(BF16)
    return hi, lo


assert D_MODEL == 8 * 128
SUBLANES = 8
LANES = 128


def _rows_to_tiles(ref, val, row0=0):
    n = val.shape[0]
    for c in range(SUBLANES):
        ref[pl.ds(row0 * SUBLANES + c, n, stride=SUBLANES), :] = val[:, c * LANES:(c + 1) * LANES]


def _tiles_to_rows(ref, n):
    return jnp.concatenate(
        [ref[pl.ds(c, n, stride=SUBLANES), :] for c in range(SUBLANES)], axis=1)


def _ada_kernel(cond_ref, w_ref, b_ref, o_ref):
    a = jax.nn.silu(cond_ref[...]).astype(BF16)
    o_ref[0] = _dot(a, w_ref[0].astype(BF16)) + b_ref[0]


def _ada_mods(cond, ada_w, ada_b):
    depth = ada_w.shape[0]
    d = D_MODEL
    out = pl.pallas_call(
        _ada_kernel,
        grid=(depth, N_MODS),
        in_specs=[
            pl.BlockSpec((COND_ROWS, d), lambda l, n: (0, 0)),
            pl.BlockSpec((1, d, d), lambda l, n: (l, 0, n)),
            pl.BlockSpec((1, 1, d), lambda l, n: (l, 0, n)),
        ],
        out_specs=pl.BlockSpec((1, COND_ROWS, d), lambda l, n: (l, 0, n)),
        out_shape=jax.ShapeDtypeStruct((depth, COND_ROWS, N_MODS * d), F32),
        compiler_params=_params(2),
        name="ada_mods",
    )(cond, ada_w, ada_b.reshape(depth, 1, N_MODS * d))
    return out.reshape(depth * COND_ROWS, 1, N_MODS * d)


def _mod_spec(layer, chunk, row_of):
    return pl.BlockSpec((1, 1, D_MODEL),
                        lambda *ids: (layer * COND_ROWS + row_of(*ids), 0, chunk))


def _group_rms(t, bd, gain):
    hi, lo = _split_bf16(t * t)
    ss = _dot(hi, bd) + _dot(lo, bd)
    return (t * lax.rsqrt(ss * (1.0 / ATTN_DIM) + EPS)) * gain


def _rope(t, cos, sin_signed, first_half):
    width = t.shape[-1]
    quarter = ATTN_DIM // 4
    ahead = pltpu.roll(t, width - quarter, axis=1)
    behind = pltpu.roll(t, quarter, axis=1)
    return t * cos + jnp.where(first_half, ahead, behind) * sin_signed


def _with_ones_columns(v):
    rows = v.shape[0]
    lane = lax.broadcasted_iota(jnp.int32, (rows, ATTN_VDIM), 1)
    ones_col = jnp.where(lane == 0, 1.0, 0.0).astype(v.dtype)
    parts = []
    for h in range(ATTN_HEADS):
        parts += [v[:, h * ATTN_VDIM:(h + 1) * ATTN_VDIM], ones_col]
    return jnp.concatenate(parts, axis=1)


def _side_cast(w, steps, index_of):
    rows, cols = w.shape
    assert rows % (steps * 16) == 0
    spec = pl.BlockSpec((rows // steps, cols), lambda *ids: (index_of(*ids), 0))
    return spec, jax.ShapeDtypeStruct(w.shape, BF16)


def _inproj_kernel(x_ref, xp_ref, xn_ref, sc_ref, sh_ref, g_ref, w_ref, bd_ref, qg_ref, kg_ref,
                   cos_ref, sin_ref, cw_ref, side_ref, q_ref, k_ref, v_ref, yc_ref, side_out_ref):
    side_out_ref[...] = side_ref[...].astype(BF16)
    i = pl.program_id(1)
    last = pl.num_programs(1) - 1
    tm = x_ref.shape[1]
    w = QK_W
    xe = jnp.concatenate([xp_ref[0], x_ref[0], xn_ref[0]], axis=0)
    he = _modulate(xe, g_ref[...], sc_ref[0], sh_ref[0]).astype(BF16)
    hm = he[HALO:HALO + tm]

    qkv = _dot(hm, w_ref[:, :3 * w])
    bd = bd_ref[...]
    cos = cos_ref[...]
    sin = sin_ref[...]
    lane = lax.broadcasted_iota(jnp.int32, (1, w), 1)
    first_half = (lane % (ATTN_DIM // 2)) < (ATTN_DIM // 4)
    q = _rope(_group_rms(qkv[:, :w], bd, qg_ref[...]), cos, sin, first_half)
    k = _rope(_group_rms(qkv[:, w:2 * w], bd, kg_ref[...]), cos, sin, first_half)
    q_ref[0] = (q * (ATTN_DIM ** -0.5 * LOG2_E)).astype(BF16)
    k_ref[0] = k.astype(BF16)
    v_ref[0] = _with_ones_columns(qkv[:, 2 * w:].astype(BF16))

    gcu = _dot(he, w_ref[:, 4 * w:])
    cu = gcu[:, :w] * gcu[:, w:]
    row = lax.broadcasted_iota(jnp.int32, (tm + 2 * HALO, 1), 0)
    first_row = jnp.where(i == 0, HALO, 0)
    end_row = jnp.where(i == last, HALO + tm, tm + 2 * HALO)
    cu = jnp.where((row >= first_row) & (row < end_row), cu, 0.0)
    prev = pltpu.roll(cu, 1, axis=0)[HALO:HALO + tm]
    nxt = pltpu.roll(cu, tm + 2 * HALO - 1, axis=0)[HALO:HALO + tm]
    cw = cw_ref[...]
    conv = prev * cw[0:1] + cu[HALO:HALO + tm] * cw[1:2] + nxt * cw[2:3]
    gate_b = _dot(hm, w_ref[:, 3 * w:4 * w])
    yc_ref[0] = (gate_b * conv).astype(BF16)


def _inproj(x, mods, g, w_in, bd, qg, kg, cos, sin, conv_w, side_w):
    b, s, d = x.shape
    tm = TM_PROJ
    nh = tm // HALO
    n_halo_blocks = s // HALO
    tok = pl.BlockSpec((1, tm, QK_W), lambda bi, i: (bi, i, 0))
    out_sds = jax.ShapeDtypeStruct((b, s, QK_W), BF16)
    side_spec, side_sds = _side_cast(side_w, b * (s // tm), lambda bi, i: bi * (s // tm) + i)
    return pl.pallas_call(
        _inproj_kernel,
        grid=(b, s // tm),
        in_specs=[
            pl.BlockSpec((1, tm, d), lambda bi, i: (bi, i, 0)),
            pl.BlockSpec((1, HALO, d), lambda bi, i: (bi, jnp.maximum(i * nh - 1, 0), 0)),
            pl.BlockSpec((1, HALO, d),
                         lambda bi, i: (bi, jnp.minimum((i + 1) * nh, n_halo_blocks - 1), 0)),
            _mod_spec(0, 1, lambda bi, i: bi),
            _mod_spec(0, 0, lambda bi, i: bi),
            _const_spec((1, d)),
            _const_spec(w_in.shape),
            _const_spec(bd.shape),
            _const_spec((1, QK_W)),
            _const_spec((1, QK_W)),
            pl.BlockSpec((tm, QK_W), lambda bi, i: (i, 0)),
            pl.BlockSpec((tm, QK_W), lambda bi, i: (i, 0)),
            _const_spec(conv_w.shape),
            side_spec,
        ],
        out_specs=[tok, tok, pl.BlockSpec((1, tm, 2 * QK_W), lambda bi, i: (bi, i, 0)), tok,
                   side_spec],
        out_shape=[out_sds, out_sds, jax.ShapeDtypeStruct((b, s, 2 * QK_W), BF16), out_sds,
                   side_sds],
        compiler_params=_params(2),
        name="inproj",
    )(x, x, x, mods, mods, g, w_in, bd, qg, kg, cos, sin, conv_w, side_w)


def _ctxproj_kernel(x_ref, sc_ref, sh_ref, g_ref, wk_ref, wv_ref, bd_ref, kg_ref, k_ref, v_ref):
    hc = _modulate(x_ref[0], g_ref[...], sc_ref[0], sh_ref[0]).astype(BF16)
    k_ref[0] = _group_rms(_dot(hc, wk_ref[...]), bd_ref[...], kg_ref[...]).astype(BF16)
    v_ref[0] = _with_ones_columns(_dot(hc, wv_ref[...]).astype(BF16))


def _ctxproj(ctx, mods, g, w_in, bd, kg, ctx_row):
    b, l, d = ctx.shape
    out = pl.BlockSpec((1, l, QK_W), lambda bi: (bi, 0, 0))
    out_sds = jax.ShapeDtypeStruct((b, l, QK_W), BF16)
    return pl.pallas_call(
        _ctxproj_kernel,
        grid=(b,),
        in_specs=[
            pl.BlockSpec((1, l, d), lambda bi: (bi, 0, 0)),
            _mod_spec(0, 1, lambda bi: ctx_row),
            _mod_spec(0, 0, lambda bi: ctx_row),
            _const_spec((1, d)),
            pl.BlockSpec((d, QK_W), lambda bi: (0, 1), pipeline_mode=pl.Buffered(1)),
            pl.BlockSpec((d, QK_W), lambda bi: (0, 2), pipeline_mode=pl.Buffered(1)),
            _const_spec(bd.shape),
            _const_spec((1, QK_W)),
        ],
        out_specs=[out, pl.BlockSpec((1, l, 2 * QK_W), lambda bi: (bi, 0, 0))],
        out_shape=[out_sds, jax.ShapeDtypeStruct((b, l, 2 * QK_W), BF16)],
        compiler_params=_params(1),
        name="ctxproj",
    )(ctx, mods, mods, g, w_in, w_in, bd, kg)


def _attn_kernel(q_ref, k_ref, kc_ref, v_ref, vc_ref, sg_ref, lq1_ref, lk1_ref, lq2_ref, lk2_ref,
                 o_ref, s_ref, e_ref, *, lambda_init):
    tq = q_ref.shape[1]
    n_ctx = kc_ref.shape[1] // ATTN_KC
    n_lat = k_ref.shape[1] // ATTN_KC
    chunks = [(kc_ref, vc_ref, c) for c in range(n_ctx)] + [(k_ref, v_ref, c) for c in range(n_lat)]
    lane = lax.broadcasted_iota(jnp.int32, (1, ATTN_VDIM), 1)
    first = lane < ATTN_DIM
    lam = (jnp.exp(jnp.sum(lq1_ref[...] * lk1_ref[...], keepdims=True))
           - jnp.exp(jnp.sum(lq2_ref[...] * lk2_ref[...], keepdims=True)) + lambda_init)
    nt = (((1,), (1,)), ((), ()))

    for rb in range(tq // ATTN_RB):
        rows = slice(rb * ATTN_RB, (rb + 1) * ATTN_RB)
        q = q_ref[0, rows, :]
        zero = jnp.zeros_like(q)
        outs = []
        for part in range(2):
            qx = jnp.where(first, q, zero) if part == 0 else jnp.where(first, zero, q)
            slot = (rb * 2 + part) % ATTN_SLOTS
            mpart = None
            for ci, (kr, _, c) in enumerate(chunks):
                sc = lax.dot_general(qx, kr[0, c * ATTN_KC:(c + 1) * ATTN_KC, :], nt,
                                     preferred_element_type=F32)
                s_ref[slot, ci] = sc
                cm = sc[:, :ATTN_VDIM]
                for j in range(1, ATTN_KC // ATTN_VDIM):
                    cm = jnp.maximum(cm, sc[:, j * ATTN_VDIM:(j + 1) * ATTN_VDIM])
                mpart = cm if mpart is None else jnp.maximum(mpart, cm)
            m = jnp.max(mpart, axis=-1, keepdims=True)
            for ci in range(len(chunks)):
                e_ref[slot, :, ci * ATTN_KC:(ci + 1) * ATTN_KC] = (
                    jnp.exp2(s_ref[slot, ci] - m).astype(BF16))
            n_c = n_ctx * ATTN_KC
            acc = _dot(e_ref[slot, :, :n_c], vc_ref[0]) + _dot(e_ref[slot, :, n_c:], v_ref[0])
            outs.append(acc[:, :ATTN_VDIM] * (1.0 / acc[:, ATTN_VDIM:ATTN_VDIM + 1]))
        o = outs[0] - lam * outs[1]
        ms = jnp.mean(o * o, axis=-1, keepdims=True)
        y = (o * lax.rsqrt(ms + EPS)) * sg_ref[...]
        o_ref[0, rows, :] = (y * (1.0 - lambda_init)).astype(BF16)


def _attention(q, k, kc, v, vc, subln_g, lq1, lk1, lq2, lk2, lambda_init):
    b, s, _ = q.shape
    l = kc.shape[1]
    hd = ATTN_VDIM
    va = 2 * ATTN_VDIM
    assert l % ATTN_KC == 0 and s % ATTN_KC == 0
    n_chunks = (l + s) // ATTN_KC
    vec = _const_spec((1, ATTN_DIM))
    return pl.pallas_call(
        functools.partial(_attn_kernel, lambda_init=lambda_init),
        grid=(b, ATTN_HEADS, s // TQ),
        in_specs=[
            pl.BlockSpec((1, TQ, hd), lambda bi, h, i: (bi, i, h)),
            pl.BlockSpec((1, s, hd), lambda bi, h, i: (bi, 0, h)),
            pl.BlockSpec((1, l, hd), lambda bi, h, i: (bi, 0, h)),
            pl.BlockSpec((1, s, va), lambda bi, h, i: (bi, 0, h)),
            pl.BlockSpec((1, l, va), lambda bi, h, i: (bi, 0, h)),
            _const_spec((1, hd)),
            vec, vec, vec, vec,
        ],
        out_specs=pl.BlockSpec((1, TQ, hd), lambda bi, h, i: (bi, i, h)),
        out_shape=jax.ShapeDtypeStruct((b, s, ATTN_HEADS * hd), BF16),
        scratch_shapes=[pltpu.VMEM((ATTN_SLOTS, n_chunks, ATTN_RB, ATTN_KC), F32),
                        pltpu.VMEM((ATTN_SLOTS, ATTN_RB, l + s), BF16)],
        compiler_params=_params(3),
        name="diff_attention",
    )(q, k, kc, v, vc, subln_g, lq1, lk1, lq2, lk2)


def _outproj_ffn_kernel(x_ref, at_ref, yc_ref, gm_ref, sf_ref, cf_ref, gf_ref, g_ref,
                        wo_ref, wg_ref, wu_ref, wd_ref, side_ref, o_ref, side_out_ref):
    side_out_ref[...] = side_ref[...].astype(BF16)
    half = wo_ref.shape[0] // 2
    d_ff = wg_ref.shape[1]
    for rb in range(x_ref.shape[1] // FFN_RB):
        rows = slice(rb * FFN_RB, (rb + 1) * FFN_RB)
        y = _dot(at_ref[0, rows, :], wo_ref[:half]) + _dot(yc_ref[0, rows, :], wo_ref[half:])
        h1 = x_ref[0, rows, :] + gm_ref[0] * y
        hf = _modulate(h1, g_ref[...], cf_ref[0], sf_ref[0]).astype(BF16)
        acc = jnp.zeros(h1.shape, F32)
        for f in range(d_ff // MXU_N):
            cols = slice(f * MXU_N, (f + 1) * MXU_N)
            a = jax.nn.silu(_dot(hf, wg_ref[:, cols])) * _dot(hf, wu_ref[:, cols])
            acc = acc + _dot(a.astype(BF16), wd_ref[cols, :])
        o_ref[0, rows, :] = h1 + gf_ref[0] * acc


def _outproj_ffn(x, attn, yconv, mods, g, w_out, wg, wu, wd, side_w):
    b, s, d = x.shape
    tm = TM_FFN
    side_spec, side_sds = _side_cast(side_w, b * (s // tm), lambda bi, i: bi * (s // tm) + i)
    return pl.pallas_call(
        _outproj_ffn_kernel,
        grid=(b, s // tm),
        in_specs=[
            pl.BlockSpec((1, tm, d), lambda bi, i: (bi, i, 0)),
            pl.BlockSpec((1, tm, QK_W), lambda bi, i: (bi, i, 0)),
            pl.BlockSpec((1, tm, QK_W), lambda bi, i: (bi, i, 0)),
            _mod_spec(0, 2, lambda bi, i: bi),
            _mod_spec(0, 3, lambda bi, i: bi),
            _mod_spec(0, 4, lambda bi, i: bi),
            _mod_spec(0, 5, lambda bi, i: bi),
            _const_spec((1, d)),
            _const_spec(w_out.shape),
            _const_spec(wg.shape),
            _const_spec(wu.shape),
            _const_spec(wd.shape),
            side_spec,
        ],
        out_specs=[pl.BlockSpec((1, tm, d), lambda bi, i: (bi, i, 0)), side_spec],
        out_shape=[jax.ShapeDtypeStruct((b, s, d), F32), side_sds],
        compiler_params=_params(2),
        name="outproj_ffn",
    )(x, attn, yconv, mods, mods, mods, mods, g, w_out, wg, wu, wd, side_w)


def _sgu_kernel(x_ref, sm_ref, cm_ref, gm_ref, sf_ref, cf_ref, g_ref, gff_ref, win_ref, lng_ref,
                lnb_ref, ws_ref, bs_ref, wo_ref, rwh_ref, rwl_ref, side_ref,
                h_ref, hf_ref, lg_ref, side_out_ref, us_ref):
    side_out_ref[...] = side_ref[...].astype(BF16)
    tm, d = x_ref.shape[1:]
    gw = d // SGU_GROUPS
    for rb in range(tm // SGU_RB):
        r0 = rb * SGU_RB
        x = x_ref[0, r0:r0 + SGU_RB, :]
        hm = _modulate(x, g_ref[...], cm_ref[0], sm_ref[0]).astype(BF16)
        z = jax.nn.gelu(_dot(hm, win_ref[...]))
        v = z[:, d:]
        mu = jnp.mean(v, axis=-1, keepdims=True)
        vc = v - mu
        var = jnp.mean(vc * vc, axis=-1, keepdims=True)
        vn = ((vc * lax.rsqrt(var + EPS)) * lng_ref[...] + lnb_ref[...]).astype(BF16)
        for n in range(SGU_RB // CHUNK):
            rows = slice(n * CHUNK, (n + 1) * CHUNK)
            for gi in range(SGU_GROUPS):
                cols = slice(gi * gw, (gi + 1) * gw)
                s = _dot(ws_ref[gi], vn[rows, cols]) + bs_ref[:, cols]
                us_ref[r0 + n * CHUNK:r0 + (n + 1) * CHUNK, cols] = (z[rows, cols] * s).astype(BF16)
        h = x + gm_ref[0] * _dot(us_ref[r0:r0 + SGU_RB, :], wo_ref[...])
        h_ref[0, r0:r0 + SGU_RB, :] = h
        hf = _modulate(h, gff_ref[...], cf_ref[0], sf_ref[0])
        _rows_to_tiles(hf_ref, hf, row0=r0)
        hi, lo = _split_bf16(hf)
        lg_ref[0, r0:r0 + SGU_RB, :] = (
            _dot(hi, rwh_ref[...]) + (_dot(hi, rwl_ref[...]) + _dot(lo, rwh_ref[...])))


def _sgu(h, mods, g_mix, g_ffn, w_in, ln_g, ln_b, w_s, bias, w_out, rw_hi, rw_lo, side_w):
    b, s, d = h.shape
    tm = TM_SGU
    tok = pl.BlockSpec((1, tm, d), lambda bi, i: (bi, i, 0))
    side_spec, side_sds = _side_cast(side_w, b * (s // tm), lambda bi, i: bi * (s // tm) + i)
    return pl.pallas_call(
        _sgu_kernel,
        grid=(b, s // tm),
        in_specs=[
            tok,
            _mod_spec(1, 0, lambda bi, i: bi),
            _mod_spec(1, 1, lambda bi, i: bi),
            _mod_spec(1, 2, lambda bi, i: bi),
            _mod_spec(1, 3, lambda bi, i: bi),
            _mod_spec(1, 4, lambda bi, i: bi),
            _const_spec((1, d)),
            _const_spec((1, d)),
            _const_spec(w_in.shape),
            _const_spec((1, d)),
            _const_spec((1, d)),
            _const_spec(w_s.shape),
            _const_spec(bias.shape),
            _const_spec(w_out.shape),
            _const_spec(rw_hi.shape),
            _const_spec(rw_lo.shape),
            side_spec,
        ],
        out_specs=[tok,
                   pl.BlockSpec((tm * SUBLANES, LANES), lambda bi, i: (bi * (s // tm) + i, 0)),
                   pl.BlockSpec((1, tm, N_EXPERTS), lambda bi, i: (bi, i, 0)),
                   side_spec],
        out_shape=[jax.ShapeDtypeStruct((b, s, d), F32),
                   jax.ShapeDtypeStruct((b * s * SUBLANES, LANES), F32),
                   jax.ShapeDtypeStruct((b, s, N_EXPERTS), F32),
                   side_sds],
        scratch_shapes=[pltpu.VMEM((tm, d), BF16)],
        compiler_params=_params(2),
        name="sgu",
    )(h, mods, mods, mods, mods, mods, g_mix, g_ffn, w_in, ln_g, ln_b, w_s, bias, w_out,
      rw_hi, rw_lo, side_w)


def _route(logits, tile_rows, n_tiles):
    t = logits.shape[0]
    eid = lax.broadcasted_iota(jnp.int32, (t, N_EXPERTS), 1)
    i1 = jnp.argmax(logits, axis=-1).astype(jnp.int32)
    v1 = jnp.max(logits, axis=-1)
    rest = jnp.where(eid == i1[:, None], -jnp.inf, logits)
    i2 = jnp.argmax(rest, axis=-1).astype(jnp.int32)
    v2 = jnp.max(rest, axis=-1)
    e2 = jnp.exp(v2 - v1)
    gates = jnp.stack([1.0 / (1.0 + e2), e2 / (1.0 + e2)], axis=-1)
    onehot = ((eid == i1[:, None]) | (eid == i2[:, None])).astype(jnp.int32)
    csum = jnp.cumsum(onehot, axis=0)
    rank = csum - onehot
    counts = csum[-1]
    padded = ((counts + tile_rows - 1) // tile_rows) * tile_rows
    ends = jnp.cumsum(padded)
    starts = ends - padded
    idx = jnp.stack([i1, i2], axis=-1)
    pos = (starts[idx] + jnp.take_along_axis(rank, idx, axis=-1)).astype(jnp.int32)
    tile_start = jnp.arange(n_tiles, dtype=jnp.int32) * tile_rows
    valid = tile_start < ends[-1]
    tile_e = jnp.sum(tile_start[:, None] >= ends[None, :], axis=-1).astype(jnp.int32)
    last_e = jnp.max(jnp.where(counts > 0, jnp.arange(N_EXPERTS, dtype=jnp.int32), 0))
    tile_e = jnp.where(valid, tile_e, last_e)
    total = jnp.full((1,), n_tiles * tile_rows, jnp.int32)
    pad_lo = jnp.concatenate([starts + counts, ends[-1:]]).astype(jnp.int32)
    pad_hi = jnp.concatenate([ends, total]).astype(jnp.int32)
    return pos.reshape(-1), gates, tile_e, valid.astype(jnp.int32), pad_lo, pad_hi


INVERT_UNROLL = 32


def _invert_kernel(pos_ref, lo_ref, hi_ref, pair_ref):
    n_pairs = pos_ref.shape[0]
    assert n_pairs % INVERT_UNROLL == 0

    def fill(i, c):
        pair_ref[i] = -1
        return c

    def place(blk, c):
        for u in range(INVERT_UNROLL):
            i = blk * INVERT_UNROLL + u
            pair_ref[pos_ref[i]] = i
        return c

    for g in range(lo_ref.shape[0]):
        lax.fori_loop(lo_ref[g], hi_ref[g], fill, 0)
    lax.fori_loop(0, n_pairs // INVERT_UNROLL, place, 0)


def _row_tables(pos, pad_lo, pad_hi, n_tokens, n_tiles):
    n_rows = n_tiles * TM_EXP
    smem = pl.BlockSpec(memory_space=pltpu.SMEM)
    pair = pl.pallas_call(
        _invert_kernel,
        in_specs=[smem, smem, smem],
        out_specs=smem,
        out_shape=jax.ShapeDtypeStruct((n_rows,), jnp.int32),
        name="moe_invert",
    )(pos, pad_lo, pad_hi)
    tok = pair >> 1
    scratch_row = n_tokens * TOP_K + jnp.arange(n_rows, dtype=jnp.int32) % TM_EXP
    src = jnp.where(pair < 0, 0, tok)
    dst = jnp.where(pair < 0, scratch_row, (pair & 1) * n_tokens + tok)
    return src * SUBLANES, jnp.concatenate([scratch_row[:TM_EXP], dst]) * SUBLANES


def _expert_kernel(te_ref, tv_ref, tf_ref, src_ref, dst_ref, hf_ref, wg_ref, wu_ref, wd_ref, y_ref,
                   x_ref, xb_ref, acc_ref, yt_ref, sem_in, sem_out):
    del te_ref, tf_ref
    tm = TM_EXP
    j = pl.program_id(0)
    step = pl.program_id(1)
    valid = tv_ref[j] > 0
    prev_valid = jnp.logical_and(j > 0, tv_ref[jnp.maximum(j - 1, 0)] > 0)
    prev2_valid = jnp.logical_and(j > 1, tv_ref[jnp.maximum(j - 2, 0)] > 0)
    slot = j % 2
    x_rows = lambda r: x_ref.at[pl.ds(r * SUBLANES, SUBLANES)]
    yt_rows = lambda sl, r: yt_ref.at[sl, pl.ds(r * SUBLANES, SUBLANES)]

    def start_gathers(tile):
        for r in range(tm):
            t8 = src_ref[tile * tm + r]
            pltpu.make_async_copy(hf_ref.at[pl.ds(pl.multiple_of(t8, SUBLANES), SUBLANES)],
                                  x_rows(r), sem_in).start(priority=ROW_DMA_PRIORITY)

    def gather_wait(r):
        return pltpu.make_async_copy(hf_ref.at[pl.ds(0, SUBLANES)], x_rows(r), sem_in)

    def start_scatters(tile, sl):
        for r in range(tm):
            d8 = dst_ref[(tile + 1) * tm + r]
            pltpu.make_async_copy(yt_rows(sl, r),
                                  y_ref.at[pl.ds(pl.multiple_of(d8, SUBLANES), SUBLANES)],
                                  sem_out).start(priority=ROW_DMA_PRIORITY)

    def scatter_wait(sl, r):
        return pltpu.make_async_copy(yt_rows(sl, r), y_ref.at[pl.ds(0, SUBLANES)], sem_out)

    def ffn_step():
        xb = xb_ref[...]
        a = jax.nn.silu(_dot(xb, wg_ref[0])) * _dot(xb, wu_ref[0])
        return _dot(a.astype(BF16), wd_ref[0])

    @pl.when(jnp.logical_and(j == 0, step == 0))
    def _():
        yt_ref[...] = jnp.zeros_like(yt_ref)
        start_gathers(0)

    @pl.when(jnp.logical_and(step == 0, jnp.logical_or(j == 0, prev_valid)))
    def _():
        for r in range(tm):
            gather_wait(r).wait()

    @pl.when(jnp.logical_and(valid, step == 0))
    def _():
        xb_ref[...] = _tiles_to_rows(x_ref, tm).astype(BF16)
        start_gathers(j + 1)
        acc_ref[...] = ffn_step()

    @pl.when(jnp.logical_and(step == 1, jnp.logical_or(j == 1, prev2_valid)))
    def _():
        for r in range(tm):
            scatter_wait(slot, r).wait()

    @pl.when(jnp.logical_and(valid, step == 1))
    def _():
        start_scatters(j - 1, 1 - slot)
        _rows_to_tiles(yt_ref.at[slot], acc_ref[...] + ffn_step())

    @pl.when(jnp.logical_and(jnp.logical_and(jnp.logical_not(valid), prev_valid), step == 1))
    def _():
        start_scatters(j - 1, 1 - slot)


def _expert_ffn(tile_e, tile_valid, src, dst, hf, wg, wu, wd):
    n_tok = hf.shape[0] // SUBLANES
    d = D_MODEL
    n_tiles = tile_e.shape[0]
    d_ff = wg.shape[2]
    nf = d_ff // TF_EXP
    assert nf == 2
    j = jnp.arange(n_tiles, dtype=jnp.int32)
    first = j % 2
    n_valid = jnp.sum(tile_valid)
    last_f = 1 - (n_valid - 1) % 2
    half = jnp.where(tile_valid[:, None] > 0, jnp.stack([first, 1 - first], axis=1), last_f)
    tile_f = half.reshape(-1).astype(jnp.int32)

    return pl.pallas_call(
        _expert_kernel,
        grid_spec=pltpu.PrefetchScalarGridSpec(
            num_scalar_prefetch=5,
            grid=(n_tiles, nf),
            in_specs=[
                pl.BlockSpec(memory_space=pl.ANY),
                pl.BlockSpec((1, d, TF_EXP), lambda j, s, te, tv, tf, *_: (te[j], 0, tf[2 * j + s])),
                pl.BlockSpec((1, d, TF_EXP), lambda j, s, te, tv, tf, *_: (te[j], 0, tf[2 * j + s])),
                pl.BlockSpec((1, TF_EXP, d), lambda j, s, te, tv, tf, *_: (te[j], tf[2 * j + s], 0)),
            ],
            out_specs=pl.BlockSpec(memory_space=pl.ANY),
            scratch_shapes=[pltpu.VMEM((TM_EXP * SUBLANES, LANES), F32),
                            pltpu.VMEM((TM_EXP, d), BF16),
                            pltpu.VMEM((TM_EXP, d), F32),
                            pltpu.VMEM((2, TM_EXP * SUBLANES, LANES), F32),
                            pltpu.SemaphoreType.DMA(()), pltpu.SemaphoreType.DMA(())],
        ),
        out_shape=jax.ShapeDtypeStruct(((TOP_K * n_tok + TM_EXP) * SUBLANES, LANES), F32),
        compiler_params=_params(2),
        name="moe_experts",
    )(tile_e, tile_valid, tile_f, src, dst, hf, wg, wu, wd)


def _combine_kernel(y0_ref, y1_ref, h_ref, gate_ref, gf_ref, o_ref):
    gates = gate_ref[...]
    n = h_ref.shape[0]
    mix = gates[:, 0:1] * _tiles_to_rows(y0_ref, n) + gates[:, 1:2] * _tiles_to_rows(y1_ref, n)
    o_ref[...] = h_ref[...] + gf_ref[0] * mix


def _combine_rows(y, h, gates, mods, seq):
    t, d = h.shape
    steps_per_batch = seq // TG
    second = t // TG
    tok = pl.BlockSpec((TG, d), lambda i: (i, 0))
    return pl.pallas_call(
        _combine_kernel,
        grid=(t // TG,),
        in_specs=[
            pl.BlockSpec((TG * SUBLANES, LANES), lambda i: (i, 0)),
            pl.BlockSpec((TG * SUBLANES, LANES), lambda i: (second + i, 0)),
            tok,
            pl.BlockSpec((TG, TOP_K), lambda i: (i, 0)),
            pl.BlockSpec((1, 1, d), lambda i: (COND_ROWS + i // steps_per_batch, 0, N_MODS - 1)),
        ],
        out_specs=tok,
        out_shape=jax.ShapeDtypeStruct((t, d), F32),
        compiler_params=_params(1),
        name="moe_combine",
    )(y, y, h, gates, mods)


def _rope_tables(seq):
    rows = seq // GRID_W
    row = jnp.repeat(jnp.arange(rows), GRID_W).astype(F32)
    col = jnp.tile(jnp.arange(GRID_W), rows).astype(F32)
    quarter = ATTN_DIM // 4
    inv = ROPE_BASE ** (-jnp.arange(quarter, dtype=F32) / quarter)
    ar = row[:, None] * inv
    ac = col[:, None] * inv
    ang = jnp.concatenate([ar, ar, ac, ac], axis=-1)
    sign = jnp.tile(jnp.concatenate([-jnp.ones(quarter, F32), jnp.ones(quarter, F32)]), 2)
    reps = QK_W // ATTN_DIM
    return jnp.tile(jnp.cos(ang), (1, reps)), jnp.tile(jnp.sin(ang) * sign, (1, reps))


def kernel(x, c, ctx, c_ctx, ada_w, ada_b, norm_mix_g, norm_ffn_g, w_in_even, q_norm_g, k_norm_g,
           lam_q1, lam_k1, lam_q2, lam_k2, subln_g, conv_w, w_out_even, ffn_w_gate, ffn_w_up,
           ffn_w_down, sgu_w_in, sgu_ln_g, sgu_ln_b, sgu_w_s, sgu_b_s, sgu_w_out, router_w,
           moe_w_gate, moe_w_up, moe_w_down):
    b, s, d = x.shape
    assert d == D_MODEL and b < COND_ROWS and ada_w.shape[0] == 2
    assert s % TM_PROJ == 0 and s % TQ == 0 and s % TM_FFN == 0 and s % TM_SGU == 0 and s % TG == 0

    cond = jnp.zeros((COND_ROWS, d), F32).at[:b].set(c).at[b].set(c_ctx)
    mods = _ada_mods(cond, ada_w, ada_b)

    lambda_init = 0.8 - 0.6 * math.exp(-0.3 * 0)
    group = jnp.arange(QK_W) // ATTN_DIM
    bd = (group[:, None] == group[None, :]).astype(BF16)
    reps = QK_W // ATTN_DIM
    qg = jnp.tile(q_norm_g[0], reps)[None, :]
    kg = jnp.tile(k_norm_g[0], reps)[None, :]
    cos, sin = _rope_tables(s)
    w_in = w_in_even[0].astype(BF16)
    g_mix0 = norm_mix_g[0][None, :]
    n_e, _, d_ffe = moe_w_gate[0].shape
    q, k, v, yconv, moe_wg = _inproj(x, mods, g_mix0, w_in, bd, qg, kg, cos, sin, conv_w[0],
                                     moe_w_gate[0].reshape(n_e * d, d_ffe))
    kc, vc = _ctxproj(ctx, mods, g_mix0, w_in, bd, kg, ctx_row=b)
    attn = _attention(q, k, kc, v, vc, subln_g[0][None, :], lam_q1[0][None, :], lam_k1[0][None, :],
                      lam_q2[0][None, :], lam_k2[0][None, :], lambda_init)
    h, moe_wd = _outproj_ffn(x, attn, yconv, mods, norm_ffn_g[0][None, :],
                             w_out_even[0].astype(BF16), ffn_w_gate[0].astype(BF16),
                             ffn_w_up[0].astype(BF16), ffn_w_down[0].astype(BF16),
                             moe_w_down[0].reshape(n_e * d_ffe, d))

    gw = d // SGU_GROUPS
    bias = jnp.repeat(sgu_b_s[0].T, gw, axis=1)
    rw_hi = router_w[0].astype(BF16)
    rw_lo = (router_w[0] - rw_hi.astype(F32)).astype(BF16)
    h, hf, logits, moe_wu = _sgu(h, mods, norm_mix_g[1][None, :], norm_ffn_g[1][None, :],
                                 sgu_w_in[0].astype(BF16), sgu_ln_g[0][None, :],
                                 sgu_ln_b[0][None, :], sgu_w_s[0].astype(BF16), bias,
                                 sgu_w_out[0].astype(BF16), rw_hi, rw_lo,
                                 moe_w_up[0].reshape(n_e * d, d_ffe))

    t = b * s
    n_tiles = (t * TOP_K) // TM_EXP + N_EXPERTS + 2
    pos, gates, tile_e, tile_valid, pad_lo, pad_hi = _route(
        logits.reshape(t, N_EXPERTS), TM_EXP, n_tiles)
    src, dst = _row_tables(pos, pad_lo, pad_hi, t, n_tiles)
    y = _expert_ffn(tile_e, tile_valid, src, dst, hf, moe_wg.reshape(n_e, d, d_ffe),
                    moe_wu.reshape(n_e, d, d_ffe), moe_wd.reshape(n_e, d_ffe, d))
    out = _combine_rows(y, h.reshape(t, d), gates, mods, s)
    return out.reshape(b, s, d)
```

```python
import functools
import math

import jax
import jax.numpy as jnp
from jax import lax
from jax.experimental import pallas as pl
from jax.experimental.pallas import tpu as pltpu

F32 = jnp.float32
BF16 = jnp.bfloat16

D_MODEL = 1024
GRID_W = 64
ATTN_HEADS = 4
ATTN_DIM = 64
ATTN_VDIM = 2 * ATTN_DIM
QK_W = ATTN_HEADS * 2 * ATTN_DIM
CONV_K = 3
ROPE_BASE = 10000.0
SGU_GROUPS = 4
CHUNK = 128
N_EXPERTS = 8
TOP_K = 2
EPS = 1e-6
LOG2_E = math.log2(math.e)
N_MODS = 6
COND_ROWS = 16

V7X_VMEM_LIMIT = 56 * 1024 * 1024
MXU_N = 256
TM_PROJ = 512
HALO = 16
TQ = 2048
ATTN_RB = 256
ATTN_SLOTS = 8
ATTN_KC = 256
TM_FFN = 512
FFN_RB = 512
TM_SGU = 512
SGU_RB = 256
TM_EXP = 512
TF_EXP = 1792
TG = 256
ROW_DMA_PRIORITY = 1


def _params(n_axes):
    return pltpu.CompilerParams(
        dimension_semantics=("arbitrary",) * n_axes, vmem_limit_bytes=V7X_VMEM_LIMIT)


def _const_spec(shape):
    zeros = (0,) * len(shape)
    return pl.BlockSpec(shape, lambda *_: zeros, pipeline_mode=pl.Buffered(1))


def _dot(a, b):
    return jnp.dot(a, b, preferred_element_type=F32)


def _modulate(x, g, scale, shift):
    ms = jnp.mean(x * x, axis=-1, keepdims=True)
    y = x * lax.rsqrt(ms + EPS)
    return (y * g) * (1.0 + scale) + shift


def _split_bf16(x):
    hi = x.astype(BF16)
    lo = (x - hi.astype(F32)).astype(BF16)
    return hi, lo


assert D_MODEL == 8 * 128
SUBLANES = 8
LANES = 128


def _rows_to_tiles(ref, val, row0=0):
    n = val.shape[0]
    for c in range(SUBLANES):
        ref[pl.ds(row0 * SUBLANES + c, n, stride=SUBLANES), :] = val[:, c * LANES:(c + 1) * LANES]


def _tiles_to_rows(ref, n):
    return jnp.concatenate(
        [ref[pl.ds(c, n, stride=SUBLANES), :] for c in range(SUBLANES)], axis=1)


def _ada_kernel(cond_ref, w_ref, b_ref, o_ref):
    a = jax.nn.silu(cond_ref[...]).astype(BF16)
    o_ref[0] = _dot(a, w_ref[0].astype(BF16)) + b_ref[0]


def _ada_mods(cond, ada_w, ada_b):
    depth = ada_w.shape[0]
    d = D_MODEL
    out = pl.pallas_call(
        _ada_kernel,
        grid=(depth, N_MODS),
        in_specs=[
            pl.BlockSpec((COND_ROWS, d), lambda l, n: (0, 0)),
            pl.BlockSpec((1, d, d), lambda l, n: (l, 0, n)),
            pl.BlockSpec((1, 1, d), lambda l, n: (l, 0, n)),
        ],
        out_specs=pl.BlockSpec((1, COND_ROWS, d), lambda l, n: (l, 0, n)),
        out_shape=jax.ShapeDtypeStruct((depth, COND_ROWS, N_MODS * d), F32),
        compiler_params=_params(2),
        name="ada_mods",
    )(cond, ada_w, ada_b.reshape(depth, 1, N_MODS * d))
    return out.reshape(depth * COND_ROWS, 1, N_MODS * d)


def _mod_spec(layer, chunk, row_of):
    return pl.BlockSpec((1, 1, D_MODEL),
                        lambda *ids: (layer * COND_ROWS + row_of(*ids), 0, chunk))


def _group_rms(t, bd, gain):
    hi, lo = _split_bf16(t * t)
    ss = _dot(hi, bd) + _dot(lo, bd)
    return (t * lax.rsqrt(ss * (1.0 / ATTN_DIM) + EPS)) * gain


def _rope(t, cos, sin_signed, first_half):
    width = t.shape[-1]
    quarter = ATTN_DIM // 4
    ahead = pltpu.roll(t, width - quarter, axis=1)
    behind = pltpu.roll(t, quarter, axis=1)
    return t * cos + jnp.where(first_half, ahead, behind) * sin_signed


def _with_ones_columns(v):
    rows = v.shape[0]
    lane = lax.broadcasted_iota(jnp.int32, (rows, ATTN_VDIM), 1)
    ones_col = jnp.where(lane == 0, 1.0, 0.0).astype(v.dtype)
    parts = []
    for h in range(ATTN_HEADS):
        parts += [v[:, h * ATTN_VDIM:(h + 1) * ATTN_VDIM], ones_col]
    return jnp.concatenate(parts, axis=1)


def _side_cast(w, steps, index_of):
    rows, cols = w.shape
    assert rows % (steps * 16) == 0
    spec = pl.BlockSpec((rows // steps, cols), lambda *ids: (index_of(*ids), 0))
    return spec, jax.ShapeDtypeStruct(w.shape, BF16)


def _inproj_kernel(x_ref, xp_ref, xn_ref, sc_ref, sh_ref, g_ref, w_ref, bd_ref, qg_ref, kg_ref,
                   cos_ref, sin_ref, cw_ref, side_ref, q_ref, k_ref, v_ref, yc_ref, side_out_ref):
    side_out_ref[...] = side_ref[...].astype(BF16)
    i = pl.program_id(1)
    last = pl.num_programs(1) - 1
    tm = x_ref.shape[1]
    w = QK_W
    xe = jnp.concatenate([xp_ref[0], x_ref[0], xn_ref[0]], axis=0)
    he = _modulate(xe, g_ref[...], sc_ref[0], sh_ref[0]).astype(BF16)
    hm = he[HALO:HALO + tm]

    qkv = _dot(hm, w_ref[:, :3 * w])
    bd = bd_ref[...]
    cos = cos_ref[...]
    sin = sin_ref[...]
    lane = lax.broadcasted_iota(jnp.int32, (1, w), 1)
    first_half = (lane % (ATTN_DIM // 2)) < (ATTN_DIM // 4)
    q = _rope(_group_rms(qkv[:, :w], bd, qg_ref[...]), cos, sin, first_half)
    k = _rope(_group_rms(qkv[:, w:2 * w], bd, kg_ref[...]), cos, sin, first_half)
    q_ref[0] = (q * (ATTN_DIM ** -0.5 * LOG2_E)).astype(BF16)
    k_ref[0] = k.astype(BF16)
    v_ref[0] = _with_ones_columns(qkv[:, 2 * w:].astype(BF16))

    gcu = _dot(he, w_ref[:, 4 * w:])
    cu = gcu[:, :w] * gcu[:, w:]
    row = lax.broadcasted_iota(jnp.int32, (tm + 2 * HALO, 1), 0)
    first_row = jnp.where(i == 0, HALO, 0)
    end_row = jnp.where(i == last, HALO + tm, tm + 2 * HALO)
    cu = jnp.where((row >= first_row) & (row < end_row), cu, 0.0)
    prev = pltpu.roll(cu, 1, axis=0)[HALO:HALO + tm]
    nxt = pltpu.roll(cu, tm + 2 * HALO - 1, axis=0)[HALO:HALO + tm]
    cw = cw_ref[...]
    conv = prev * cw[0:1] + cu[HALO:HALO + tm] * cw[1:2] + nxt * cw[2:3]
    gate_b = _dot(hm, w_ref[:, 3 * w:4 * w])
    yc_ref[0] = (gate_b * conv).astype(BF16)


def _inproj(x, mods, g, w_in, bd, qg, kg, cos, sin, conv_w, side_w):
    b, s, d = x.shape
    tm = TM_PROJ
    nh = tm // HALO
    n_halo_blocks = s // HALO
    tok = pl.BlockSpec((1, tm, QK_W), lambda bi, i: (bi, i, 0))
    out_sds = jax.ShapeDtypeStruct((b, s, QK_W), BF16)
    side_spec, side_sds = _side_cast(side_w, b * (s // tm), lambda bi, i: bi * (s // tm) + i)
    return pl.pallas_call(
        _inproj_kernel,
        grid=(b, s // tm),
        in_specs=[
            pl.BlockSpec((1, tm, d), lambda bi, i: (bi, i, 0)),
            pl.BlockSpec((1, HALO, d), lambda bi, i: (bi, jnp.maximum(i * nh - 1, 0), 0)),
            pl.BlockSpec((1, HALO, d),
                         lambda bi, i: (bi, jnp.minimum((i + 1) * nh, n_halo_blocks - 1), 0)),
            _mod_spec(0, 1, lambda bi, i: bi),
            _mod_spec(0, 0, lambda bi, i: bi),
            _const_spec((1, d)),
            _const_spec(w_in.shape),
            _const_spec(bd.shape),
            _const_spec((1, QK_W)),
            _const_spec((1, QK_W)),
            pl.BlockSpec((tm, QK_W), lambda bi, i: (i, 0)),
            pl.BlockSpec((tm, QK_W), lambda bi, i: (i, 0)),
            _const_spec(conv_w.shape),
            side_spec,
        ],
        out_specs=[tok, tok, pl.BlockSpec((1, tm, 2 * QK_W), lambda bi, i: (bi, i, 0)), tok,
                   side_spec],
        out_shape=[out_sds, out_sds, jax.ShapeDtypeStruct((b, s, 2 * QK_W), BF16), out_sds,
                   side_sds],
        compiler_params=_params(2),
        name="inproj",
    )(x, x, x, mods, mods, g, w_in, bd, qg, kg, cos, sin, conv_w, side_w)


def _ctxproj_kernel(x_ref, sc_ref, sh_ref, g_ref, wk_ref, wv_ref, bd_ref, kg_ref, k_ref, v_ref):
    hc = _modulate(x_ref[0], g_ref[...], sc_ref[0], sh_ref[0]).astype(BF16)
    k_ref[0] = _group_rms(_dot(hc, wk_ref[...]), bd_ref[...], kg_ref[...]).astype(BF16)
    v_ref[0] = _with_ones_columns(_dot(hc, wv_ref[...]).astype(BF16))


def _ctxproj(ctx, mods, g, w_in, bd, kg, ctx_row):
    b, l, d = ctx.shape
    out = pl.BlockSpec((1, l, QK_W), lambda bi: (bi, 0, 0))
    out_sds = jax.ShapeDtypeStruct((b, l, QK_W), BF16)
    return pl.pallas_call(
        _ctxproj_kernel,
        grid=(b,),
        in_specs=[
            pl.BlockSpec((1, l, d), lambda bi: (bi, 0, 0)),
            _mod_spec(0, 1, lambda bi: ctx_row),
            _mod_spec(0, 0, lambda bi: ctx_row),
            _const_spec((1, d)),
            pl.BlockSpec((d, QK_W), lambda bi: (0, 1), pipeline_mode=pl.Buffered(1)),
            pl.BlockSpec((d, QK_W), lambda bi: (0, 2), pipeline_mode=pl.Buffered(1)),
            _const_spec(bd.shape),
            _const_spec((1, QK_W)),
        ],
        out_specs=[out, pl.BlockSpec((1, l, 2 * QK_W), lambda bi: (bi, 0, 0))],
        out_shape=[out_sds, jax.ShapeDtypeStruct((b, l, 2 * QK_W), BF16)],
        compiler_params=_params(1),
        name="ctxproj",
    )(ctx, mods, mods, g, w_in, w_in, bd, kg)


def _attn_kernel(q_ref, k_ref, kc_ref, v_ref, vc_ref, sg_ref, lq1_ref, lk1_ref, lq2_ref, lk2_ref,
                 *rest, lambda_init, n_side):
    side_refs, o_ref = rest[:n_side], rest[n_side]
    side_out_refs = rest[n_side + 1:2 * n_side + 1]
    s_ref, e_ref = rest[2 * n_side + 1:]
    for side_ref, side_out_ref in zip(side_refs, side_out_refs):
        side_out_ref[...] = side_ref[...].astype(BF16)
    tq = q_ref.shape[1]
    n_ctx = kc_ref.shape[1] // ATTN_KC
    n_lat = k_ref.shape[1] // ATTN_KC
    chunks = [(kc_ref, vc_ref, c) for c in range(n_ctx)] + [(k_ref, v_ref, c) for c in range(n_lat)]
    lane = lax.broadcasted_iota(jnp.int32, (1, ATTN_VDIM), 1)
    first = lane < ATTN_DIM
    lam = (jnp.exp(jnp.sum(lq1_ref[...] * lk1_ref[...], keepdims=True))
           - jnp.exp(jnp.sum(lq2_ref[...] * lk2_ref[...], keepdims=True)) + lambda_init)
    nt = (((1,), (1,)), ((), ()))

    for rb in range(tq // ATTN_RB):
        rows = slice(rb * ATTN_RB, (rb + 1) * ATTN_RB)
        q = q_ref[0, rows, :]
        zero = jnp.zeros_like(q)
        outs = []
        for part in range(2):
            qx = jnp.where(first, q, zero) if part == 0 else jnp.where(first, zero, q)
            slot = (rb * 2 + part) % ATTN_SLOTS
            mpart = None
            for ci, (kr, _, c) in enumerate(chunks):
                sc = lax.dot_general(qx, kr[0, c * ATTN_KC:(c + 1) * ATTN_KC, :], nt,
                                     preferred_element_type=F32)
                s_ref[slot, ci] = sc
                cm = sc[:, :ATTN_VDIM]
                for j in range(1, ATTN_KC // ATTN_VDIM):
                    cm = jnp.maximum(cm, sc[:, j * ATTN_VDIM:(j + 1) * ATTN_VDIM])
                mpart = cm if mpart is None else jnp.maximum(mpart, cm)
            m = jnp.max(mpart, axis=-1, keepdims=True)
            for ci in range(len(chunks)):
                e_ref[slot, :, ci * ATTN_KC:(ci + 1) * ATTN_KC] = (
                    jnp.exp2(s_ref[slot, ci] - m).astype(BF16))
            n_c = n_ctx * ATTN_KC
            acc = _dot(e_ref[slot, :, :n_c], vc_ref[0]) + _dot(e_ref[slot, :, n_c:], v_ref[0])
            outs.append(acc[:, :ATTN_VDIM] * (1.0 / acc[:, ATTN_VDIM:ATTN_VDIM + 1]))
        o = outs[0] - lam * outs[1]
        ms = jnp.mean(o * o, axis=-1, keepdims=True)
        y = (o * lax.rsqrt(ms + EPS)) * sg_ref[...]
        o_ref[0, rows, :] = (y * (1.0 - lambda_init)).astype(BF16)


def _attention(q, k, kc, v, vc, subln_g, lq1, lk1, lq2, lk2, lambda_init, side_ws):
    b, s, _ = q.shape
    l = kc.shape[1]
    hd = ATTN_VDIM
    va = 2 * ATTN_VDIM
    assert l % ATTN_KC == 0 and s % ATTN_KC == 0 and s == TQ
    n_chunks = (l + s) // ATTN_KC
    vec = _const_spec((1, ATTN_DIM))
    sides = [_side_cast(w, b * ATTN_HEADS, lambda bi, h, i: bi * ATTN_HEADS + h) for w in side_ws]
    return pl.pallas_call(
        functools.partial(_attn_kernel, lambda_init=lambda_init, n_side=len(side_ws)),
        grid=(b, ATTN_HEADS, s // TQ),
        in_specs=[
            pl.BlockSpec((1, TQ, hd), lambda bi, h, i: (bi, i, h)),
            pl.BlockSpec((1, s, hd), lambda bi, h, i: (bi, 0, h)),
            pl.BlockSpec((1, l, hd), lambda bi, h, i: (bi, 0, h)),
            pl.BlockSpec((1, s, va), lambda bi, h, i: (bi, 0, h)),
            pl.BlockSpec((1, l, va), lambda bi, h, i: (bi, 0, h)),
            _const_spec((1, hd)),
            vec, vec, vec, vec,
        ] + [spec for spec, _ in sides],
        out_specs=[pl.BlockSpec((1, TQ, hd), lambda bi, h, i: (bi, i, h))]
        + [spec for spec, _ in sides],
        out_shape=[jax.ShapeDtypeStruct((b, s, ATTN_HEADS * hd), BF16)] + [sds for _, sds in sides],
        scratch_shapes=[pltpu.VMEM((ATTN_SLOTS, n_chunks, ATTN_RB, ATTN_KC), F32),
                        pltpu.VMEM((ATTN_SLOTS, ATTN_RB, l + s), BF16)],
        compiler_params=_params(3),
        name="diff_attention",
    )(q, k, kc, v, vc, subln_g, lq1, lk1, lq2, lk2, *side_ws)


def _outproj_ffn_kernel(x_ref, at_ref, yc_ref, gm_ref, sf_ref, cf_ref, gf_ref, g_ref,
                        wo_ref, wg_ref, wu_ref, wd_ref, *rest):
    n_side = (len(rest) - 1) // 2
    o_ref = rest[n_side]
    for side_ref, side_out_ref in zip(rest[:n_side], rest[n_side + 1:]):
        side_out_ref[...] = side_ref[...].astype(BF16)
    half = wo_ref.shape[0] // 2
    d_ff = wg_ref.shape[1]
    for rb in range(x_ref.shape[1] // FFN_RB):
        rows = slice(rb * FFN_RB, (rb + 1) * FFN_RB)
        y = _dot(at_ref[0, rows, :], wo_ref[:half]) + _dot(yc_ref[0, rows, :], wo_ref[half:])
        h1 = x_ref[0, rows, :] + gm_ref[0] * y
        hf = _modulate(h1, g_ref[...], cf_ref[0], sf_ref[0]).astype(BF16)
        acc = jnp.zeros(h1.shape, F32)
        for f in range(d_ff // MXU_N):
            cols = slice(f * MXU_N, (f + 1) * MXU_N)
            a = jax.nn.silu(_dot(hf, wg_ref[:, cols])) * _dot(hf, wu_ref[:, cols])
            acc = acc + _dot(a.astype(BF16), wd_ref[cols, :])
        o_ref[0, rows, :] = h1 + gf_ref[0] * acc


def _outproj_ffn(x, attn, yconv, mods, g, w_out, wg, wu, wd, side_ws):
    b, s, d = x.shape
    tm = TM_FFN
    sides = [_side_cast(w, b * (s // tm), lambda bi, i: bi * (s // tm) + i) for w in side_ws]
    return pl.pallas_call(
        _outproj_ffn_kernel,
        grid=(b, s // tm),
        in_specs=[
            pl.BlockSpec((1, tm, d), lambda bi, i: (bi, i, 0)),
            pl.BlockSpec((1, tm, QK_W), lambda bi, i: (bi, i, 0)),
            pl.BlockSpec((1, tm, QK_W), lambda bi, i: (bi, i, 0)),
            _mod_spec(0, 2, lambda bi, i: bi),
            _mod_spec(0, 3, lambda bi, i: bi),
            _mod_spec(0, 4, lambda bi, i: bi),
            _mod_spec(0, 5, lambda bi, i: bi),
            _const_spec((1, d)),
            _const_spec(w_out.shape),
            _const_spec(wg.shape),
            _const_spec(wu.shape),
            _const_spec(wd.shape),
        ] + [spec for spec, _ in sides],
        out_specs=[pl.BlockSpec((1, tm, d), lambda bi, i: (bi, i, 0))] + [spec for spec, _ in sides],
        out_shape=[jax.ShapeDtypeStruct((b, s, d), F32)] + [sds for _, sds in sides],
        compiler_params=_params(2),
        name="outproj_ffn",
    )(x, attn, yconv, mods, mods, mods, mods, g, w_out, wg, wu, wd, *side_ws)


def _sgu_kernel(x_ref, sm_ref, cm_ref, gm_ref, sf_ref, cf_ref, g_ref, gff_ref, win_ref, lng_ref,
                lnb_ref, ws_ref, bs_ref, wo_ref, rwh_ref, rwl_ref, side_ref,
                h_ref, hf_ref, lg_ref, side_out_ref, us_ref):
    side_out_ref[...] = side_ref[...].astype(BF16)
    tm, d = x_ref.shape[1:]
    gw = d // SGU_GROUPS
    for rb in range(tm // SGU_RB):
        r0 = rb * SGU_RB
        x = x_ref[0, r0:r0 + SGU_RB, :]
        hm = _modulate(x, g_ref[...], cm_ref[0], sm_ref[0]).astype(BF16)
        z = jax.nn.gelu(_dot(hm, win_ref[...]))
        v = z[:, d:]
        mu = jnp.mean(v, axis=-1, keepdims=True)
        vc = v - mu
        var = jnp.mean(vc * vc, axis=-1, keepdims=True)
        vn = ((vc * lax.rsqrt(var + EPS)) * lng_ref[...] + lnb_ref[...]).astype(BF16)
        for n in range(SGU_RB // CHUNK):
            rows = slice(n * CHUNK, (n + 1) * CHUNK)
            for gi in range(SGU_GROUPS):
                cols = slice(gi * gw, (gi + 1) * gw)
                s = _dot(ws_ref[gi], vn[rows, cols]) + bs_ref[:, cols]
                us_ref[r0 + n * CHUNK:r0 + (n + 1) * CHUNK, cols] = (z[rows, cols] * s).astype(BF16)
        h = x + gm_ref[0] * _dot(us_ref[r0:r0 + SGU_RB, :], wo_ref[...])
        h_ref[0, r0:r0 + SGU_RB, :] = h
        hf = _modulate(h, gff_ref[...], cf_ref[0], sf_ref[0])
        _rows_to_tiles(hf_ref, hf, row0=r0)
        hi, lo = _split_bf16(hf)
        lg_ref[0, r0:r0 + SGU_RB, :] = (
            _dot(hi, rwh_ref[...]) + (_dot(hi, rwl_ref[...]) + _dot(lo, rwh_ref[...])))


def _sgu(h, mods, g_mix, g_ffn, w_in, ln_g, ln_b, w_s, bias, w_out, rw_hi, rw_lo, side_w):
    b, s, d = h.shape
    tm = TM_SGU
    tok = pl.BlockSpec((1, tm, d), lambda bi, i: (bi, i, 0))
    side_spec, side_sds = _side_cast(side_w, b * (s // tm), lambda bi, i: bi * (s // tm) + i)
    return pl.pallas_call(
        _sgu_kernel,
        grid=(b, s // tm),
        in_specs=[
            tok,
            _mod_spec(1, 0, lambda bi, i: bi),
            _mod_spec(1, 1, lambda bi, i: bi),
            _mod_spec(1, 2, lambda bi, i: bi),
            _mod_spec(1, 3, lambda bi, i: bi),
            _mod_spec(1, 4, lambda bi, i: bi),
            _const_spec((1, d)),
            _const_spec((1, d)),
            _const_spec(w_in.shape),
            _const_spec((1, d)),
            _const_spec((1, d)),
            _const_spec(w_s.shape),
            _const_spec(bias.shape),
            _const_spec(w_out.shape),
            _const_spec(rw_hi.shape),
            _const_spec(rw_lo.shape),
            side_spec,
        ],
        out_specs=[tok,
                   pl.BlockSpec((tm * SUBLANES, LANES), lambda bi, i: (bi * (s // tm) + i, 0)),
                   pl.BlockSpec((1, tm, N_EXPERTS), lambda bi, i: (bi, i, 0)),
                   side_spec],
        out_shape=[jax.ShapeDtypeStruct((b, s, d), F32),
                   jax.ShapeDtypeStruct((b * s * SUBLANES, LANES), F32),
                   jax.ShapeDtypeStruct((b, s, N_EXPERTS), F32),
                   side_sds],
        scratch_shapes=[pltpu.VMEM((tm, d), BF16)],
        compiler_params=_params(2),
        name="sgu",
    )(h, mods, mods, mods, mods, mods, g_mix, g_ffn, w_in, ln_g, ln_b, w_s, bias, w_out,
      rw_hi, rw_lo, side_w)


def _route(logits, tile_rows, n_tiles):
    lt = logits.T
    t = lt.shape[1]
    eid = lax.broadcasted_iota(jnp.int32, (N_EXPERTS, t), 0)
    i1 = jnp.argmax(lt, axis=0).astype(jnp.int32)
    v1 = jnp.max(lt, axis=0)
    first = eid == i1[None, :]
    rest = jnp.where(first, -jnp.inf, lt)
    i2 = jnp.argmax(rest, axis=0).astype(jnp.int32)
    v2 = jnp.max(rest, axis=0)
    second = eid == i2[None, :]
    e2 = jnp.exp(v2 - v1)
    gates = jnp.stack([1.0 / (1.0 + e2), e2 / (1.0 + e2)], axis=-1)
    onehot = (first | second).astype(jnp.int32)
    csum = jnp.cumsum(onehot, axis=1)
    counts = csum[:, -1]
    padded = ((counts + tile_rows - 1) // tile_rows) * tile_rows
    ends = jnp.cumsum(padded)
    starts = ends - padded
    slot = starts[:, None] + csum - onehot
    pos = jnp.concatenate([jnp.sum(jnp.where(first, slot, 0), axis=0),
                           jnp.sum(jnp.where(second, slot, 0), axis=0)]).astype(jnp.int32)
    tile_start = jnp.arange(n_tiles, dtype=jnp.int32) * tile_rows
    valid = tile_start < ends[-1]
    tile_e = jnp.sum(tile_start[:, None] >= ends[None, :], axis=-1).astype(jnp.int32)
    last_e = jnp.max(jnp.where(counts > 0, jnp.arange(N_EXPERTS, dtype=jnp.int32), 0))
    tile_e = jnp.where(valid, tile_e, last_e)
    total = jnp.full((1,), n_tiles * tile_rows, jnp.int32)
    pad_lo = jnp.concatenate([starts + counts, ends[-1:]]).astype(jnp.int32)
    pad_hi = jnp.concatenate([ends, total]).astype(jnp.int32)
    return pos, gates, tile_e, valid.astype(jnp.int32), pad_lo, pad_hi


INVERT_UNROLL = 32


def _invert_kernel(pos_ref, lo_ref, hi_ref, pair_ref):
    n_pairs = pos_ref.shape[0]
    assert n_pairs % INVERT_UNROLL == 0

    def fill(i, c):
        pair_ref[i] = -1
        return c

    def place(blk, c):
        for u in range(INVERT_UNROLL):
            i = blk * INVERT_UNROLL + u
            pair_ref[pos_ref[i]] = i
        return c

    for g in range(lo_ref.shape[0]):
        lax.fori_loop(lo_ref[g], hi_ref[g], fill, 0)
    lax.fori_loop(0, n_pairs // INVERT_UNROLL, place, 0)


def _row_tables(pos, pad_lo, pad_hi, n_tokens, n_tiles):
    n_rows = n_tiles * TM_EXP
    smem = pl.BlockSpec(memory_space=pltpu.SMEM)
    pair = pl.pallas_call(
        _invert_kernel,
        in_specs=[smem, smem, smem],
        out_specs=smem,
        out_shape=jax.ShapeDtypeStruct((n_rows,), jnp.int32),
        name="moe_invert",
    )(pos, pad_lo, pad_hi)
    scratch_row = n_tokens * TOP_K + jnp.arange(n_rows, dtype=jnp.int32) % TM_EXP
    src = jnp.where(pair < 0, 0, jnp.where(pair >= n_tokens, pair - n_tokens, pair))
    dst = jnp.where(pair < 0, scratch_row, pair)
    return src * SUBLANES, jnp.concatenate([scratch_row[:TM_EXP], dst]) * SUBLANES


def _expert_kernel(te_ref, tv_ref, tf_ref, src_ref, dst_ref, hf_ref, wg_ref, wu_ref, wd_ref, y_ref,
                   x_ref, xb_ref, acc_ref, yt_ref, sem_in, sem_out):
    del te_ref, tf_ref
    tm = TM_EXP
    j = pl.program_id(0)
    step = pl.program_id(1)
    valid = tv_ref[j] > 0
    prev_valid = jnp.logical_and(j > 0, tv_ref[jnp.maximum(j - 1, 0)] > 0)
    prev2_valid = jnp.logical_and(j > 1, tv_ref[jnp.maximum(j - 2, 0)] > 0)
    slot = j % 2
    x_rows = lambda r: x_ref.at[pl.ds(r * SUBLANES, SUBLANES)]
    yt_rows = lambda sl, r: yt_ref.at[sl, pl.ds(r * SUBLANES, SUBLANES)]

    def start_gathers(tile):
        for r in range(tm):
            t8 = src_ref[tile * tm + r]
            pltpu.make_async_copy(hf_ref.at[pl.ds(pl.multiple_of(t8, SUBLANES), SUBLANES)],
                                  x_rows(r), sem_in).start(priority=ROW_DMA_PRIORITY)

    def gather_wait(r):
        return pltpu.make_async_copy(hf_ref.at[pl.ds(0, SUBLANES)], x_rows(r), sem_in)

    def start_scatters(tile, sl):
        for r in range(tm):
            d8 = dst_ref[(tile + 1) * tm + r]
            pltpu.make_async_copy(yt_rows(sl, r),
                                  y_ref.at[pl.ds(pl.multiple_of(d8, SUBLANES), SUBLANES)],
                                  sem_out).start(priority=ROW_DMA_PRIORITY)

    def scatter_wait(sl, r):
        return pltpu.make_async_copy(yt_rows(sl, r), y_ref.at[pl.ds(0, SUBLANES)], sem_out)

    def ffn_step():
        xb = xb_ref[...]
        a = jax.nn.silu(_dot(xb, wg_ref[0])) * _dot(xb, wu_ref[0])
        return _dot(a.astype(BF16), wd_ref[0])

    @pl.when(jnp.logical_and(j == 0, step == 0))
    def _():
        yt_ref[...] = jnp.zeros_like(yt_ref)
        start_gathers(0)

    @pl.when(jnp.logical_and(step == 0, jnp.logical_or(j == 0, prev_valid)))
    def _():
        for r in range(tm):
            gather_wait(r).wait()

    @pl.when(jnp.logical_and(valid, step == 0))
    def _():
        xb_ref[...] = _tiles_to_rows(x_ref, tm).astype(BF16)
        start_gathers(j + 1)
        acc_ref[...] = ffn_step()

    @pl.when(jnp.logical_and(step == 1, jnp.logical_or(j == 1, prev2_valid)))
    def _():
        for r in range(tm):
            scatter_wait(slot, r).wait()

    @pl.when(jnp.logical_and(valid, step == 1))
    def _():
        start_scatters(j - 1, 1 - slot)
        _rows_to_tiles(yt_ref.at[slot], acc_ref[...] + ffn_step())

    @pl.when(jnp.logical_and(jnp.logical_and(jnp.logical_not(valid), prev_valid), step == 1))
    def _():
        start_scatters(j - 1, 1 - slot)


def _expert_ffn(tile_e, tile_valid, src, dst, hf, wg, wu, wd):
    n_tok = hf.shape[0] // SUBLANES
    d = D_MODEL
    n_tiles = tile_e.shape[0]
    d_ff = wg.shape[2]
    nf = d_ff // TF_EXP
    assert nf == 2
    j = jnp.arange(n_tiles, dtype=jnp.int32)
    first = j % 2
    n_valid = jnp.sum(tile_valid)
    last_f = 1 - (n_valid - 1) % 2
    half = jnp.where(tile_valid[:, None] > 0, jnp.stack([first, 1 - first], axis=1), last_f)
    tile_f = half.reshape(-1).astype(jnp.int32)

    return pl.pallas_call(
        _expert_kernel,
        grid_spec=pltpu.PrefetchScalarGridSpec(
            num_scalar_prefetch=5,
            grid=(n_tiles, nf),
            in_specs=[
                pl.BlockSpec(memory_space=pl.ANY),
                pl.BlockSpec((1, d, TF_EXP), lambda j, s, te, tv, tf, *_: (te[j], 0, tf[2 * j + s])),
                pl.BlockSpec((1, d, TF_EXP), lambda j, s, te, tv, tf, *_: (te[j], 0, tf[2 * j + s])),
                pl.BlockSpec((1, TF_EXP, d), lambda j, s, te, tv, tf, *_: (te[j], tf[2 * j + s], 0)),
            ],
            out_specs=pl.BlockSpec(memory_space=pl.ANY),
            scratch_shapes=[pltpu.VMEM((TM_EXP * SUBLANES, LANES), F32),
                            pltpu.VMEM((TM_EXP, d), BF16),
                            pltpu.VMEM((TM_EXP, d), F32),
                            pltpu.VMEM((2, TM_EXP * SUBLANES, LANES), F32),
                            pltpu.SemaphoreType.DMA(()), pltpu.SemaphoreType.DMA(())],
        ),
        out_shape=jax.ShapeDtypeStruct(((TOP_K * n_tok + TM_EXP) * SUBLANES, LANES), F32),
        compiler_params=_params(2),
        name="moe_experts",
    )(tile_e, tile_valid, tile_f, src, dst, hf, wg, wu, wd)


def _combine_kernel(y0_ref, y1_ref, h_ref, gate_ref, gf_ref, o_ref):
    gates = gate_ref[...]
    n = h_ref.shape[0]
    mix = gates[:, 0:1] * _tiles_to_rows(y0_ref, n) + gates[:, 1:2] * _tiles_to_rows(y1_ref, n)
    o_ref[...] = h_ref[...] + gf_ref[0] * mix


def _combine_rows(y, h, gates, mods, seq):
    t, d = h.shape
    steps_per_batch = seq // TG
    second = t // TG
    tok = pl.BlockSpec((TG, d), lambda i: (i, 0))
    return pl.pallas_call(
        _combine_kernel,
        grid=(t // TG,),
        in_specs=[
            pl.BlockSpec((TG * SUBLANES, LANES), lambda i: (i, 0)),
            pl.BlockSpec((TG * SUBLANES, LANES), lambda i: (second + i, 0)),
            tok,
            pl.BlockSpec((TG, TOP_K), lambda i: (i, 0)),
            pl.BlockSpec((1, 1, d), lambda i: (COND_ROWS + i // steps_per_batch, 0, N_MODS - 1)),
        ],
        out_specs=tok,
        out_shape=jax.ShapeDtypeStruct((t, d), F32),
        compiler_params=_params(1),
        name="moe_combine",
    )(y, y, h, gates, mods)


def _rope_tables(seq):
    rows = seq // GRID_W
    row = jnp.repeat(jnp.arange(rows), GRID_W).astype(F32)
    col = jnp.tile(jnp.arange(GRID_W), rows).astype(F32)
    quarter = ATTN_DIM // 4
    inv = ROPE_BASE ** (-jnp.arange(quarter, dtype=F32) / quarter)
    ar = row[:, None] * inv
    ac = col[:, None] * inv
    ang = jnp.concatenate([ar, ar, ac, ac], axis=-1)
    sign = jnp.tile(jnp.concatenate([-jnp.ones(quarter, F32), jnp.ones(quarter, F32)]), 2)
    reps = QK_W // ATTN_DIM
    return jnp.tile(jnp.cos(ang), (1, reps)), jnp.tile(jnp.sin(ang) * sign, (1, reps))


def kernel(x, c, ctx, c_ctx, ada_w, ada_b, norm_mix_g, norm_ffn_g, w_in_even, q_norm_g, k_norm_g,
           lam_q1, lam_k1, lam_q2, lam_k2, subln_g, conv_w, w_out_even, ffn_w_gate, ffn_w_up,
           ffn_w_down, sgu_w_in, sgu_ln_g, sgu_ln_b, sgu_w_s, sgu_b_s, sgu_w_out, router_w,
           moe_w_gate, moe_w_up, moe_w_down):
    b, s, d = x.shape
    assert d == D_MODEL and b < COND_ROWS and ada_w.shape[0] == 2
    assert s % TM_PROJ == 0 and s % TQ == 0 and s % TM_FFN == 0 and s % TM_SGU == 0 and s % TG == 0

    cond = jnp.zeros((COND_ROWS, d), F32).at[:b].set(c).at[b].set(c_ctx)
    mods = _ada_mods(cond, ada_w, ada_b)

    lambda_init = 0.8 - 0.6 * math.exp(-0.3 * 0)
    group = jnp.arange(QK_W) // ATTN_DIM
    bd = (group[:, None] == group[None, :]).astype(BF16)
    reps = QK_W // ATTN_DIM
    qg = jnp.tile(q_norm_g[0], reps)[None, :]
    kg = jnp.tile(k_norm_g[0], reps)[None, :]
    cos, sin = _rope_tables(s)
    w_in = w_in_even[0].astype(BF16)
    g_mix0 = norm_mix_g[0][None, :]
    n_e, _, d_ffe = moe_w_gate[0].shape
    q, k, v, yconv, moe_wg = _inproj(x, mods, g_mix0, w_in, bd, qg, kg, cos, sin, conv_w[0],
                                     moe_w_gate[0].reshape(n_e * d, d_ffe))
    kc, vc = _ctxproj(ctx, mods, g_mix0, w_in, bd, kg, ctx_row=b)
    attn, w_out0, ffn_wg, ffn_wu = _attention(
        q, k, kc, v, vc, subln_g[0][None, :], lam_q1[0][None, :], lam_k1[0][None, :],
        lam_q2[0][None, :], lam_k2[0][None, :], lambda_init,
        [w_out_even[0], ffn_w_gate[0], ffn_w_up[0]])
    h, moe_wd, sgu_win, sgu_wout = _outproj_ffn(
        x, attn, yconv, mods, norm_ffn_g[0][None, :], w_out0, ffn_wg, ffn_wu,
        ffn_w_down[0].astype(BF16),
        [moe_w_down[0].reshape(n_e * d_ffe, d), sgu_w_in[0], sgu_w_out[0]])

    gw = d // SGU_GROUPS
    bias = jnp.repeat(sgu_b_s[0].T, gw, axis=1)
    rw_hi = router_w[0].astype(BF16)
    rw_lo = (router_w[0] - rw_hi.astype(F32)).astype(BF16)
    h, hf, logits, moe_wu = _sgu(h, mods, norm_mix_g[1][None, :], norm_ffn_g[1][None, :],
                                 sgu_win, sgu_ln_g[0][None, :],
                                 sgu_ln_b[0][None, :], sgu_w_s[0].astype(BF16), bias,
                                 sgu_wout, rw_hi, rw_lo,
                                 moe_w_up[0].reshape(n_e * d, d_ffe))

    t = b * s
    n_tiles = (t * TOP_K) // TM_EXP + N_EXPERTS + 2
    pos, gates, tile_e, tile_valid, pad_lo, pad_hi = _route(
        logits.reshape(t, N_EXPERTS), TM_EXP, n_tiles)
    src, dst = _row_tables(pos, pad_lo, pad_hi, t, n_tiles)
    y = _expert_ffn(tile_e, tile_valid, src, dst, hf, moe_wg.reshape(n_e, d, d_ffe),
                    moe_wu.reshape(n_e, d, d_ffe), moe_wd.reshape(n_e, d_ffe, d))
    out = _combine_rows(y, h.reshape(t, d), gates, mods, s)
    return out.reshape(b, s, d)
```

```python
import functools
import math

import jax
import jax.numpy as jnp
from jax import lax
from jax.experimental import pallas as pl
from jax.experimental.pallas import tpu as pltpu

F32 = jnp.float32
BF16 = jnp.bfloat16

D_MODEL = 1024
GRID_W = 64
ATTN_HEADS = 4
ATTN_DIM = 64
ATTN_VDIM = 2 * ATTN_DIM
QK_W = ATTN_HEADS * 2 * ATTN_DIM
CONV_K = 3
ROPE_BASE = 10000.0
SGU_GROUPS = 4
CHUNK = 128
N_EXPERTS = 8
TOP_K = 2
EPS = 1e-6
LOG2_E = math.log2(math.e)
N_MODS = 6
COND_ROWS = 16

V7X_VMEM_LIMIT = 56 * 1024 * 1024
MXU_N = 256
TM_PROJ = 512
HALO = 16
TQ = 2048
ATTN_RB = 256
ATTN_SLOTS = 8
ATTN_KC = 256
TM_FFN = 512
FFN_RB = 512
TM_SGU = 1024
SGU_RB = 256
TM_EXP = 512
TF_EXP = 1792
TG = 512
ROW_DMA_PRIORITY = 1


def _params(n_axes):
    return pltpu.CompilerParams(
        dimension_semantics=("arbitrary",) * n_axes, vmem_limit_bytes=V7X_VMEM_LIMIT)


def _const_spec(shape):
    zeros = (0,) * len(shape)
    return pl.BlockSpec(shape, lambda *_: zeros, pipeline_mode=pl.Buffered(1))


def _dot(a, b):
    return jnp.dot(a, b, preferred_element_type=F32)


def _modulate(x, g, scale, shift):
    ms = jnp.mean(x * x, axis=-1, keepdims=True)
    y = x * lax.rsqrt(ms + EPS)
    return (y * g) * (1.0 + scale) + shift


def _split_bf16(x):
    hi = x.astype(BF16)
    lo = (x - hi.astype(F32)).astype(BF16)
    return hi, lo


assert D_MODEL == 8 * 128
SUBLANES = 8
LANES = 128


def _rows_to_tiles(ref, val, row0=0):
    n = val.shape[0]
    for c in range(SUBLANES):
        ref[pl.ds(row0 * SUBLANES + c, n, stride=SUBLANES), :] = val[:, c * LANES:(c + 1) * LANES]


def _tiles_to_rows(ref, n):
    return jnp.concatenate(
        [ref[pl.ds(c, n, stride=SUBLANES), :] for c in range(SUBLANES)], axis=1)


def _ada_kernel(cond_ref, w_ref, b_ref, o_ref):
    a = jax.nn.silu(cond_ref[...]).astype(BF16)
    o_ref[0] = _dot(a, w_ref[0].astype(BF16)) + b_ref[0]


def _ada_mods(cond, ada_w, ada_b):
    depth = ada_w.shape[0]
    d = D_MODEL
    out = pl.pallas_call(
        _ada_kernel,
        grid=(depth, N_MODS),
        in_specs=[
            pl.BlockSpec((COND_ROWS, d), lambda l, n: (0, 0)),
            pl.BlockSpec((1, d, d), lambda l, n: (l, 0, n)),
            pl.BlockSpec((1, 1, d), lambda l, n: (l, 0, n)),
        ],
        out_specs=pl.BlockSpec((1, COND_ROWS, d), lambda l, n: (l, 0, n)),
        out_shape=jax.ShapeDtypeStruct((depth, COND_ROWS, N_MODS * d), F32),
        compiler_params=_params(2),
        name="ada_mods",
    )(cond, ada_w, ada_b.reshape(depth, 1, N_MODS * d))
    return out.reshape(depth * COND_ROWS, 1, N_MODS * d)


def _mod_spec(layer, chunk, row_of):
    return pl.BlockSpec((1, 1, D_MODEL),
                        lambda *ids: (layer * COND_ROWS + row_of(*ids), 0, chunk))


def _group_rms(t, bd, gain):
    hi, lo = _split_bf16(t * t)
    ss = _dot(hi, bd) + _dot(lo, bd)
    return (t * lax.rsqrt(ss * (1.0 / ATTN_DIM) + EPS)) * gain


def _rope(t, cos, sin_signed, first_half):
    width = t.shape[-1]
    quarter = ATTN_DIM // 4
    ahead = pltpu.roll(t, width - quarter, axis=1)
    behind = pltpu.roll(t, quarter, axis=1)
    return t * cos + jnp.where(first_half, ahead, behind) * sin_signed


def _with_ones_columns(v):
    rows = v.shape[0]
    lane = lax.broadcasted_iota(jnp.int32, (rows, ATTN_VDIM), 1)
    ones_col = jnp.where(lane == 0, 1.0, 0.0).astype(v.dtype)
    parts = []
    for h in range(ATTN_HEADS):
        parts += [v[:, h * ATTN_VDIM:(h + 1) * ATTN_VDIM], ones_col]
    return jnp.concatenate(parts, axis=1)


def _side_cast(w, steps, index_of):
    rows, cols = w.shape
    assert rows % (steps * 16) == 0
    spec = pl.BlockSpec((rows // steps, cols), lambda *ids: (index_of(*ids), 0))
    return spec, jax.ShapeDtypeStruct(w.shape, BF16)


def _inproj_kernel(x_ref, xp_ref, xn_ref, sc_ref, sh_ref, g_ref, w_ref, bd_ref, qg_ref, kg_ref,
                   cos_ref, sin_ref, cw_ref, side_ref, q_ref, k_ref, v_ref, yc_ref, side_out_ref):
    side_out_ref[...] = side_ref[...].astype(BF16)
    i = pl.program_id(1)
    last = pl.num_programs(1) - 1
    tm = x_ref.shape[1]
    w = QK_W
    xe = jnp.concatenate([xp_ref[0], x_ref[0], xn_ref[0]], axis=0)
    he = _modulate(xe, g_ref[...], sc_ref[0], sh_ref[0]).astype(BF16)
    hm = he[HALO:HALO + tm]

    qkv = _dot(hm, w_ref[:, :3 * w])
    bd = bd_ref[...]
    cos = cos_ref[...]
    sin = sin_ref[...]
    lane = lax.broadcasted_iota(jnp.int32, (1, w), 1)
    first_half = (lane % (ATTN_DIM // 2)) < (ATTN_DIM // 4)
    q = _rope(_group_rms(qkv[:, :w], bd, qg_ref[...]), cos, sin, first_half)
    k = _rope(_group_rms(qkv[:, w:2 * w], bd, kg_ref[...]), cos, sin, first_half)
    q_ref[0] = (q * (ATTN_DIM ** -0.5 * LOG2_E)).astype(BF16)
    k_ref[0] = k.astype(BF16)
    v_ref[0] = _with_ones_columns(qkv[:, 2 * w:].astype(BF16))

    gcu = _dot(he, w_ref[:, 4 * w:])
    cu = gcu[:, :w] * gcu[:, w:]
    row = lax.broadcasted_iota(jnp.int32, (tm + 2 * HALO, 1), 0)
    first_row = jnp.where(i == 0, HALO, 0)
    end_row = jnp.where(i == last, HALO + tm, tm + 2 * HALO)
    cu = jnp.where((row >= first_row) & (row < end_row), cu, 0.0)
    prev = pltpu.roll(cu, 1, axis=0)[HALO:HALO + tm]
    nxt = pltpu.roll(cu, tm + 2 * HALO - 1, axis=0)[HALO:HALO + tm]
    cw = cw_ref[...]
    conv = prev * cw[0:1] + cu[HALO:HALO + tm] * cw[1:2] + nxt * cw[2:3]
    gate_b = _dot(hm, w_ref[:, 3 * w:4 * w])
    yc_ref[0] = (gate_b * conv).astype(BF16)


def _inproj(x, mods, g, w_in, bd, qg, kg, cos, sin, conv_w, side_w):
    b, s, d = x.shape
    tm = TM_PROJ
    nh = tm // HALO
    n_halo_blocks = s // HALO
    tok = pl.BlockSpec((1, tm, QK_W), lambda bi, i: (bi, i, 0))
    out_sds = jax.ShapeDtypeStruct((b, s, QK_W), BF16)
    side_spec, side_sds = _side_cast(side_w, b * (s // tm), lambda bi, i: bi * (s // tm) + i)
    return pl.pallas_call(
        _inproj_kernel,
        grid=(b, s // tm),
        in_specs=[
            pl.BlockSpec((1, tm, d), lambda bi, i: (bi, i, 0)),
            pl.BlockSpec((1, HALO, d), lambda bi, i: (bi, jnp.maximum(i * nh - 1, 0), 0)),
            pl.BlockSpec((1, HALO, d),
                         lambda bi, i: (bi, jnp.minimum((i + 1) * nh, n_halo_blocks - 1), 0)),
            _mod_spec(0, 1, lambda bi, i: bi),
            _mod_spec(0, 0, lambda bi, i: bi),
            _const_spec((1, d)),
            _const_spec(w_in.shape),
            _const_spec(bd.shape),
            _const_spec((1, QK_W)),
            _const_spec((1, QK_W)),
            pl.BlockSpec((tm, QK_W), lambda bi, i: (i, 0)),
            pl.BlockSpec((tm, QK_W), lambda bi, i: (i, 0)),
            _const_spec(conv_w.shape),
            side_spec,
        ],
        out_specs=[tok, tok, pl.BlockSpec((1, tm, 2 * QK_W), lambda bi, i: (bi, i, 0)), tok,
                   side_spec],
        out_shape=[out_sds, out_sds, jax.ShapeDtypeStruct((b, s, 2 * QK_W), BF16), out_sds,
                   side_sds],
        compiler_params=_params(2),
        name="inproj",
    )(x, x, x, mods, mods, g, w_in, bd, qg, kg, cos, sin, conv_w, side_w)


def _ctxproj_kernel(x_ref, sc_ref, sh_ref, g_ref, wk_ref, wv_ref, bd_ref, kg_ref, k_ref, v_ref):
    hc = _modulate(x_ref[0], g_ref[...], sc_ref[0], sh_ref[0]).astype(BF16)
    k_ref[0] = _group_rms(_dot(hc, wk_ref[...]), bd_ref[...], kg_ref[...]).astype(BF16)
    v_ref[0] = _with_ones_columns(_dot(hc, wv_ref[...]).astype(BF16))


def _ctxproj(ctx, mods, g, w_in, bd, kg, ctx_row):
    b, l, d = ctx.shape
    out = pl.BlockSpec((1, l, QK_W), lambda bi: (bi, 0, 0))
    out_sds = jax.ShapeDtypeStruct((b, l, QK_W), BF16)
    return pl.pallas_call(
        _ctxproj_kernel,
        grid=(b,),
        in_specs=[
            pl.BlockSpec((1, l, d), lambda bi: (bi, 0, 0)),
            _mod_spec(0, 1, lambda bi: ctx_row),
            _mod_spec(0, 0, lambda bi: ctx_row),
            _const_spec((1, d)),
            pl.BlockSpec((d, QK_W), lambda bi: (0, 1), pipeline_mode=pl.Buffered(1)),
            pl.BlockSpec((d, QK_W), lambda bi: (0, 2), pipeline_mode=pl.Buffered(1)),
            _const_spec(bd.shape),
            _const_spec((1, QK_W)),
        ],
        out_specs=[out, pl.BlockSpec((1, l, 2 * QK_W), lambda bi: (bi, 0, 0))],
        out_shape=[out_sds, jax.ShapeDtypeStruct((b, l, 2 * QK_W), BF16)],
        compiler_params=_params(1),
        name="ctxproj",
    )(ctx, mods, mods, g, w_in, w_in, bd, kg)


def _attn_kernel(q_ref, k_ref, kc_ref, v_ref, vc_ref, sg_ref, lq1_ref, lk1_ref, lq2_ref, lk2_ref,
                 *rest, lambda_init, n_side):
    side_refs, o_ref = rest[:n_side], rest[n_side]
    side_out_refs = rest[n_side + 1:2 * n_side + 1]
    s_ref, e_ref = rest[2 * n_side + 1:]
    for side_ref, side_out_ref in zip(side_refs, side_out_refs):
        side_out_ref[...] = side_ref[...].astype(BF16)
    tq = q_ref.shape[1]
    n_ctx = kc_ref.shape[1] // ATTN_KC
    n_lat = k_ref.shape[1] // ATTN_KC
    chunks = [(kc_ref, vc_ref, c) for c in range(n_ctx)] + [(k_ref, v_ref, c) for c in range(n_lat)]
    lane = lax.broadcasted_iota(jnp.int32, (1, ATTN_VDIM), 1)
    first = lane < ATTN_DIM
    lam = (jnp.exp(jnp.sum(lq1_ref[...] * lk1_ref[...], keepdims=True))
           - jnp.exp(jnp.sum(lq2_ref[...] * lk2_ref[...], keepdims=True)) + lambda_init)
    nt = (((1,), (1,)), ((), ()))

    for rb in range(tq // ATTN_RB):
        rows = slice(rb * ATTN_RB, (rb + 1) * ATTN_RB)
        q = q_ref[0, rows, :]
        zero = jnp.zeros_like(q)
        outs = []
        for part in range(2):
            qx = jnp.where(first, q, zero) if part == 0 else jnp.where(first, zero, q)
            slot = (rb * 2 + part) % ATTN_SLOTS
            mpart = None
            for ci, (kr, _, c) in enumerate(chunks):
                sc = lax.dot_general(qx, kr[0, c * ATTN_KC:(c + 1) * ATTN_KC, :], nt,
                                     preferred_element_type=F32)
                s_ref[slot, ci] = sc
                cm = sc[:, :ATTN_VDIM]
                for j in range(1, ATTN_KC // ATTN_VDIM):
                    cm = jnp.maximum(cm, sc[:, j * ATTN_VDIM:(j + 1) * ATTN_VDIM])
                mpart = cm if mpart is None else jnp.maximum(mpart, cm)
            m = jnp.max(mpart, axis=-1, keepdims=True)
            for ci in range(len(chunks)):
                e_ref[slot, :, ci * ATTN_KC:(ci + 1) * ATTN_KC] = (
                    jnp.exp2(s_ref[slot, ci] - m).astype(BF16))
            n_c = n_ctx * ATTN_KC
            acc = _dot(e_ref[slot, :, :n_c], vc_ref[0]) + _dot(e_ref[slot, :, n_c:], v_ref[0])
            outs.append(acc[:, :ATTN_VDIM] * (1.0 / acc[:, ATTN_VDIM:ATTN_VDIM + 1]))
        o = outs[0] - lam * outs[1]
        ms = jnp.mean(o * o, axis=-1, keepdims=True)
        y = (o * lax.rsqrt(ms + EPS)) * sg_ref[...]
        o_ref[0, rows, :] = (y * (1.0 - lambda_init)).astype(BF16)


def _attention(q, k, kc, v, vc, subln_g, lq1, lk1, lq2, lk2, lambda_init, side_ws):
    b, s, _ = q.shape
    l = kc.shape[1]
    hd = ATTN_VDIM
    va = 2 * ATTN_VDIM
    assert l % ATTN_KC == 0 and s % ATTN_KC == 0 and s == TQ
    n_chunks = (l + s) // ATTN_KC
    vec = _const_spec((1, ATTN_DIM))
    sides = [_side_cast(w, b * ATTN_HEADS, lambda bi, h, i: bi * ATTN_HEADS + h) for w in side_ws]
    return pl.pallas_call(
        functools.partial(_attn_kernel, lambda_init=lambda_init, n_side=len(side_ws)),
        grid=(b, ATTN_HEADS, s // TQ),
        in_specs=[
            pl.BlockSpec((1, TQ, hd), lambda bi, h, i: (bi, i, h)),
            pl.BlockSpec((1, s, hd), lambda bi, h, i: (bi, 0, h)),
            pl.BlockSpec((1, l, hd), lambda bi, h, i: (bi, 0, h)),
            pl.BlockSpec((1, s, va), lambda bi, h, i: (bi, 0, h)),
            pl.BlockSpec((1, l, va), lambda bi, h, i: (bi, 0, h)),
            _const_spec((1, hd)),
            vec, vec, vec, vec,
        ] + [spec for spec, _ in sides],
        out_specs=[pl.BlockSpec((1, TQ, hd), lambda bi, h, i: (bi, i, h))]
        + [spec for spec, _ in sides],
        out_shape=[jax.ShapeDtypeStruct((b, s, ATTN_HEADS * hd), BF16)] + [sds for _, sds in sides],
        scratch_shapes=[pltpu.VMEM((ATTN_SLOTS, n_chunks, ATTN_RB, ATTN_KC), F32),
                        pltpu.VMEM((ATTN_SLOTS, ATTN_RB, l + s), BF16)],
        compiler_params=_params(3),
        name="diff_attention",
    )(q, k, kc, v, vc, subln_g, lq1, lk1, lq2, lk2, *side_ws)


def _outproj_ffn_kernel(x_ref, at_ref, yc_ref, gm_ref, sf_ref, cf_ref, gf_ref, g_ref,
                        wo_ref, wg_ref, wu_ref, wd_ref, *rest):
    n_side = (len(rest) - 1) // 2
    o_ref = rest[n_side]
    for side_ref, side_out_ref in zip(rest[:n_side], rest[n_side + 1:]):
        side_out_ref[...] = side_ref[...].astype(BF16)
    half = wo_ref.shape[0] // 2
    d_ff = wg_ref.shape[1]
    for rb in range(x_ref.shape[1] // FFN_RB):
        rows = slice(rb * FFN_RB, (rb + 1) * FFN_RB)
        y = _dot(at_ref[0, rows, :], wo_ref[:half]) + _dot(yc_ref[0, rows, :], wo_ref[half:])
        h1 = x_ref[0, rows, :] + gm_ref[0] * y
        hf = _modulate(h1, g_ref[...], cf_ref[0], sf_ref[0]).astype(BF16)
        acc = jnp.zeros(h1.shape, F32)
        for f in range(d_ff // MXU_N):
            cols = slice(f * MXU_N, (f + 1) * MXU_N)
            a = jax.nn.silu(_dot(hf, wg_ref[:, cols])) * _dot(hf, wu_ref[:, cols])
            acc = acc + _dot(a.astype(BF16), wd_ref[cols, :])
        o_ref[0, rows, :] = h1 + gf_ref[0] * acc


def _outproj_ffn(x, attn, yconv, mods, g, w_out, wg, wu, wd, side_ws):
    b, s, d = x.shape
    tm = TM_FFN
    sides = [_side_cast(w, b * (s // tm), lambda bi, i: bi * (s // tm) + i) for w in side_ws]
    return pl.pallas_call(
        _outproj_ffn_kernel,
        grid=(b, s // tm),
        in_specs=[
            pl.BlockSpec((1, tm, d), lambda bi, i: (bi, i, 0)),
            pl.BlockSpec((1, tm, QK_W), lambda bi, i: (bi, i, 0)),
            pl.BlockSpec((1, tm, QK_W), lambda bi, i: (bi, i, 0)),
            _mod_spec(0, 2, lambda bi, i: bi),
            _mod_spec(0, 3, lambda bi, i: bi),
            _mod_spec(0, 4, lambda bi, i: bi),
            _mod_spec(0, 5, lambda bi, i: bi),
            _const_spec((1, d)),
            _const_spec(w_out.shape),
            _const_spec(wg.shape),
            _const_spec(wu.shape),
            _const_spec(wd.shape),
        ] + [spec for spec, _ in sides],
        out_specs=[pl.BlockSpec((1, tm, d), lambda bi, i: (bi, i, 0))] + [spec for spec, _ in sides],
        out_shape=[jax.ShapeDtypeStruct((b, s, d), F32)] + [sds for _, sds in sides],
        compiler_params=_params(2),
        name="outproj_ffn",
    )(x, attn, yconv, mods, mods, mods, mods, g, w_out, wg, wu, wd, *side_ws)


def _sgu_kernel(x_ref, sm_ref, cm_ref, gm_ref, sf_ref, cf_ref, g_ref, gff_ref, win_ref, lng_ref,
                lnb_ref, ws_ref, bs_ref, wo_ref, rwh_ref, rwl_ref, side_ref,
                h_ref, hf_ref, lg_ref, side_out_ref, us_ref):
    side_out_ref[...] = side_ref[...].astype(BF16)
    tm, d = x_ref.shape[1:]
    gw = d // SGU_GROUPS
    for rb in range(tm // SGU_RB):
        r0 = rb * SGU_RB
        x = x_ref[0, r0:r0 + SGU_RB, :]
        hm = _modulate(x, g_ref[...], cm_ref[0], sm_ref[0]).astype(BF16)
        z = jax.nn.gelu(_dot(hm, win_ref[...]))
        v = z[:, d:]
        mu = jnp.mean(v, axis=-1, keepdims=True)
        vc = v - mu
        var = jnp.mean(vc * vc, axis=-1, keepdims=True)
        vn = ((vc * lax.rsqrt(var + EPS)) * lng_ref[...] + lnb_ref[...]).astype(BF16)
        for n in range(SGU_RB // CHUNK):
            rows = slice(n * CHUNK, (n + 1) * CHUNK)
            for gi in range(SGU_GROUPS):
                cols = slice(gi * gw, (gi + 1) * gw)
                s = _dot(ws_ref[gi], vn[rows, cols]) + bs_ref[:, cols]
                us_ref[r0 + n * CHUNK:r0 + (n + 1) * CHUNK, cols] = (z[rows, cols] * s).astype(BF16)
        h = x + gm_ref[0] * _dot(us_ref[r0:r0 + SGU_RB, :], wo_ref[...])
        h_ref[0, r0:r0 + SGU_RB, :] = h
        hf = _modulate(h, gff_ref[...], cf_ref[0], sf_ref[0])
        _rows_to_tiles(hf_ref, hf, row0=r0)
        hi, lo = _split_bf16(hf)
        lg_ref[0, r0:r0 + SGU_RB, :] = (
            _dot(hi, rwh_ref[...]) + (_dot(hi, rwl_ref[...]) + _dot(lo, rwh_ref[...])))


def _sgu(h, mods, g_mix, g_ffn, w_in, ln_g, ln_b, w_s, bias, w_out, rw_hi, rw_lo, side_w):
    b, s, d = h.shape
    tm = TM_SGU
    tok = pl.BlockSpec((1, tm, d), lambda bi, i: (bi, i, 0))
    side_spec, side_sds = _side_cast(side_w, b * (s // tm), lambda bi, i: bi * (s // tm) + i)
    return pl.pallas_call(
        _sgu_kernel,
        grid=(b, s // tm),
        in_specs=[
            tok,
            _mod_spec(1, 0, lambda bi, i: bi),
            _mod_spec(1, 1, lambda bi, i: bi),
            _mod_spec(1, 2, lambda bi, i: bi),
            _mod_spec(1, 3, lambda bi, i: bi),
            _mod_spec(1, 4, lambda bi, i: bi),
            _const_spec((1, d)),
            _const_spec((1, d)),
            _const_spec(w_in.shape),
            _const_spec((1, d)),
            _const_spec((1, d)),
            _const_spec(w_s.shape),
            _const_spec(bias.shape),
            _const_spec(w_out.shape),
            _const_spec(rw_hi.shape),
            _const_spec(rw_lo.shape),
            side_spec,
        ],
        out_specs=[tok,
                   pl.BlockSpec((tm * SUBLANES, LANES), lambda bi, i: (bi * (s // tm) + i, 0)),
                   pl.BlockSpec((1, tm, N_EXPERTS), lambda bi, i: (bi, i, 0)),
                   side_spec],
        out_shape=[jax.ShapeDtypeStruct((b, s, d), F32),
                   jax.ShapeDtypeStruct((b * s * SUBLANES, LANES), F32),
                   jax.ShapeDtypeStruct((b, s, N_EXPERTS), F32),
                   side_sds],
        scratch_shapes=[pltpu.VMEM((tm, d), BF16)],
        compiler_params=_params(2),
        name="sgu",
    )(h, mods, mods, mods, mods, mods, g_mix, g_ffn, w_in, ln_g, ln_b, w_s, bias, w_out,
      rw_hi, rw_lo, side_w)


def _route(logits, tile_rows, n_tiles):
    lt = logits.T
    t = lt.shape[1]
    eid = lax.broadcasted_iota(jnp.int32, (N_EXPERTS, t), 0)
    i1 = jnp.argmax(lt, axis=0).astype(jnp.int32)
    v1 = jnp.max(lt, axis=0)
    first = eid == i1[None, :]
    rest = jnp.where(first, -jnp.inf, lt)
    i2 = jnp.argmax(rest, axis=0).astype(jnp.int32)
    v2 = jnp.max(rest, axis=0)
    second = eid == i2[None, :]
    e2 = jnp.exp(v2 - v1)
    gates = jnp.stack([1.0 / (1.0 + e2), e2 / (1.0 + e2)], axis=-1)
    onehot = (first | second).astype(jnp.int32)
    csum = jnp.cumsum(onehot, axis=1)
    counts = csum[:, -1]
    padded = ((counts + tile_rows - 1) // tile_rows) * tile_rows
    ends = jnp.cumsum(padded)
    starts = ends - padded
    slot = starts[:, None] + csum - onehot
    pos = jnp.concatenate([jnp.sum(jnp.where(first, slot, 0), axis=0),
                           jnp.sum(jnp.where(second, slot, 0), axis=0)]).astype(jnp.int32)
    tile_start = jnp.arange(n_tiles, dtype=jnp.int32) * tile_rows
    valid = tile_start < ends[-1]
    tile_e = jnp.sum(tile_start[:, None] >= ends[None, :], axis=-1).astype(jnp.int32)
    last_e = jnp.max(jnp.where(counts > 0, jnp.arange(N_EXPERTS, dtype=jnp.int32), 0))
    tile_e = jnp.where(valid, tile_e, last_e)
    total = jnp.full((1,), n_tiles * tile_rows, jnp.int32)
    pad_lo = jnp.concatenate([starts + counts, ends[-1:]]).astype(jnp.int32)
    pad_hi = jnp.concatenate([ends, total]).astype(jnp.int32)
    return pos, gates, tile_e, valid.astype(jnp.int32), pad_lo, pad_hi


INVERT_UNROLL = 32
FILL_UNROLL = 8


def _invert_kernel(pos_ref, lo_ref, hi_ref, pair_ref):
    n_pairs = pos_ref.shape[0]
    assert n_pairs % INVERT_UNROLL == 0

    def fill(blk, lo):
        for u in range(FILL_UNROLL):
            pair_ref[lo + blk * FILL_UNROLL + u] = -1
        return lo

    def place(blk, c):
        for u in range(INVERT_UNROLL):
            i = blk * INVERT_UNROLL + u
            pair_ref[pos_ref[i]] = i
        return c

    for g in range(lo_ref.shape[0]):
        n_blocks = (hi_ref[g] - lo_ref[g] + FILL_UNROLL - 1) // FILL_UNROLL
        lax.fori_loop(0, n_blocks, fill, lo_ref[g])
    lax.fori_loop(0, n_pairs // INVERT_UNROLL, place, 0)


def _row_tables(pos, pad_lo, pad_hi, n_tokens, n_tiles):
    n_rows = n_tiles * TM_EXP
    smem = pl.BlockSpec(memory_space=pltpu.SMEM)
    pair = pl.pallas_call(
        _invert_kernel,
        in_specs=[smem, smem, smem],
        out_specs=smem,
        out_shape=jax.ShapeDtypeStruct((n_rows,), jnp.int32),
        name="moe_invert",
    )(pos, pad_lo, pad_hi)
    scratch_row = n_tokens * TOP_K + jnp.arange(n_rows, dtype=jnp.int32) % TM_EXP
    src = jnp.where(pair < 0, 0, jnp.where(pair >= n_tokens, pair - n_tokens, pair))
    dst = jnp.where(pair < 0, scratch_row, pair)
    return src * SUBLANES, jnp.concatenate([scratch_row[:TM_EXP], dst]) * SUBLANES


def _expert_kernel(te_ref, tv_ref, tf_ref, src_ref, dst_ref, hf_ref, wg_ref, wu_ref, wd_ref, y_ref,
                   x_ref, xb_ref, acc_ref, yt_ref, sem_in, sem_out):
    del te_ref, tf_ref
    tm = TM_EXP
    j = pl.program_id(0)
    step = pl.program_id(1)
    valid = tv_ref[j] > 0
    prev_valid = jnp.logical_and(j > 0, tv_ref[jnp.maximum(j - 1, 0)] > 0)
    prev2_valid = jnp.logical_and(j > 1, tv_ref[jnp.maximum(j - 2, 0)] > 0)
    slot = j % 2
    x_rows = lambda r: x_ref.at[pl.ds(r * SUBLANES, SUBLANES)]
    yt_rows = lambda sl, r: yt_ref.at[sl, pl.ds(r * SUBLANES, SUBLANES)]

    def start_gathers(tile):
        for r in range(tm):
            t8 = src_ref[tile * tm + r]
            pltpu.make_async_copy(hf_ref.at[pl.ds(pl.multiple_of(t8, SUBLANES), SUBLANES)],
                                  x_rows(r), sem_in).start(priority=ROW_DMA_PRIORITY)

    def gather_wait(r):
        return pltpu.make_async_copy(hf_ref.at[pl.ds(0, SUBLANES)], x_rows(r), sem_in)

    def start_scatters(tile, sl):
        for r in range(tm):
            d8 = dst_ref[(tile + 1) * tm + r]
            pltpu.make_async_copy(yt_rows(sl, r),
                                  y_ref.at[pl.ds(pl.multiple_of(d8, SUBLANES), SUBLANES)],
                                  sem_out).start(priority=ROW_DMA_PRIORITY)

    def scatter_wait(sl, r):
        return pltpu.make_async_copy(yt_rows(sl, r), y_ref.at[pl.ds(0, SUBLANES)], sem_out)

    def ffn_step():
        xb = xb_ref[...]
        a = jax.nn.silu(_dot(xb, wg_ref[0])) * _dot(xb, wu_ref[0])
        return _dot(a.astype(BF16), wd_ref[0])

    @pl.when(jnp.logical_and(j == 0, step == 0))
    def _():
        yt_ref[...] = jnp.zeros_like(yt_ref)
        start_gathers(0)

    @pl.when(jnp.logical_and(step == 0, jnp.logical_or(j == 0, prev_valid)))
    def _():
        for r in range(tm):
            gather_wait(r).wait()

    @pl.when(jnp.logical_and(valid, step == 0))
    def _():
        xb_ref[...] = _tiles_to_rows(x_ref, tm).astype(BF16)
        start_gathers(j + 1)
        acc_ref[...] = ffn_step()

    @pl.when(jnp.logical_and(step == 1, jnp.logical_or(j == 1, prev2_valid)))
    def _():
        for r in range(tm):
            scatter_wait(slot, r).wait()

    @pl.when(jnp.logical_and(valid, step == 1))
    def _():
        start_scatters(j - 1, 1 - slot)
        _rows_to_tiles(yt_ref.at[slot], acc_ref[...] + ffn_step())

    @pl.when(jnp.logical_and(jnp.logical_and(jnp.logical_not(valid), prev_valid), step == 1))
    def _():
        start_scatters(j - 1, 1 - slot)


def _expert_ffn(tile_e, tile_valid, src, dst, hf, wg, wu, wd):
    n_tok = hf.shape[0] // SUBLANES
    d = D_MODEL
    n_tiles = tile_e.shape[0]
    d_ff = wg.shape[2]
    nf = d_ff // TF_EXP
    assert nf == 2
    j = jnp.arange(n_tiles, dtype=jnp.int32)
    first = j % 2
    n_valid = jnp.sum(tile_valid)
    last_f = 1 - (n_valid - 1) % 2
    half = jnp.where(tile_valid[:, None] > 0, jnp.stack([first, 1 - first], axis=1), last_f)
    tile_f = half.reshape(-1).astype(jnp.int32)

    return pl.pallas_call(
        _expert_kernel,
        grid_spec=pltpu.PrefetchScalarGridSpec(
            num_scalar_prefetch=5,
            grid=(n_tiles, nf),
            in_specs=[
                pl.BlockSpec(memory_space=pl.ANY),
                pl.BlockSpec((1, d, TF_EXP), lambda j, s, te, tv, tf, *_: (te[j], 0, tf[2 * j + s])),
                pl.BlockSpec((1, d, TF_EXP), lambda j, s, te, tv, tf, *_: (te[j], 0, tf[2 * j + s])),
                pl.BlockSpec((1, TF_EXP, d), lambda j, s, te, tv, tf, *_: (te[j], tf[2 * j + s], 0)),
            ],
            out_specs=pl.BlockSpec(memory_space=pl.ANY),
            scratch_shapes=[pltpu.VMEM((TM_EXP * SUBLANES, LANES), F32),
                            pltpu.VMEM((TM_EXP, d), BF16),
                            pltpu.VMEM((TM_EXP, d), F32),
                            pltpu.VMEM((2, TM_EXP * SUBLANES, LANES), F32),
                            pltpu.SemaphoreType.DMA(()), pltpu.SemaphoreType.DMA(())],
        ),
        out_shape=jax.ShapeDtypeStruct(((TOP_K * n_tok + TM_EXP) * SUBLANES, LANES), F32),
        compiler_params=_params(2),
        name="moe_experts",
    )(tile_e, tile_valid, tile_f, src, dst, hf, wg, wu, wd)


def _combine_kernel(y0_ref, y1_ref, h_ref, gate_ref, gf_ref, o_ref):
    gates = gate_ref[...]
    n = h_ref.shape[0]
    mix = gates[:, 0:1] * _tiles_to_rows(y0_ref, n) + gates[:, 1:2] * _tiles_to_rows(y1_ref, n)
    o_ref[...] = h_ref[...] + gf_ref[0] * mix


def _combine_rows(y, h, gates, mods, seq):
    t, d = h.shape
    steps_per_batch = seq // TG
    second = t // TG
    tok = pl.BlockSpec((TG, d), lambda i: (i, 0))
    return pl.pallas_call(
        _combine_kernel,
        grid=(t // TG,),
        in_specs=[
            pl.BlockSpec((TG * SUBLANES, LANES), lambda i: (i, 0)),
            pl.BlockSpec((TG * SUBLANES, LANES), lambda i: (second + i, 0)),
            tok,
            pl.BlockSpec((TG, TOP_K), lambda i: (i, 0)),
            pl.BlockSpec((1, 1, d), lambda i: (COND_ROWS + i // steps_per_batch, 0, N_MODS - 1)),
        ],
        out_specs=tok,
        out_shape=jax.ShapeDtypeStruct((t, d), F32),
        compiler_params=_params(1),
        name="moe_combine",
    )(y, y, h, gates, mods)


def _rope_tables(seq):
    rows = seq // GRID_W
    row = jnp.repeat(jnp.arange(rows), GRID_W).astype(F32)
    col = jnp.tile(jnp.arange(GRID_W), rows).astype(F32)
    quarter = ATTN_DIM // 4
    inv = ROPE_BASE ** (-jnp.arange(quarter, dtype=F32) / quarter)
    ar = row[:, None] * inv
    ac = col[:, None] * inv
    ang = jnp.concatenate([ar, ar, ac, ac], axis=-1)
    sign = jnp.tile(jnp.concatenate([-jnp.ones(quarter, F32), jnp.ones(quarter, F32)]), 2)
    reps = QK_W // ATTN_DIM
    return jnp.tile(jnp.cos(ang), (1, reps)), jnp.tile(jnp.sin(ang) * sign, (1, reps))


def kernel(x, c, ctx, c_ctx, ada_w, ada_b, norm_mix_g, norm_ffn_g, w_in_even, q_norm_g, k_norm_g,
           lam_q1, lam_k1, lam_q2, lam_k2, subln_g, conv_w, w_out_even, ffn_w_gate, ffn_w_up,
           ffn_w_down, sgu_w_in, sgu_ln_g, sgu_ln_b, sgu_w_s, sgu_b_s, sgu_w_out, router_w,
           moe_w_gate, moe_w_up, moe_w_down):
    b, s, d = x.shape
    assert d == D_MODEL and b < COND_ROWS and ada_w.shape[0] == 2
    assert s % TM_PROJ == 0 and s % TQ == 0 and s % TM_FFN == 0 and s % TM_SGU == 0 and s % TG == 0

    cond = jnp.zeros((COND_ROWS, d), F32).at[:b].set(c).at[b].set(c_ctx)
    mods = _ada_mods(cond, ada_w, ada_b)

    lambda_init = 0.8 - 0.6 * math.exp(-0.3 * 0)
    group = jnp.arange(QK_W) // ATTN_DIM
    bd = (group[:, None] == group[None, :]).astype(BF16)
    reps = QK_W // ATTN_DIM
    qg = jnp.tile(q_norm_g[0], reps)[None, :]
    kg = jnp.tile(k_norm_g[0], reps)[None, :]
    cos, sin = _rope_tables(s)
    w_in = w_in_even[0].astype(BF16)
    g_mix0 = norm_mix_g[0][None, :]
    n_e, _, d_ffe = moe_w_gate[0].shape
    q, k, v, yconv, moe_wg = _inproj(x, mods, g_mix0, w_in, bd, qg, kg, cos, sin, conv_w[0],
                                     moe_w_gate[0].reshape(n_e * d, d_ffe))
    kc, vc = _ctxproj(ctx, mods, g_mix0, w_in, bd, kg, ctx_row=b)
    attn, w_out0, ffn_wg, ffn_wu = _attention(
        q, k, kc, v, vc, subln_g[0][None, :], lam_q1[0][None, :], lam_k1[0][None, :],
        lam_q2[0][None, :], lam_k2[0][None, :], lambda_init,
        [w_out_even[0], ffn_w_gate[0], ffn_w_up[0]])
    h, moe_wd, sgu_win, sgu_wout = _outproj_ffn(
        x, attn, yconv, mods, norm_ffn_g[0][None, :], w_out0, ffn_wg, ffn_wu,
        ffn_w_down[0].astype(BF16),
        [moe_w_down[0].reshape(n_e * d_ffe, d), sgu_w_in[0], sgu_w_out[0]])

    gw = d // SGU_GROUPS
    bias = jnp.repeat(sgu_b_s[0].T, gw, axis=1)
    rw_hi = router_w[0].astype(BF16)
    rw_lo = (router_w[0] - rw_hi.astype(F32)).astype(BF16)
    h, hf, logits, moe_wu = _sgu(h, mods, norm_mix_g[1][None, :], norm_ffn_g[1][None, :],
                                 sgu_win, sgu_ln_g[0][None, :],
                                 sgu_ln_b[0][None, :], sgu_w_s[0].astype(BF16), bias,
                                 sgu_wout, rw_hi, rw_lo,
                                 moe_w_up[0].reshape(n_e * d, d_ffe))

    t = b * s
    n_tiles = (t * TOP_K) // TM_EXP + N_EXPERTS + 2
    pos, gates, tile_e, tile_valid, pad_lo, pad_hi = _route(
        logits.reshape(t, N_EXPERTS), TM_EXP, n_tiles)
    src, dst = _row_tables(pos, pad_lo, pad_hi, t, n_tiles)
    y = _expert_ffn(tile_e, tile_valid, src, dst, hf, moe_wg.reshape(n_e, d, d_ffe),
                    moe_wu.reshape(n_e, d, d_ffe), moe_wd.reshape(n_e, d_ffe, d))
    out = _combine_rows(y, h.reshape(t, d), gates, mods, s)
    return out.reshape(b, s, d)
```

```python
import functools
import math

import jax
import jax.numpy as jnp
import numpy as np
from jax import lax
from jax.experimental import pallas as pl
from jax.experimental.pallas import tpu as pltpu

F32 = jnp.float32
BF16 = jnp.bfloat16

D_MODEL = 1024
GRID_W = 64
ATTN_HEADS = 4
ATTN_DIM = 64
ATTN_VDIM = 2 * ATTN_DIM
QK_W = ATTN_HEADS * 2 * ATTN_DIM
CONV_K = 3
ROPE_BASE = 10000.0
SGU_GROUPS = 4
CHUNK = 128
N_EXPERTS = 8
TOP_K = 2
EPS = 1e-6
LOG2_E = math.log2(math.e)
N_MODS = 6
COND_ROWS = 16

V7X_VMEM_LIMIT = 56 * 1024 * 1024
MXU_N = 256
TM_PROJ = 512
HALO = 16
TQ = 2048
ATTN_RB = 256
ATTN_SLOTS = 8
ATTN_KC = 256
TM_FFN = 512
TM_SGU = 1024
SGU_RB = 256
TM_EXP = 512
TF_EXP = 1792
TG = 512
ROW_DMA_PRIORITY = 1


def _params(n_axes):
    return pltpu.CompilerParams(
        dimension_semantics=("arbitrary",) * n_axes, vmem_limit_bytes=V7X_VMEM_LIMIT)


def _const_spec(shape):
    zeros = (0,) * len(shape)
    return pl.BlockSpec(shape, lambda *_: zeros, pipeline_mode=pl.Buffered(1))


def _dot(a, b):
    return jnp.dot(a, b, preferred_element_type=F32)


def _modulate(x, g, scale, shift):
    ms = jnp.mean(x * x, axis=-1, keepdims=True)
    y = x * lax.rsqrt(ms + EPS)
    return (y * g) * (1.0 + scale) + shift


def _split_bf16(x):
    hi = x.astype(BF16)
    lo = (x - hi.astype(F32)).astype(BF16)
    return hi, lo


assert D_MODEL == 8 * 128
SUBLANES = 8
LANES = 128


def _rows_to_tiles(ref, val, row0=0):
    n = val.shape[0]
    for c in range(SUBLANES):
        ref[pl.ds(row0 * SUBLANES + c, n, stride=SUBLANES), :] = val[:, c * LANES:(c + 1) * LANES]


def _tiles_to_rows(ref, n):
    return jnp.concatenate(
        [ref[pl.ds(c, n, stride=SUBLANES), :] for c in range(SUBLANES)], axis=1)


def _ada_kernel(cond_ref, w_ref, b_ref, o_ref):
    a = jax.nn.silu(cond_ref[...]).astype(BF16)
    o_ref[0] = _dot(a, w_ref[0].astype(BF16)) + b_ref[0]


def _ada_mods(cond, ada_w, ada_b):
    depth = ada_w.shape[0]
    d = D_MODEL
    out = pl.pallas_call(
        _ada_kernel,
        grid=(depth, N_MODS),
        in_specs=[
            pl.BlockSpec((COND_ROWS, d), lambda l, n: (0, 0)),
            pl.BlockSpec((1, d, d), lambda l, n: (l, 0, n)),
            pl.BlockSpec((1, 1, d), lambda l, n: (l, 0, n)),
        ],
        out_specs=pl.BlockSpec((1, COND_ROWS, d), lambda l, n: (l, 0, n)),
        out_shape=jax.ShapeDtypeStruct((depth, COND_ROWS, N_MODS * d), F32),
        compiler_params=_params(2),
        name="ada_mods",
    )(cond, ada_w, ada_b.reshape(depth, 1, N_MODS * d))
    return out.reshape(depth * COND_ROWS, 1, N_MODS * d)


def _mod_spec(layer, chunk, row_of):
    return pl.BlockSpec((1, 1, D_MODEL),
                        lambda *ids: (layer * COND_ROWS + row_of(*ids), 0, chunk))


def _group_rms(t, bd, gain):
    hi, lo = _split_bf16(t * t)
    ss = _dot(hi, bd) + _dot(lo, bd)
    return (t * lax.rsqrt(ss * (1.0 / ATTN_DIM) + EPS)) * gain


def _rope(t, cos, sin_signed, first_half):
    width = t.shape[-1]
    quarter = ATTN_DIM // 4
    ahead = pltpu.roll(t, width - quarter, axis=1)
    behind = pltpu.roll(t, quarter, axis=1)
    return t * cos + jnp.where(first_half, ahead, behind) * sin_signed


def _with_ones_columns(v):
    rows = v.shape[0]
    lane = lax.broadcasted_iota(jnp.int32, (rows, ATTN_VDIM), 1)
    ones_col = jnp.where(lane == 0, 1.0, 0.0).astype(v.dtype)
    parts = []
    for h in range(ATTN_HEADS):
        parts += [v[:, h * ATTN_VDIM:(h + 1) * ATTN_VDIM], ones_col]
    return jnp.concatenate(parts, axis=1)


def _side_cast(w, steps, index_of):
    rows, cols = w.shape
    assert rows % (steps * 16) == 0
    spec = pl.BlockSpec((rows // steps, cols), lambda *ids: (index_of(*ids), 0))
    return spec, jax.ShapeDtypeStruct(w.shape, BF16)


def _inproj_kernel(x_ref, xp_ref, xn_ref, sc_ref, sh_ref, g_ref, w_ref, bd_ref, qg_ref, kg_ref,
                   cos_ref, sin_ref, cw_ref, side_ref, q_ref, k_ref, v_ref, yc_ref, side_out_ref):
    side_out_ref[...] = side_ref[...].astype(BF16)
    i = pl.program_id(1)
    last = pl.num_programs(1) - 1
    tm = x_ref.shape[1]
    w = QK_W
    xe = jnp.concatenate([xp_ref[0], x_ref[0], xn_ref[0]], axis=0)
    he = _modulate(xe, g_ref[...], sc_ref[0], sh_ref[0]).astype(BF16)
    hm = he[HALO:HALO + tm]

    qkv = _dot(hm, w_ref[:, :3 * w])
    bd = bd_ref[...]
    cos = cos_ref[...]
    sin = sin_ref[...]
    lane = lax.broadcasted_iota(jnp.int32, (1, w), 1)
    first_half = (lane % (ATTN_DIM // 2)) < (ATTN_DIM // 4)
    q = _rope(_group_rms(qkv[:, :w], bd, qg_ref[...]), cos, sin, first_half)
    k = _rope(_group_rms(qkv[:, w:2 * w], bd, kg_ref[...]), cos, sin, first_half)
    q_ref[0] = (q * (ATTN_DIM ** -0.5 * LOG2_E)).astype(BF16)
    k_ref[0] = k.astype(BF16)
    v_ref[0] = _with_ones_columns(qkv[:, 2 * w:].astype(BF16))

    gcu = _dot(he, w_ref[:, 4 * w:])
    cu = gcu[:, :w] * gcu[:, w:]
    row = lax.broadcasted_iota(jnp.int32, (tm + 2 * HALO, 1), 0)
    first_row = jnp.where(i == 0, HALO, 0)
    end_row = jnp.where(i == last, HALO + tm, tm + 2 * HALO)
    cu = jnp.where((row >= first_row) & (row < end_row), cu, 0.0)
    prev = pltpu.roll(cu, 1, axis=0)[HALO:HALO + tm]
    nxt = pltpu.roll(cu, tm + 2 * HALO - 1, axis=0)[HALO:HALO + tm]
    cw = cw_ref[...]
    conv = prev * cw[0:1] + cu[HALO:HALO + tm] * cw[1:2] + nxt * cw[2:3]
    gate_b = _dot(hm, w_ref[:, 3 * w:4 * w])
    yc_ref[0] = (gate_b * conv).astype(BF16)


def _inproj(x, mods, g, w_in, bd, qg, kg, cos, sin, conv_w, side_w):
    b, s, d = x.shape
    tm = TM_PROJ
    nh = tm // HALO
    n_halo_blocks = s // HALO
    tok = pl.BlockSpec((1, tm, QK_W), lambda bi, i: (bi, i, 0))
    out_sds = jax.ShapeDtypeStruct((b, s, QK_W), BF16)
    side_spec, side_sds = _side_cast(side_w, b * (s // tm), lambda bi, i: bi * (s // tm) + i)
    return pl.pallas_call(
        _inproj_kernel,
        grid=(b, s // tm),
        in_specs=[
            pl.BlockSpec((1, tm, d), lambda bi, i: (bi, i, 0)),
            pl.BlockSpec((1, HALO, d), lambda bi, i: (bi, jnp.maximum(i * nh - 1, 0), 0)),
            pl.BlockSpec((1, HALO, d),
                         lambda bi, i: (bi, jnp.minimum((i + 1) * nh, n_halo_blocks - 1), 0)),
            _mod_spec(0, 1, lambda bi, i: bi),
            _mod_spec(0, 0, lambda bi, i: bi),
            _const_spec((1, d)),
            _const_spec(w_in.shape),
            _const_spec(bd.shape),
            _const_spec((1, QK_W)),
            _const_spec((1, QK_W)),
            pl.BlockSpec((tm, QK_W), lambda bi, i: (i, 0)),
            pl.BlockSpec((tm, QK_W), lambda bi, i: (i, 0)),
            _const_spec(conv_w.shape),
            side_spec,
        ],
        out_specs=[tok, tok, pl.BlockSpec((1, tm, 2 * QK_W), lambda bi, i: (bi, i, 0)), tok,
                   side_spec],
        out_shape=[out_sds, out_sds, jax.ShapeDtypeStruct((b, s, 2 * QK_W), BF16), out_sds,
                   side_sds],
        compiler_params=_params(2),
        name="inproj",
    )(x, x, x, mods, mods, g, w_in, bd, qg, kg, cos, sin, conv_w, side_w)


def _ctxproj_kernel(x_ref, sc_ref, sh_ref, g_ref, wk_ref, wv_ref, bd_ref, kg_ref, k_ref, v_ref):
    hc = _modulate(x_ref[0], g_ref[...], sc_ref[0], sh_ref[0]).astype(BF16)
    k_ref[0] = _group_rms(_dot(hc, wk_ref[...]), bd_ref[...], kg_ref[...]).astype(BF16)
    v_ref[0] = _with_ones_columns(_dot(hc, wv_ref[...]).astype(BF16))


def _ctxproj(ctx, mods, g, w_in, bd, kg, ctx_row):
    b, l, d = ctx.shape
    out = pl.BlockSpec((1, l, QK_W), lambda bi: (bi, 0, 0))
    out_sds = jax.ShapeDtypeStruct((b, l, QK_W), BF16)
    return pl.pallas_call(
        _ctxproj_kernel,
        grid=(b,),
        in_specs=[
            pl.BlockSpec((1, l, d), lambda bi: (bi, 0, 0)),
            _mod_spec(0, 1, lambda bi: ctx_row),
            _mod_spec(0, 0, lambda bi: ctx_row),
            _const_spec((1, d)),
            pl.BlockSpec((d, QK_W), lambda bi: (0, 1), pipeline_mode=pl.Buffered(1)),
            pl.BlockSpec((d, QK_W), lambda bi: (0, 2), pipeline_mode=pl.Buffered(1)),
            _const_spec(bd.shape),
            _const_spec((1, QK_W)),
        ],
        out_specs=[out, pl.BlockSpec((1, l, 2 * QK_W), lambda bi: (bi, 0, 0))],
        out_shape=[out_sds, jax.ShapeDtypeStruct((b, l, 2 * QK_W), BF16)],
        compiler_params=_params(1),
        name="ctxproj",
    )(ctx, mods, mods, g, w_in, w_in, bd, kg)


def _attn_kernel(q_ref, k_ref, kc_ref, v_ref, vc_ref, sg_ref, lq1_ref, lk1_ref, lq2_ref, lk2_ref,
                 *rest, lambda_init, n_side):
    side_refs, o_ref = rest[:n_side], rest[n_side]
    side_out_refs = rest[n_side + 1:2 * n_side + 1]
    s_ref, e_ref = rest[2 * n_side + 1:]
    for side_ref, side_out_ref in zip(side_refs, side_out_refs):
        side_out_ref[...] = side_ref[...].astype(BF16)
    tq = q_ref.shape[1]
    n_ctx = kc_ref.shape[1] // ATTN_KC
    n_lat = k_ref.shape[1] // ATTN_KC
    chunks = [(kc_ref, vc_ref, c) for c in range(n_ctx)] + [(k_ref, v_ref, c) for c in range(n_lat)]
    lane = lax.broadcasted_iota(jnp.int32, (1, ATTN_VDIM), 1)
    first = lane < ATTN_DIM
    lam = (jnp.exp(jnp.sum(lq1_ref[...] * lk1_ref[...], keepdims=True))
           - jnp.exp(jnp.sum(lq2_ref[...] * lk2_ref[...], keepdims=True)) + lambda_init)
    nt = (((1,), (1,)), ((), ()))

    for rb in range(tq // ATTN_RB):
        rows = slice(rb * ATTN_RB, (rb + 1) * ATTN_RB)
        q = q_ref[0, rows, :]
        zero = jnp.zeros_like(q)
        outs = []
        for part in range(2):
            qx = jnp.where(first, q, zero) if part == 0 else jnp.where(first, zero, q)
            slot = (rb * 2 + part) % ATTN_SLOTS
            mpart = None
            for ci, (kr, _, c) in enumerate(chunks):
                sc = lax.dot_general(qx, kr[0, c * ATTN_KC:(c + 1) * ATTN_KC, :], nt,
                                     preferred_element_type=F32)
                s_ref[slot, ci] = sc
                cm = sc[:, :ATTN_VDIM]
                for j in range(1, ATTN_KC // ATTN_VDIM):
                    cm = jnp.maximum(cm, sc[:, j * ATTN_VDIM:(j + 1) * ATTN_VDIM])
                mpart = cm if mpart is None else jnp.maximum(mpart, cm)
            m = jnp.max(mpart, axis=-1, keepdims=True)
            for ci in range(len(chunks)):
                e_ref[slot, :, ci * ATTN_KC:(ci + 1) * ATTN_KC] = (
                    jnp.exp2(s_ref[slot, ci] - m).astype(BF16))
            n_c = n_ctx * ATTN_KC
            acc = _dot(e_ref[slot, :, :n_c], vc_ref[0]) + _dot(e_ref[slot, :, n_c:], v_ref[0])
            outs.append(acc[:, :ATTN_VDIM] * (1.0 / acc[:, ATTN_VDIM:ATTN_VDIM + 1]))
        o = outs[0] - lam * outs[1]
        ms = jnp.mean(o * o, axis=-1, keepdims=True)
        y = (o * lax.rsqrt(ms + EPS)) * sg_ref[...]
        o_ref[0, rows, :] = (y * (1.0 - lambda_init)).astype(BF16)


def _attention(q, k, kc, v, vc, subln_g, lq1, lk1, lq2, lk2, lambda_init, side_ws):
    b, s, _ = q.shape
    l = kc.shape[1]
    hd = ATTN_VDIM
    va = 2 * ATTN_VDIM
    assert l % ATTN_KC == 0 and s % ATTN_KC == 0 and s == TQ
    n_chunks = (l + s) // ATTN_KC
    vec = _const_spec((1, ATTN_DIM))
    sides = [_side_cast(w, b * ATTN_HEADS, lambda bi, h, i: bi * ATTN_HEADS + h) for w in side_ws]
    return pl.pallas_call(
        functools.partial(_attn_kernel, lambda_init=lambda_init, n_side=len(side_ws)),
        grid=(b, ATTN_HEADS, s // TQ),
        in_specs=[
            pl.BlockSpec((1, TQ, hd), lambda bi, h, i: (bi, i, h)),
            pl.BlockSpec((1, s, hd), lambda bi, h, i: (bi, 0, h)),
            pl.BlockSpec((1, l, hd), lambda bi, h, i: (bi, 0, h)),
            pl.BlockSpec((1, s, va), lambda bi, h, i: (bi, 0, h)),
            pl.BlockSpec((1, l, va), lambda bi, h, i: (bi, 0, h)),
            _const_spec((1, hd)),
            vec, vec, vec, vec,
        ] + [spec for spec, _ in sides],
        out_specs=[pl.BlockSpec((1, TQ, hd), lambda bi, h, i: (bi, i, h))]
        + [spec for spec, _ in sides],
        out_shape=[jax.ShapeDtypeStruct((b, s, ATTN_HEADS * hd), BF16)] + [sds for _, sds in sides],
        scratch_shapes=[pltpu.VMEM((ATTN_SLOTS, n_chunks, ATTN_RB, ATTN_KC), F32),
                        pltpu.VMEM((ATTN_SLOTS, ATTN_RB, l + s), BF16)],
        compiler_params=_params(3),
        name="diff_attention",
    )(q, k, kc, v, vc, subln_g, lq1, lk1, lq2, lk2, *side_ws)


def _outproj_ffn_kernel(x_ref, at_ref, yc_ref, gm_ref, sf_ref, cf_ref, gf_ref, g_ref,
                        wo_ref, wg_ref, wu_ref, wd_ref, *rest):
    n_side = (len(rest) - 2) // 2
    o_ref, a_ref = rest[n_side], rest[-1]
    for side_ref, side_out_ref in zip(rest[:n_side], rest[n_side + 1:-1]):
        side_out_ref[...] = side_ref[...].astype(BF16)
    half = wo_ref.shape[0] // 2
    d_ff = wg_ref.shape[1]
    y = _dot(at_ref[0], wo_ref[:half]) + _dot(yc_ref[0], wo_ref[half:])
    h1 = x_ref[0] + gm_ref[0] * y
    hf = _modulate(h1, g_ref[...], cf_ref[0], sf_ref[0]).astype(BF16)
    for f in range(d_ff // MXU_N):
        cols = slice(f * MXU_N, (f + 1) * MXU_N)
        a = jax.nn.silu(_dot(hf, wg_ref[:, cols])) * _dot(hf, wu_ref[:, cols])
        a_ref[:, cols] = a.astype(BF16)
    o_ref[0] = h1 + gf_ref[0] * _dot(a_ref[...], wd_ref[...])


def _outproj_ffn(x, attn, yconv, mods, g, w_out, wg, wu, wd, side_ws):
    b, s, d = x.shape
    tm = TM_FFN
    sides = [_side_cast(w, b * (s // tm), lambda bi, i: bi * (s // tm) + i) for w in side_ws]
    return pl.pallas_call(
        _outproj_ffn_kernel,
        grid=(b, s // tm),
        in_specs=[
            pl.BlockSpec((1, tm, d), lambda bi, i: (bi, i, 0)),
            pl.BlockSpec((1, tm, QK_W), lambda bi, i: (bi, i, 0)),
            pl.BlockSpec((1, tm, QK_W), lambda bi, i: (bi, i, 0)),
            _mod_spec(0, 2, lambda bi, i: bi),
            _mod_spec(0, 3, lambda bi, i: bi),
            _mod_spec(0, 4, lambda bi, i: bi),
            _mod_spec(0, 5, lambda bi, i: bi),
            _const_spec((1, d)),
            _const_spec(w_out.shape),
            _const_spec(wg.shape),
            _const_spec(wu.shape),
            _const_spec(wd.shape),
        ] + [spec for spec, _ in sides],
        out_specs=[pl.BlockSpec((1, tm, d), lambda bi, i: (bi, i, 0))] + [spec for spec, _ in sides],
        out_shape=[jax.ShapeDtypeStruct((b, s, d), F32)] + [sds for _, sds in sides],
        scratch_shapes=[pltpu.VMEM((tm, wg.shape[1]), BF16)],
        compiler_params=_params(2),
        name="outproj_ffn",
    )(x, attn, yconv, mods, mods, mods, mods, g, w_out, wg, wu, wd, *side_ws)


def _sgu_kernel(x_ref, sm_ref, cm_ref, gm_ref, sf_ref, cf_ref, g_ref, gff_ref, win_ref, lng_ref,
                lnb_ref, ws_ref, bs_ref, wo_ref, rwh_ref, rwl_ref, side_ref,
                h_ref, hf_ref, lg_ref, side_out_ref, us_ref):
    side_out_ref[...] = side_ref[...].astype(BF16)
    tm, d = x_ref.shape[1:]
    gw = d // SGU_GROUPS
    for rb in range(tm // SGU_RB):
        r0 = rb * SGU_RB
        x = x_ref[0, r0:r0 + SGU_RB, :]
        hm = _modulate(x, g_ref[...], cm_ref[0], sm_ref[0]).astype(BF16)
        z = jax.nn.gelu(_dot(hm, win_ref[...]))
        v = z[:, d:]
        mu = jnp.mean(v, axis=-1, keepdims=True)
        vc = v - mu
        var = jnp.mean(vc * vc, axis=-1, keepdims=True)
        vn = ((vc * lax.rsqrt(var + EPS)) * lng_ref[...] + lnb_ref[...]).astype(BF16)
        for n in range(SGU_RB // CHUNK):
            rows = slice(n * CHUNK, (n + 1) * CHUNK)
            for gi in range(SGU_GROUPS):
                cols = slice(gi * gw, (gi + 1) * gw)
                s = _dot(ws_ref[gi], vn[rows, cols]) + bs_ref[:, cols]
                us_ref[r0 + n * CHUNK:r0 + (n + 1) * CHUNK, cols] = (z[rows, cols] * s).astype(BF16)
        h = x + gm_ref[0] * _dot(us_ref[r0:r0 + SGU_RB, :], wo_ref[...])
        h_ref[0, r0:r0 + SGU_RB, :] = h
        hf = _modulate(h, gff_ref[...], cf_ref[0], sf_ref[0])
        _rows_to_tiles(hf_ref, hf, row0=r0)
        hi, lo = _split_bf16(hf)
        lg_ref[0, r0:r0 + SGU_RB, :] = (
            _dot(hi, rwh_ref[...]) + (_dot(hi, rwl_ref[...]) + _dot(lo, rwh_ref[...])))


def _sgu(h, mods, g_mix, g_ffn, w_in, ln_g, ln_b, w_s, bias, w_out, rw_hi, rw_lo, side_w):
    b, s, d = h.shape
    tm = TM_SGU
    tok = pl.BlockSpec((1, tm, d), lambda bi, i: (bi, i, 0))
    side_spec, side_sds = _side_cast(side_w, b * (s // tm), lambda bi, i: bi * (s // tm) + i)
    return pl.pallas_call(
        _sgu_kernel,
        grid=(b, s // tm),
        in_specs=[
            tok,
            _mod_spec(1, 0, lambda bi, i: bi),
            _mod_spec(1, 1, lambda bi, i: bi),
            _mod_spec(1, 2, lambda bi, i: bi),
            _mod_spec(1, 3, lambda bi, i: bi),
            _mod_spec(1, 4, lambda bi, i: bi),
            _const_spec((1, d)),
            _const_spec((1, d)),
            _const_spec(w_in.shape),
            _const_spec((1, d)),
            _const_spec((1, d)),
            _const_spec(w_s.shape),
            _const_spec(bias.shape),
            _const_spec(w_out.shape),
            _const_spec(rw_hi.shape),
            _const_spec(rw_lo.shape),
            side_spec,
        ],
        out_specs=[tok,
                   pl.BlockSpec((tm * SUBLANES, LANES), lambda bi, i: (bi * (s // tm) + i, 0)),
                   pl.BlockSpec((1, tm, N_EXPERTS), lambda bi, i: (bi, i, 0)),
                   side_spec],
        out_shape=[jax.ShapeDtypeStruct((b, s, d), F32),
                   jax.ShapeDtypeStruct((b * s * SUBLANES, LANES), F32),
                   jax.ShapeDtypeStruct((b, s, N_EXPERTS), F32),
                   side_sds],
        scratch_shapes=[pltpu.VMEM((tm, d), BF16)],
        compiler_params=_params(2),
        name="sgu",
    )(h, mods, mods, mods, mods, mods, g_mix, g_ffn, w_in, ln_g, ln_b, w_s, bias, w_out,
      rw_hi, rw_lo, side_w)


def _route(logits, tile_rows, n_tiles):
    lt = logits.T
    t = lt.shape[1]
    eid = lax.broadcasted_iota(jnp.int32, (N_EXPERTS, t), 0)
    i1 = jnp.argmax(lt, axis=0).astype(jnp.int32)
    v1 = jnp.max(lt, axis=0)
    first = eid == i1[None, :]
    rest = jnp.where(first, -jnp.inf, lt)
    i2 = jnp.argmax(rest, axis=0).astype(jnp.int32)
    v2 = jnp.max(rest, axis=0)
    second = eid == i2[None, :]
    e2 = jnp.exp(v2 - v1)
    gates = jnp.stack([1.0 / (1.0 + e2), e2 / (1.0 + e2)], axis=-1)
    onehot = (first | second).astype(jnp.int32)
    csum = jnp.cumsum(onehot, axis=1)
    counts = csum[:, -1]
    padded = ((counts + tile_rows - 1) // tile_rows) * tile_rows
    ends = jnp.cumsum(padded)
    starts = ends - padded
    slot = starts[:, None] + csum - onehot
    pos = jnp.concatenate([jnp.sum(jnp.where(first, slot, 0), axis=0),
                           jnp.sum(jnp.where(second, slot, 0), axis=0)]).astype(jnp.int32)
    tile_start = jnp.arange(n_tiles, dtype=jnp.int32) * tile_rows
    valid = tile_start < ends[-1]
    tile_e = jnp.sum(tile_start[:, None] >= ends[None, :], axis=-1).astype(jnp.int32)
    last_e = jnp.max(jnp.where(counts > 0, jnp.arange(N_EXPERTS, dtype=jnp.int32), 0))
    tile_e = jnp.where(valid, tile_e, last_e)
    total = jnp.full((1,), n_tiles * tile_rows, jnp.int32)
    pad_lo = jnp.concatenate([starts + counts, ends[-1:]]).astype(jnp.int32)
    pad_hi = jnp.concatenate([ends, total]).astype(jnp.int32)
    return pos, gates, tile_e, valid.astype(jnp.int32), pad_lo, pad_hi


INVERT_UNROLL = 32
FILL_UNROLL = 8


def _invert_kernel(pos_ref, lo_ref, hi_ref, pair_ref):
    n_pairs = pos_ref.shape[0]
    assert n_pairs % INVERT_UNROLL == 0

    def fill(blk, lo):
        for u in range(FILL_UNROLL):
            pair_ref[lo + blk * FILL_UNROLL + u] = -1
        return lo

    def place(blk, c):
        for u in range(INVERT_UNROLL):
            i = blk * INVERT_UNROLL + u
            pair_ref[pos_ref[i]] = i
        return c

    for g in range(lo_ref.shape[0]):
        n_blocks = (hi_ref[g] - lo_ref[g] + FILL_UNROLL - 1) // FILL_UNROLL
        lax.fori_loop(0, n_blocks, fill, lo_ref[g])
    lax.fori_loop(0, n_pairs // INVERT_UNROLL, place, 0)


def _row_tables(pos, pad_lo, pad_hi, n_tokens, n_tiles):
    n_rows = n_tiles * TM_EXP
    smem = pl.BlockSpec(memory_space=pltpu.SMEM)
    pair = pl.pallas_call(
        _invert_kernel,
        in_specs=[smem, smem, smem],
        out_specs=smem,
        out_shape=jax.ShapeDtypeStruct((n_rows,), jnp.int32),
        name="moe_invert",
    )(pos, pad_lo, pad_hi)
    scratch_row = n_tokens * TOP_K + jnp.arange(n_rows, dtype=jnp.int32) % TM_EXP
    src = jnp.where(pair < 0, 0, jnp.where(pair >= n_tokens, pair - n_tokens, pair))
    dst = jnp.where(pair < 0, scratch_row, pair)
    return src * SUBLANES, jnp.concatenate([scratch_row[:TM_EXP], dst]) * SUBLANES


def _expert_kernel(te_ref, tv_ref, tf_ref, src_ref, dst_ref, hf_ref, wg_ref, wu_ref, wd_ref, y_ref,
                   x_ref, xb_ref, acc_ref, yt_ref, sem_in, sem_out):
    del te_ref, tf_ref
    tm = TM_EXP
    j = pl.program_id(0)
    step = pl.program_id(1)
    valid = tv_ref[j] > 0
    prev_valid = jnp.logical_and(j > 0, tv_ref[jnp.maximum(j - 1, 0)] > 0)
    prev2_valid = jnp.logical_and(j > 1, tv_ref[jnp.maximum(j - 2, 0)] > 0)
    slot = j % 2
    x_rows = lambda r: x_ref.at[pl.ds(r * SUBLANES, SUBLANES)]
    yt_rows = lambda sl, r: yt_ref.at[sl, pl.ds(r * SUBLANES, SUBLANES)]

    def start_gathers(tile):
        for r in range(tm):
            t8 = src_ref[tile * tm + r]
            pltpu.make_async_copy(hf_ref.at[pl.ds(pl.multiple_of(t8, SUBLANES), SUBLANES)],
                                  x_rows(r), sem_in).start(priority=ROW_DMA_PRIORITY)

    def gather_wait(r):
        return pltpu.make_async_copy(hf_ref.at[pl.ds(0, SUBLANES)], x_rows(r), sem_in)

    def start_scatters(tile, sl):
        for r in range(tm):
            d8 = dst_ref[(tile + 1) * tm + r]
            pltpu.make_async_copy(yt_rows(sl, r),
                                  y_ref.at[pl.ds(pl.multiple_of(d8, SUBLANES), SUBLANES)],
                                  sem_out).start(priority=ROW_DMA_PRIORITY)

    def scatter_wait(sl, r):
        return pltpu.make_async_copy(yt_rows(sl, r), y_ref.at[pl.ds(0, SUBLANES)], sem_out)

    def ffn_step():
        xb = xb_ref[...]
        a = jax.nn.silu(_dot(xb, wg_ref[0])) * _dot(xb, wu_ref[0])
        return _dot(a.astype(BF16), wd_ref[0])

    @pl.when(jnp.logical_and(j == 0, step == 0))
    def _():
        yt_ref[...] = jnp.zeros_like(yt_ref)
        start_gathers(0)

    @pl.when(jnp.logical_and(step == 0, jnp.logical_or(j == 0, prev_valid)))
    def _():
        for r in range(tm):
            gather_wait(r).wait()

    @pl.when(jnp.logical_and(valid, step == 0))
    def _():
        xb_ref[...] = _tiles_to_rows(x_ref, tm).astype(BF16)
        start_gathers(j + 1)
        acc_ref[...] = ffn_step()

    @pl.when(jnp.logical_and(step == 1, jnp.logical_or(j == 1, prev2_valid)))
    def _():
        for r in range(tm):
            scatter_wait(slot, r).wait()

    @pl.when(jnp.logical_and(valid, step == 1))
    def _():
        start_scatters(j - 1, 1 - slot)
        _rows_to_tiles(yt_ref.at[slot], acc_ref[...] + ffn_step())

    @pl.when(jnp.logical_and(jnp.logical_and(jnp.logical_not(valid), prev_valid), step == 1))
    def _():
        start_scatters(j - 1, 1 - slot)


def _expert_ffn(tile_e, tile_valid, src, dst, hf, wg, wu, wd):
    n_tok = hf.shape[0] // SUBLANES
    d = D_MODEL
    n_tiles = tile_e.shape[0]
    d_ff = wg.shape[2]
    nf = d_ff // TF_EXP
    assert nf == 2
    j = jnp.arange(n_tiles, dtype=jnp.int32)
    first = j % 2
    n_valid = jnp.sum(tile_valid)
    last_f = 1 - (n_valid - 1) % 2
    half = jnp.where(tile_valid[:, None] > 0, jnp.stack([first, 1 - first], axis=1), last_f)
    tile_f = half.reshape(-1).astype(jnp.int32)

    return pl.pallas_call(
        _expert_kernel,
        grid_spec=pltpu.PrefetchScalarGridSpec(
            num_scalar_prefetch=5,
            grid=(n_tiles, nf),
            in_specs=[
                pl.BlockSpec(memory_space=pl.ANY),
                pl.BlockSpec((1, d, TF_EXP), lambda j, s, te, tv, tf, *_: (te[j], 0, tf[2 * j + s])),
                pl.BlockSpec((1, d, TF_EXP), lambda j, s, te, tv, tf, *_: (te[j], 0, tf[2 * j + s])),
                pl.BlockSpec((1, TF_EXP, d), lambda j, s, te, tv, tf, *_: (te[j], tf[2 * j + s], 0)),
            ],
            out_specs=pl.BlockSpec(memory_space=pl.ANY),
            scratch_shapes=[pltpu.VMEM((TM_EXP * SUBLANES, LANES), F32),
                            pltpu.VMEM((TM_EXP, d), BF16),
                            pltpu.VMEM((TM_EXP, d), F32),
                            pltpu.VMEM((2, TM_EXP * SUBLANES, LANES), F32),
                            pltpu.SemaphoreType.DMA(()), pltpu.SemaphoreType.DMA(())],
        ),
        out_shape=jax.ShapeDtypeStruct(((TOP_K * n_tok + TM_EXP) * SUBLANES, LANES), F32),
        compiler_params=_params(2),
        name="moe_experts",
    )(tile_e, tile_valid, tile_f, src, dst, hf, wg, wu, wd)


def _combine_kernel(y0_ref, y1_ref, h_ref, gate_ref, gf_ref, o_ref):
    gates = gate_ref[...]
    n = h_ref.shape[0]
    mix = gates[:, 0:1] * _tiles_to_rows(y0_ref, n) + gates[:, 1:2] * _tiles_to_rows(y1_ref, n)
    o_ref[...] = h_ref[...] + gf_ref[0] * mix


def _combine_rows(y, h, gates, mods, seq):
    t, d = h.shape
    steps_per_batch = seq // TG
    second = t // TG
    tok = pl.BlockSpec((TG, d), lambda i: (i, 0))
    return pl.pallas_call(
        _combine_kernel,
        grid=(t // TG,),
        in_specs=[
            pl.BlockSpec((TG * SUBLANES, LANES), lambda i: (i, 0)),
            pl.BlockSpec((TG * SUBLANES, LANES), lambda i: (second + i, 0)),
            tok,
            pl.BlockSpec((TG, TOP_K), lambda i: (i, 0)),
            pl.BlockSpec((1, 1, d), lambda i: (COND_ROWS + i // steps_per_batch, 0, N_MODS - 1)),
        ],
        out_specs=tok,
        out_shape=jax.ShapeDtypeStruct((t, d), F32),
        compiler_params=_params(1),
        name="moe_combine",
    )(y, y, h, gates, mods)


def _rope_tables(seq):
    rows = seq // GRID_W
    row = np.repeat(np.arange(rows), GRID_W).astype(np.float32)
    col = np.tile(np.arange(GRID_W), rows).astype(np.float32)
    quarter = ATTN_DIM // 4
    inv = (np.float32(ROPE_BASE) ** (-np.arange(quarter, dtype=np.float32) / quarter)).astype(
        np.float32)
    ar = row[:, None] * inv
    ac = col[:, None] * inv
    ang = np.concatenate([ar, ar, ac, ac], axis=-1)
    sign = np.tile(np.concatenate([-np.ones(quarter, np.float32), np.ones(quarter, np.float32)]), 2)
    reps = QK_W // ATTN_DIM
    cos = np.tile(np.cos(ang).astype(np.float32), (1, reps))
    sin = np.tile((np.sin(ang) * sign).astype(np.float32), (1, reps))
    return jnp.asarray(cos), jnp.asarray(sin)


def kernel(x, c, ctx, c_ctx, ada_w, ada_b, norm_mix_g, norm_ffn_g, w_in_even, q_norm_g, k_norm_g,
           lam_q1, lam_k1, lam_q2, lam_k2, subln_g, conv_w, w_out_even, ffn_w_gate, ffn_w_up,
           ffn_w_down, sgu_w_in, sgu_ln_g, sgu_ln_b, sgu_w_s, sgu_b_s, sgu_w_out, router_w,
           moe_w_gate, moe_w_up, moe_w_down):
    b, s, d = x.shape
    assert d == D_MODEL and b < COND_ROWS and ada_w.shape[0] == 2
    assert s % TM_PROJ == 0 and s % TQ == 0 and s % TM_FFN == 0 and s % TM_SGU == 0 and s % TG == 0

    cond = jnp.zeros((COND_ROWS, d), F32).at[:b].set(c).at[b].set(c_ctx)
    mods = _ada_mods(cond, ada_w, ada_b)

    lambda_init = 0.8 - 0.6 * math.exp(-0.3 * 0)
    group = np.arange(QK_W) // ATTN_DIM
    bd = jnp.asarray(group[:, None] == group[None, :], dtype=BF16)
    reps = QK_W // ATTN_DIM
    qg = jnp.tile(q_norm_g[0], reps)[None, :]
    kg = jnp.tile(k_norm_g[0], reps)[None, :]
    cos, sin = _rope_tables(s)
    w_in = w_in_even[0].astype(BF16)
    g_mix0 = norm_mix_g[0][None, :]
    n_e, _, d_ffe = moe_w_gate[0].shape
    q, k, v, yconv, moe_wg = _inproj(x, mods, g_mix0, w_in, bd, qg, kg, cos, sin, conv_w[0],
                                     moe_w_gate[0].reshape(n_e * d, d_ffe))
    kc, vc = _ctxproj(ctx, mods, g_mix0, w_in, bd, kg, ctx_row=b)
    attn, w_out0, ffn_wg, ffn_wu = _attention(
        q, k, kc, v, vc, subln_g[0][None, :], lam_q1[0][None, :], lam_k1[0][None, :],
        lam_q2[0][None, :], lam_k2[0][None, :], lambda_init,
        [w_out_even[0], ffn_w_gate[0], ffn_w_up[0]])
    h, moe_wd, sgu_win, sgu_wout = _outproj_ffn(
        x, attn, yconv, mods, norm_ffn_g[0][None, :], w_out0, ffn_wg, ffn_wu,
        ffn_w_down[0].astype(BF16),
        [moe_w_down[0].reshape(n_e * d_ffe, d), sgu_w_in[0], sgu_w_out[0]])

    gw = d // SGU_GROUPS
    bias = jnp.repeat(sgu_b_s[0].T, gw, axis=1)
    rw_hi = router_w[0].astype(BF16)
    rw_lo = (router_w[0] - rw_hi.astype(F32)).astype(BF16)
    h, hf, logits, moe_wu = _sgu(h, mods, norm_mix_g[1][None, :], norm_ffn_g[1][None, :],
                                 sgu_win, sgu_ln_g[0][None, :],
                                 sgu_ln_b[0][None, :], sgu_w_s[0].astype(BF16), bias,
                                 sgu_wout, rw_hi, rw_lo,
                                 moe_w_up[0].reshape(n_e * d, d_ffe))

    t = b * s
    n_tiles = (t * TOP_K) // TM_EXP + N_EXPERTS + 2
    pos, gates, tile_e, tile_valid, pad_lo, pad_hi = _route(
        logits.reshape(t, N_EXPERTS), TM_EXP, n_tiles)
    src, dst = _row_tables(pos, pad_lo, pad_hi, t, n_tiles)
    y = _expert_ffn(tile_e, tile_valid, src, dst, hf, moe_wg.reshape(n_e, d, d_ffe),
                    moe_wu.reshape(n_e, d, d_ffe), moe_wd.reshape(n_e, d_ffe, d))
    out = _combine_rows(y, h.reshape(t, d), gates, mods, s)
    return out.reshape(b, s, d)
```

```python
import functools
import math

import jax
import jax.numpy as jnp
import numpy as np
from jax import lax
from jax.experimental import pallas as pl
from jax.experimental.pallas import tpu as pltpu

F32 = jnp.float32
BF16 = jnp.bfloat16

D_MODEL = 1024
GRID_W = 64
ATTN_HEADS = 4
ATTN_DIM = 64
ATTN_VDIM = 2 * ATTN_DIM
QK_W = ATTN_HEADS * 2 * ATTN_DIM
CONV_K = 3
ROPE_BASE = 10000.0
SGU_GROUPS = 4
CHUNK = 128
N_EXPERTS = 8
TOP_K = 2
EPS = 1e-6
LOG2_E = math.log2(math.e)
N_MODS = 6
COND_ROWS = 16

V7X_VMEM_LIMIT = 56 * 1024 * 1024
MXU_N = 256
TM_PROJ = 512
HALO = 16
TQ = 2048
ATTN_RB = 256
ATTN_SLOTS = 8
ATTN_KC = 256
TM_FFN = 512
TM_SGU = 1024
SGU_RB = 256
TM_EXP = 512
TF_EXP = 1792
TG = 512
ROW_DMA_PRIORITY = 1


def _params(n_axes):
    return pltpu.CompilerParams(
        dimension_semantics=("arbitrary",) * n_axes, vmem_limit_bytes=V7X_VMEM_LIMIT)


def _const_spec(shape):
    zeros = (0,) * len(shape)
    return pl.BlockSpec(shape, lambda *_: zeros, pipeline_mode=pl.Buffered(1))


def _dot(a, b):
    return jnp.dot(a, b, preferred_element_type=F32)


def _modulate(x, g, scale, shift):
    ms = jnp.mean(x * x, axis=-1, keepdims=True)
    y = x * lax.rsqrt(ms + EPS)
    return (y * g) * (1.0 + scale) + shift


def _split_bf16(x):
    hi = x.astype(BF16)
    lo = (x - hi.astype(F32)).astype(BF16)
    return hi, lo


assert D_MODEL == 8 * 128
SUBLANES = 8
LANES = 128


def _rows_to_tiles(ref, val, row0=0):
    n = val.shape[0]
    for c in range(SUBLANES):
        ref[pl.ds(row0 * SUBLANES + c, n, stride=SUBLANES), :] = val[:, c * LANES:(c + 1) * LANES]


def _tiles_to_rows(ref, n):
    return jnp.concatenate(
        [ref[pl.ds(c, n, stride=SUBLANES), :] for c in range(SUBLANES)], axis=1)


def _ada_kernel(cond_ref, w_ref, b_ref, o_ref):
    a = jax.nn.silu(cond_ref[...]).astype(BF16)
    o_ref[0] = _dot(a, w_ref[0].astype(BF16)) + b_ref[0]


def _ada_mods(cond, ada_w, ada_b):
    depth = ada_w.shape[0]
    d = D_MODEL
    out = pl.pallas_call(
        _ada_kernel,
        grid=(depth, N_MODS),
        in_specs=[
            pl.BlockSpec((COND_ROWS, d), lambda l, n: (0, 0)),
            pl.BlockSpec((1, d, d), lambda l, n: (l, 0, n)),
            pl.BlockSpec((1, 1, d), lambda l, n: (l, 0, n)),
        ],
        out_specs=pl.BlockSpec((1, COND_ROWS, d), lambda l, n: (l, 0, n)),
        out_shape=jax.ShapeDtypeStruct((depth, COND_ROWS, N_MODS * d), F32),
        compiler_params=_params(2),
        name="ada_mods",
    )(cond, ada_w, ada_b.reshape(depth, 1, N_MODS * d))
    return out.reshape(depth * COND_ROWS, 1, N_MODS * d)


def _mod_spec(layer, chunk, row_of):
    return pl.BlockSpec((1, 1, D_MODEL),
                        lambda *ids: (layer * COND_ROWS + row_of(*ids), 0, chunk))


def _group_rms(t, bd, gain):
    hi, lo = _split_bf16(t * t)
    ss = _dot(hi, bd) + _dot(lo, bd)
    return (t * lax.rsqrt(ss * (1.0 / ATTN_DIM) + EPS)) * gain


def _rope(t, cos, sin_signed, first_half):
    width = t.shape[-1]
    quarter = ATTN_DIM // 4
    ahead = pltpu.roll(t, width - quarter, axis=1)
    behind = pltpu.roll(t, quarter, axis=1)
    return t * cos + jnp.where(first_half, ahead, behind) * sin_signed


def _with_ones_columns(v):
    rows = v.shape[0]
    lane = lax.broadcasted_iota(jnp.int32, (rows, ATTN_VDIM), 1)
    ones_col = jnp.where(lane == 0, 1.0, 0.0).astype(v.dtype)
    parts = []
    for h in range(ATTN_HEADS):
        parts += [v[:, h * ATTN_VDIM:(h + 1) * ATTN_VDIM], ones_col]
    return jnp.concatenate(parts, axis=1)


def _side_cast(w, steps, index_of):
    rows, cols = w.shape
    assert rows % (steps * 16) == 0
    spec = pl.BlockSpec((rows // steps, cols), lambda *ids: (index_of(*ids), 0))
    return spec, jax.ShapeDtypeStruct(w.shape, BF16)


def _inproj_kernel(x_ref, xp_ref, xn_ref, sc_ref, sh_ref, g_ref, w_ref, bd_ref, qg_ref, kg_ref,
                   cos_ref, sin_ref, cw_ref, side_ref, q_ref, k_ref, v_ref, yc_ref, side_out_ref):
    side_out_ref[...] = side_ref[...].astype(BF16)
    i = pl.program_id(1)
    last = pl.num_programs(1) - 1
    tm = x_ref.shape[1]
    w = QK_W
    xe = jnp.concatenate([xp_ref[0], x_ref[0], xn_ref[0]], axis=0)
    he = _modulate(xe, g_ref[...], sc_ref[0], sh_ref[0]).astype(BF16)
    hm = he[HALO:HALO + tm]

    qkv = _dot(hm, w_ref[:, :3 * w])
    bd = bd_ref[...]
    cos = cos_ref[...]
    sin = sin_ref[...]
    lane = lax.broadcasted_iota(jnp.int32, (1, w), 1)
    first_half = (lane % (ATTN_DIM // 2)) < (ATTN_DIM // 4)
    q = _rope(_group_rms(qkv[:, :w], bd, qg_ref[...]), cos, sin, first_half)
    k = _rope(_group_rms(qkv[:, w:2 * w], bd, kg_ref[...]), cos, sin, first_half)
    q_ref[0] = (q * (ATTN_DIM ** -0.5 * LOG2_E)).astype(BF16)
    k_ref[0] = k.astype(BF16)
    v_ref[0] = _with_ones_columns(qkv[:, 2 * w:].astype(BF16))

    gcu = _dot(he, w_ref[:, 4 * w:])
    cu = gcu[:, :w] * gcu[:, w:]
    row = lax.broadcasted_iota(jnp.int32, (tm + 2 * HALO, 1), 0)
    first_row = jnp.where(i == 0, HALO, 0)
    end_row = jnp.where(i == last, HALO + tm, tm + 2 * HALO)
    cu = jnp.where((row >= first_row) & (row < end_row), cu, 0.0)
    prev = pltpu.roll(cu, 1, axis=0)[HALO:HALO + tm]
    nxt = pltpu.roll(cu, tm + 2 * HALO - 1, axis=0)[HALO:HALO + tm]
    cw = cw_ref[...]
    conv = prev * cw[0:1] + cu[HALO:HALO + tm] * cw[1:2] + nxt * cw[2:3]
    gate_b = _dot(hm, w_ref[:, 3 * w:4 * w])
    yc_ref[0] = (gate_b * conv).astype(BF16)


def _inproj(x, mods, g, w_in, bd, qg, kg, cos, sin, conv_w, side_w):
    b, s, d = x.shape
    tm = TM_PROJ
    nh = tm // HALO
    n_halo_blocks = s // HALO
    tok = pl.BlockSpec((1, tm, QK_W), lambda bi, i: (bi, i, 0))
    out_sds = jax.ShapeDtypeStruct((b, s, QK_W), BF16)
    side_spec, side_sds = _side_cast(side_w, b * (s // tm), lambda bi, i: bi * (s // tm) + i)
    return pl.pallas_call(
        _inproj_kernel,
        grid=(b, s // tm),
        in_specs=[
            pl.BlockSpec((1, tm, d), lambda bi, i: (bi, i, 0)),
            pl.BlockSpec((1, HALO, d), lambda bi, i: (bi, jnp.maximum(i * nh - 1, 0), 0)),
            pl.BlockSpec((1, HALO, d),
                         lambda bi, i: (bi, jnp.minimum((i + 1) * nh, n_halo_blocks - 1), 0)),
            _mod_spec(0, 1, lambda bi, i: bi),
            _mod_spec(0, 0, lambda bi, i: bi),
            _const_spec((1, d)),
            _const_spec(w_in.shape),
            _const_spec(bd.shape),
            _const_spec((1, QK_W)),
            _const_spec((1, QK_W)),
            pl.BlockSpec((tm, QK_W), lambda bi, i: (i, 0)),
            pl.BlockSpec((tm, QK_W), lambda bi, i: (i, 0)),
            _const_spec(conv_w.shape),
            side_spec,
        ],
        out_specs=[tok, tok, pl.BlockSpec((1, tm, 2 * QK_W), lambda bi, i: (bi, i, 0)), tok,
                   side_spec],
        out_shape=[out_sds, out_sds, jax.ShapeDtypeStruct((b, s, 2 * QK_W), BF16), out_sds,
                   side_sds],
        compiler_params=_params(2),
        name="inproj",
    )(x, x, x, mods, mods, g, w_in, bd, qg, kg, cos, sin, conv_w, side_w)


def _ctxproj_kernel(x_ref, sc_ref, sh_ref, g_ref, wk_ref, wv_ref, bd_ref, kg_ref, k_ref, v_ref):
    hc = _modulate(x_ref[0], g_ref[...], sc_ref[0], sh_ref[0]).astype(BF16)
    k_ref[0] = _group_rms(_dot(hc, wk_ref[...]), bd_ref[...], kg_ref[...]).astype(BF16)
    v_ref[0] = _with_ones_columns(_dot(hc, wv_ref[...]).astype(BF16))


def _ctxproj(ctx, mods, g, w_in, bd, kg, ctx_row):
    b, l, d = ctx.shape
    out = pl.BlockSpec((1, l, QK_W), lambda bi: (bi, 0, 0))
    out_sds = jax.ShapeDtypeStruct((b, l, QK_W), BF16)
    return pl.pallas_call(
        _ctxproj_kernel,
        grid=(b,),
        in_specs=[
            pl.BlockSpec((1, l, d), lambda bi: (bi, 0, 0)),
            _mod_spec(0, 1, lambda bi: ctx_row),
            _mod_spec(0, 0, lambda bi: ctx_row),
            _const_spec((1, d)),
            pl.BlockSpec((d, QK_W), lambda bi: (0, 1), pipeline_mode=pl.Buffered(1)),
            pl.BlockSpec((d, QK_W), lambda bi: (0, 2), pipeline_mode=pl.Buffered(1)),
            _const_spec(bd.shape),
            _const_spec((1, QK_W)),
        ],
        out_specs=[out, pl.BlockSpec((1, l, 2 * QK_W), lambda bi: (bi, 0, 0))],
        out_shape=[out_sds, jax.ShapeDtypeStruct((b, l, 2 * QK_W), BF16)],
        compiler_params=_params(1),
        name="ctxproj",
    )(ctx, mods, mods, g, w_in, w_in, bd, kg)


def _attn_kernel(q_ref, k_ref, kc_ref, v_ref, vc_ref, sg_ref, lq1_ref, lk1_ref, lq2_ref, lk2_ref,
                 *rest, lambda_init, n_side):
    side_refs, o_ref = rest[:n_side], rest[n_side]
    side_out_refs = rest[n_side + 1:2 * n_side + 1]
    s_ref, e_ref = rest[2 * n_side + 1:]
    for side_ref, side_out_ref in zip(side_refs, side_out_refs):
        side_out_ref[...] = side_ref[...].astype(BF16)
    tq = q_ref.shape[1]
    n_ctx = kc_ref.shape[1] // ATTN_KC
    n_lat = k_ref.shape[1] // ATTN_KC
    chunks = [(kc_ref, vc_ref, c) for c in range(n_ctx)] + [(k_ref, v_ref, c) for c in range(n_lat)]
    lane = lax.broadcasted_iota(jnp.int32, (1, ATTN_VDIM), 1)
    first = lane < ATTN_DIM
    lam = (jnp.exp(jnp.sum(lq1_ref[...] * lk1_ref[...], keepdims=True))
           - jnp.exp(jnp.sum(lq2_ref[...] * lk2_ref[...], keepdims=True)) + lambda_init)
    nt = (((1,), (1,)), ((), ()))

    for rb in range(tq // ATTN_RB):
        rows = slice(rb * ATTN_RB, (rb + 1) * ATTN_RB)
        q = q_ref[0, rows, :]
        zero = jnp.zeros_like(q)
        outs = []
        for part in range(2):
            qx = jnp.where(first, q, zero) if part == 0 else jnp.where(first, zero, q)
            slot = (rb * 2 + part) % ATTN_SLOTS
            mpart = None
            for ci, (kr, _, c) in enumerate(chunks):
                sc = lax.dot_general(qx, kr[0, c * ATTN_KC:(c + 1) * ATTN_KC, :], nt,
                                     preferred_element_type=F32)
                s_ref[slot, ci] = sc
                cm = sc[:, :ATTN_VDIM]
                for j in range(1, ATTN_KC // ATTN_VDIM):
                    cm = jnp.maximum(cm, sc[:, j * ATTN_VDIM:(j + 1) * ATTN_VDIM])
                mpart = cm if mpart is None else jnp.maximum(mpart, cm)
            m = jnp.max(mpart, axis=-1, keepdims=True)
            for ci in range(len(chunks)):
                e_ref[slot, :, ci * ATTN_KC:(ci + 1) * ATTN_KC] = (
                    jnp.exp2(s_ref[slot, ci] - m).astype(BF16))
            n_c = n_ctx * ATTN_KC
            acc = _dot(e_ref[slot, :, :n_c], vc_ref[0]) + _dot(e_ref[slot, :, n_c:], v_ref[0])
            outs.append(acc[:, :ATTN_VDIM] * (1.0 / acc[:, ATTN_VDIM:ATTN_VDIM + 1]))
        o = outs[0] - lam * outs[1]
        ms = jnp.mean(o * o, axis=-1, keepdims=True)
        y = (o * lax.rsqrt(ms + EPS)) * sg_ref[...]
        o_ref[0, rows, :] = (y * (1.0 - lambda_init)).astype(BF16)


def _attention(q, k, kc, v, vc, subln_g, lq1, lk1, lq2, lk2, lambda_init, side_ws):
    b, s, _ = q.shape
    l = kc.shape[1]
    hd = ATTN_VDIM
    va = 2 * ATTN_VDIM
    assert l % ATTN_KC == 0 and s % ATTN_KC == 0 and s == TQ
    n_chunks = (l + s) // ATTN_KC
    vec = _const_spec((1, ATTN_DIM))
    sides = [_side_cast(w, b * ATTN_HEADS, lambda bi, h, i: bi * ATTN_HEADS + h) for w in side_ws]
    return pl.pallas_call(
        functools.partial(_attn_kernel, lambda_init=lambda_init, n_side=len(side_ws)),
        grid=(b, ATTN_HEADS, s // TQ),
        in_specs=[
            pl.BlockSpec((1, TQ, hd), lambda bi, h, i: (bi, i, h)),
            pl.BlockSpec((1, s, hd), lambda bi, h, i: (bi, 0, h)),
            pl.BlockSpec((1, l, hd), lambda bi, h, i: (bi, 0, h)),
            pl.BlockSpec((1, s, va), lambda bi, h, i: (bi, 0, h)),
            pl.BlockSpec((1, l, va), lambda bi, h, i: (bi, 0, h)),
            _const_spec((1, hd)),
            vec, vec, vec, vec,
        ] + [spec for spec, _ in sides],
        out_specs=[pl.BlockSpec((1, TQ, hd), lambda bi, h, i: (bi, i, h))]
        + [spec for spec, _ in sides],
        out_shape=[jax.ShapeDtypeStruct((b, s, ATTN_HEADS * hd), BF16)] + [sds for _, sds in sides],
        scratch_shapes=[pltpu.VMEM((ATTN_SLOTS, n_chunks, ATTN_RB, ATTN_KC), F32),
                        pltpu.VMEM((ATTN_SLOTS, ATTN_RB, l + s), BF16)],
        compiler_params=_params(3),
        name="diff_attention",
    )(q, k, kc, v, vc, subln_g, lq1, lk1, lq2, lk2, *side_ws)


def _outproj_ffn_kernel(x_ref, at_ref, yc_ref, gm_ref, sf_ref, cf_ref, gf_ref, g_ref,
                        wo_ref, wg_ref, wu_ref, wd_ref, *rest):
    n_side = (len(rest) - 2) // 2
    o_ref, a_ref = rest[n_side], rest[-1]
    for side_ref, side_out_ref in zip(rest[:n_side], rest[n_side + 1:-1]):
        side_out_ref[...] = side_ref[...].astype(BF16)
    half = wo_ref.shape[0] // 2
    d_ff = wg_ref.shape[1]
    y = _dot(at_ref[0], wo_ref[:half]) + _dot(yc_ref[0], wo_ref[half:])
    h1 = x_ref[0] + gm_ref[0] * y
    hf = _modulate(h1, g_ref[...], cf_ref[0], sf_ref[0]).astype(BF16)
    for f in range(d_ff // MXU_N):
        cols = slice(f * MXU_N, (f + 1) * MXU_N)
        a = jax.nn.silu(_dot(hf, wg_ref[:, cols])) * _dot(hf, wu_ref[:, cols])
        a_ref[:, cols] = a.astype(BF16)
    o_ref[0] = h1 + gf_ref[0] * _dot(a_ref[...], wd_ref[...])


def _outproj_ffn(x, attn, yconv, mods, g, w_out, wg, wu, wd, side_ws):
    b, s, d = x.shape
    tm = TM_FFN
    sides = [_side_cast(w, b * (s // tm), lambda bi, i: bi * (s // tm) + i) for w in side_ws]
    return pl.pallas_call(
        _outproj_ffn_kernel,
        grid=(b, s // tm),
        in_specs=[
            pl.BlockSpec((1, tm, d), lambda bi, i: (bi, i, 0)),
            pl.BlockSpec((1, tm, QK_W), lambda bi, i: (bi, i, 0)),
            pl.BlockSpec((1, tm, QK_W), lambda bi, i: (bi, i, 0)),
            _mod_spec(0, 2, lambda bi, i: bi),
            _mod_spec(0, 3, lambda bi, i: bi),
            _mod_spec(0, 4, lambda bi, i: bi),
            _mod_spec(0, 5, lambda bi, i: bi),
            _const_spec((1, d)),
            _const_spec(w_out.shape),
            _const_spec(wg.shape),
            _const_spec(wu.shape),
            _const_spec(wd.shape),
        ] + [spec for spec, _ in sides],
        out_specs=[pl.BlockSpec((1, tm, d), lambda bi, i: (bi, i, 0))] + [spec for spec, _ in sides],
        out_shape=[jax.ShapeDtypeStruct((b, s, d), F32)] + [sds for _, sds in sides],
        scratch_shapes=[pltpu.VMEM((tm, wg.shape[1]), BF16)],
        compiler_params=_params(2),
        name="outproj_ffn",
    )(x, attn, yconv, mods, mods, mods, mods, g, w_out, wg, wu, wd, *side_ws)


def _sgu_kernel(x_ref, sm_ref, cm_ref, gm_ref, sf_ref, cf_ref, g_ref, gff_ref, win_ref, lng_ref,
                lnb_ref, ws_ref, bs_ref, wo_ref, rwh_ref, rwl_ref, side_ref,
                h_ref, hf_ref, lg_ref, side_out_ref, us_ref):
    side_out_ref[...] = side_ref[...].astype(BF16)
    tm, d = x_ref.shape[1:]
    gw = d // SGU_GROUPS
    for rb in range(tm // SGU_RB):
        r0 = rb * SGU_RB
        x = x_ref[0, r0:r0 + SGU_RB, :]
        hm = _modulate(x, g_ref[...], cm_ref[0], sm_ref[0]).astype(BF16)
        z = jax.nn.gelu(_dot(hm, win_ref[...]))
        v = z[:, d:]
        mu = jnp.mean(v, axis=-1, keepdims=True)
        vc = v - mu
        var = jnp.mean(vc * vc, axis=-1, keepdims=True)
        vn = ((vc * lax.rsqrt(var + EPS)) * lng_ref[...] + lnb_ref[...]).astype(BF16)
        for n in range(SGU_RB // CHUNK):
            rows = slice(n * CHUNK, (n + 1) * CHUNK)
            for gi in range(SGU_GROUPS):
                cols = slice(gi * gw, (gi + 1) * gw)
                s = _dot(ws_ref[gi], vn[rows, cols]) + bs_ref[:, cols]
                us_ref[r0 + n * CHUNK:r0 + (n + 1) * CHUNK, cols] = (z[rows, cols] * s).astype(BF16)
        h = x + gm_ref[0] * _dot(us_ref[r0:r0 + SGU_RB, :], wo_ref[...])
        h_ref[0, r0:r0 + SGU_RB, :] = h
        hf = _modulate(h, gff_ref[...], cf_ref[0], sf_ref[0])
        _rows_to_tiles(hf_ref, hf, row0=r0)
        hi, lo = _split_bf16(hf)
        lg_ref[0, r0:r0 + SGU_RB, :] = (
            _dot(hi, rwh_ref[...]) + (_dot(hi, rwl_ref[...]) + _dot(lo, rwh_ref[...])))


def _sgu(h, mods, g_mix, g_ffn, w_in, ln_g, ln_b, w_s, bias, w_out, rw_hi, rw_lo, side_w):
    b, s, d = h.shape
    tm = TM_SGU
    tok = pl.BlockSpec((1, tm, d), lambda bi, i: (bi, i, 0))
    side_spec, side_sds = _side_cast(side_w, b * (s // tm), lambda bi, i: bi * (s // tm) + i)
    return pl.pallas_call(
        _sgu_kernel,
        grid=(b, s // tm),
        in_specs=[
            tok,
            _mod_spec(1, 0, lambda bi, i: bi),
            _mod_spec(1, 1, lambda bi, i: bi),
            _mod_spec(1, 2, lambda bi, i: bi),
            _mod_spec(1, 3, lambda bi, i: bi),
            _mod_spec(1, 4, lambda bi, i: bi),
            _const_spec((1, d)),
            _const_spec((1, d)),
            _const_spec(w_in.shape),
            _const_spec((1, d)),
            _const_spec((1, d)),
            _const_spec(w_s.shape),
            _const_spec(bias.shape),
            _const_spec(w_out.shape),
            _const_spec(rw_hi.shape),
            _const_spec(rw_lo.shape),
            side_spec,
        ],
        out_specs=[tok,
                   pl.BlockSpec((tm * SUBLANES, LANES), lambda bi, i: (bi * (s // tm) + i, 0)),
                   pl.BlockSpec((1, tm, N_EXPERTS), lambda bi, i: (bi, i, 0)),
                   side_spec],
        out_shape=[jax.ShapeDtypeStruct((b, s, d), F32),
                   jax.ShapeDtypeStruct((b * s * SUBLANES, LANES), F32),
                   jax.ShapeDtypeStruct((b, s, N_EXPERTS), F32),
                   side_sds],
        scratch_shapes=[pltpu.VMEM((tm, d), BF16)],
        compiler_params=_params(2),
        name="sgu",
    )(h, mods, mods, mods, mods, mods, g_mix, g_ffn, w_in, ln_g, ln_b, w_s, bias, w_out,
      rw_hi, rw_lo, side_w)


def _route_kernel(lt_ref, pos_ref, gate_ref, cnt_ref, *, tile_rows):
    lt = lt_ref[...]
    e, t = lt.shape
    eid = lax.broadcasted_iota(jnp.int32, (e, t), 0).astype(F32)
    lane = lax.broadcasted_iota(jnp.int32, (e, t), 1)
    v1 = jnp.max(lt, axis=0, keepdims=True)
    i1 = jnp.min(jnp.where(lt == v1, eid, float(e)), axis=0, keepdims=True)
    first = eid == i1
    rest = jnp.where(first, -jnp.inf, lt)
    v2 = jnp.max(rest, axis=0, keepdims=True)
    i2 = jnp.min(jnp.where(rest == v2, eid, float(e)), axis=0, keepdims=True)
    second = eid == i2
    e2 = jnp.exp(v2 - v1)
    gate_ref[0:1, :] = 1.0 / (1.0 + e2)
    gate_ref[1:2, :] = e2 / (1.0 + e2)

    onehot = jnp.where(first, 1.0, jnp.where(second, 1.0, 0.0))
    csum = onehot
    d = 1
    while d < t:
        csum = csum + jnp.where(lane >= d, pltpu.roll(csum, d, axis=1), 0.0)
        d *= 2
    counts = csum[:, t - 1:t]
    padded = jnp.ceil(counts * (1.0 / tile_rows)) * tile_rows
    pad_b = jnp.broadcast_to(padded, (e, LANES))
    sub = lax.broadcasted_iota(jnp.int32, (e, LANES), 0)
    starts = jnp.zeros((e, LANES), F32)
    for k in range(1, e):
        starts = starts + jnp.where(sub >= k, pltpu.roll(pad_b, k, axis=0), 0.0)
    slot = starts[:, 0:1] + csum - onehot
    pos_ref[0:1, :] = jnp.sum(jnp.where(first, slot, 0.0), axis=0, keepdims=True).astype(jnp.int32)
    pos_ref[1:2, :] = jnp.sum(jnp.where(second, slot, 0.0), axis=0, keepdims=True).astype(jnp.int32)
    cnt_ref[...] = jnp.broadcast_to(counts, (e, LANES)).astype(jnp.int32)


def _route(logits, tile_rows, n_tiles):
    lt = logits.T
    t = lt.shape[1]
    whole = lambda shape: pl.BlockSpec(shape, lambda: (0,) * len(shape))
    pos2, gates2, cnt = pl.pallas_call(
        functools.partial(_route_kernel, tile_rows=tile_rows),
        in_specs=[whole(lt.shape)],
        out_specs=[whole((TOP_K, t)), whole((TOP_K, t)), whole((N_EXPERTS, LANES))],
        out_shape=[jax.ShapeDtypeStruct((TOP_K, t), jnp.int32),
                   jax.ShapeDtypeStruct((TOP_K, t), F32),
                   jax.ShapeDtypeStruct((N_EXPERTS, LANES), jnp.int32)],
        name="moe_route",
    )(lt)
    pos = pos2.reshape(-1)
    gates = gates2.T
    counts = cnt[:, 0]
    padded = ((counts + tile_rows - 1) // tile_rows) * tile_rows
    ends = jnp.cumsum(padded)
    starts = ends - padded
    tile_start = jnp.arange(n_tiles, dtype=jnp.int32) * tile_rows
    valid = tile_start < ends[-1]
    tile_e = jnp.sum(tile_start[:, None] >= ends[None, :], axis=-1).astype(jnp.int32)
    last_e = jnp.max(jnp.where(counts > 0, jnp.arange(N_EXPERTS, dtype=jnp.int32), 0))
    tile_e = jnp.where(valid, tile_e, last_e)
    total = jnp.full((1,), n_tiles * tile_rows, jnp.int32)
    pad_lo = jnp.concatenate([starts + counts, ends[-1:]]).astype(jnp.int32)
    pad_hi = jnp.concatenate([ends, total]).astype(jnp.int32)
    return pos, gates, tile_e, valid.astype(jnp.int32), pad_lo, pad_hi


INVERT_UNROLL = 32
FILL_UNROLL = 8


def _invert_kernel(pos_ref, lo_ref, hi_ref, pair_ref):
    n_pairs = pos_ref.shape[0]
    assert n_pairs % INVERT_UNROLL == 0

    def fill(blk, lo):
        for u in range(FILL_UNROLL):
            pair_ref[lo + blk * FILL_UNROLL + u] = -1
        return lo

    def place(blk, c):
        for u in range(INVERT_UNROLL):
            i = blk * INVERT_UNROLL + u
            pair_ref[pos_ref[i]] = i
        return c

    for g in range(lo_ref.shape[0]):
        n_blocks = (hi_ref[g] - lo_ref[g] + FILL_UNROLL - 1) // FILL_UNROLL
        lax.fori_loop(0, n_blocks, fill, lo_ref[g])
    lax.fori_loop(0, n_pairs // INVERT_UNROLL, place, 0)


def _row_tables(pos, pad_lo, pad_hi, n_tokens, n_tiles):
    n_rows = n_tiles * TM_EXP
    smem = pl.BlockSpec(memory_space=pltpu.SMEM)
    pair = pl.pallas_call(
        _invert_kernel,
        in_specs=[smem, smem, smem],
        out_specs=smem,
        out_shape=jax.ShapeDtypeStruct((n_rows,), jnp.int32),
        name="moe_invert",
    )(pos, pad_lo, pad_hi)
    scratch_row = n_tokens * TOP_K + jnp.arange(n_rows, dtype=jnp.int32) % TM_EXP
    src = jnp.where(pair < 0, 0, jnp.where(pair >= n_tokens, pair - n_tokens, pair))
    dst = jnp.where(pair < 0, scratch_row, pair)
    return src * SUBLANES, jnp.concatenate([scratch_row[:TM_EXP], dst]) * SUBLANES


def _expert_kernel(te_ref, tv_ref, tf_ref, src_ref, dst_ref, hf_ref, wg_ref, wu_ref, wd_ref, y_ref,
                   x_ref, xb_ref, acc_ref, yt_ref, sem_in, sem_out):
    del te_ref, tf_ref
    tm = TM_EXP
    j = pl.program_id(0)
    step = pl.program_id(1)
    valid = tv_ref[j] > 0
    prev_valid = jnp.logical_and(j > 0, tv_ref[jnp.maximum(j - 1, 0)] > 0)
    prev2_valid = jnp.logical_and(j > 1, tv_ref[jnp.maximum(j - 2, 0)] > 0)
    slot = j % 2
    x_rows = lambda r: x_ref.at[pl.ds(r * SUBLANES, SUBLANES)]
    yt_rows = lambda sl, r: yt_ref.at[sl, pl.ds(r * SUBLANES, SUBLANES)]

    def start_gathers(tile):
        for r in range(tm):
            t8 = src_ref[tile * tm + r]
            pltpu.make_async_copy(hf_ref.at[pl.ds(pl.multiple_of(t8, SUBLANES), SUBLANES)],
                                  x_rows(r), sem_in).start(priority=ROW_DMA_PRIORITY)

    def gather_wait(r):
        return pltpu.make_async_copy(hf_ref.at[pl.ds(0, SUBLANES)], x_rows(r), sem_in)

    def start_scatters(tile, sl):
        for r in range(tm):
            d8 = dst_ref[(tile + 1) * tm + r]
            pltpu.make_async_copy(yt_rows(sl, r),
                                  y_ref.at[pl.ds(pl.multiple_of(d8, SUBLANES), SUBLANES)],
                                  sem_out).start(priority=ROW_DMA_PRIORITY)

    def scatter_wait(sl, r):
        return pltpu.make_async_copy(yt_rows(sl, r), y_ref.at[pl.ds(0, SUBLANES)], sem_out)

    def ffn_step():
        xb = xb_ref[...]
        a = jax.nn.silu(_dot(xb, wg_ref[0])) * _dot(xb, wu_ref[0])
        return _dot(a.astype(BF16), wd_ref[0])

    @pl.when(jnp.logical_and(j == 0, step == 0))
    def _():
        yt_ref[...] = jnp.zeros_like(yt_ref)
        start_gathers(0)

    @pl.when(jnp.logical_and(step == 0, jnp.logical_or(j == 0, prev_valid)))
    def _():
        for r in range(tm):
            gather_wait(r).wait()

    @pl.when(jnp.logical_and(valid, step == 0))
    def _():
        xb_ref[...] = _tiles_to_rows(x_ref, tm).astype(BF16)
        start_gathers(j + 1)
        acc_ref[...] = ffn_step()

    @pl.when(jnp.logical_and(step == 1, jnp.logical_or(j == 1, prev2_valid)))
    def _():
        for r in range(tm):
            scatter_wait(slot, r).wait()

    @pl.when(jnp.logical_and(valid, step == 1))
    def _():
        start_scatters(j - 1, 1 - slot)
        _rows_to_tiles(yt_ref.at[slot], acc_ref[...] + ffn_step())

    @pl.when(jnp.logical_and(jnp.logical_and(jnp.logical_not(valid), prev_valid), step == 1))
    def _():
        start_scatters(j - 1, 1 - slot)


def _expert_ffn(tile_e, tile_valid, src, dst, hf, wg, wu, wd):
    n_tok = hf.shape[0] // SUBLANES
    d = D_MODEL
    n_tiles = tile_e.shape[0]
    d_ff = wg.shape[2]
    nf = d_ff // TF_EXP
    assert nf == 2
    j = jnp.arange(n_tiles, dtype=jnp.int32)
    first = j % 2
    n_valid = jnp.sum(tile_valid)
    last_f = 1 - (n_valid - 1) % 2
    half = jnp.where(tile_valid[:, None] > 0, jnp.stack([first, 1 - first], axis=1), last_f)
    tile_f = half.reshape(-1).astype(jnp.int32)

    return pl.pallas_call(
        _expert_kernel,
        grid_spec=pltpu.PrefetchScalarGridSpec(
            num_scalar_prefetch=5,
            grid=(n_tiles, nf),
            in_specs=[
                pl.BlockSpec(memory_space=pl.ANY),
                pl.BlockSpec((1, d, TF_EXP), lambda j, s, te, tv, tf, *_: (te[j], 0, tf[2 * j + s])),
                pl.BlockSpec((1, d, TF_EXP), lambda j, s, te, tv, tf, *_: (te[j], 0, tf[2 * j + s])),
                pl.BlockSpec((1, TF_EXP, d), lambda j, s, te, tv, tf, *_: (te[j], tf[2 * j + s], 0)),
            ],
            out_specs=pl.BlockSpec(memory_space=pl.ANY),
            scratch_shapes=[pltpu.VMEM((TM_EXP * SUBLANES, LANES), F32),
                            pltpu.VMEM((TM_EXP, d), BF16),
                            pltpu.VMEM((TM_EXP, d), F32),
                            pltpu.VMEM((2, TM_EXP * SUBLANES, LANES), F32),
                            pltpu.SemaphoreType.DMA(()), pltpu.SemaphoreType.DMA(())],
        ),
        out_shape=jax.ShapeDtypeStruct(((TOP_K * n_tok + TM_EXP) * SUBLANES, LANES), F32),
        compiler_params=_params(2),
        name="moe_experts",
    )(tile_e, tile_valid, tile_f, src, dst, hf, wg, wu, wd)


def _combine_kernel(y0_ref, y1_ref, h_ref, gate_ref, gf_ref, o_ref):
    gates = gate_ref[...]
    n = h_ref.shape[0]
    mix = gates[:, 0:1] * _tiles_to_rows(y0_ref, n) + gates[:, 1:2] * _tiles_to_rows(y1_ref, n)
    o_ref[...] = h_ref[...] + gf_ref[0] * mix


def _combine_rows(y, h, gates, mods, seq):
    t, d = h.shape
    steps_per_batch = seq // TG
    second = t // TG
    tok = pl.BlockSpec((TG, d), lambda i: (i, 0))
    return pl.pallas_call(
        _combine_kernel,
        grid=(t // TG,),
        in_specs=[
            pl.BlockSpec((TG * SUBLANES, LANES), lambda i: (i, 0)),
            pl.BlockSpec((TG * SUBLANES, LANES), lambda i: (second + i, 0)),
            tok,
            pl.BlockSpec((TG, TOP_K), lambda i: (i, 0)),
            pl.BlockSpec((1, 1, d), lambda i: (COND_ROWS + i // steps_per_batch, 0, N_MODS - 1)),
        ],
        out_specs=tok,
        out_shape=jax.ShapeDtypeStruct((t, d), F32),
        compiler_params=_params(1),
        name="moe_combine",
    )(y, y, h, gates, mods)


def _rope_tables(seq):
    rows = seq // GRID_W
    row = np.repeat(np.arange(rows), GRID_W).astype(np.float32)
    col = np.tile(np.arange(GRID_W), rows).astype(np.float32)
    quarter = ATTN_DIM // 4
    inv = (np.float32(ROPE_BASE) ** (-np.arange(quarter, dtype=np.float32) / quarter)).astype(
        np.float32)
    ar = row[:, None] * inv
    ac = col[:, None] * inv
    ang = np.concatenate([ar, ar, ac, ac], axis=-1)
    sign = np.tile(np.concatenate([-np.ones(quarter, np.float32), np.ones(quarter, np.float32)]), 2)
    reps = QK_W // ATTN_DIM
    cos = np.tile(np.cos(ang).astype(np.float32), (1, reps))
    sin = np.tile((np.sin(ang) * sign).astype(np.float32), (1, reps))
    return jnp.asarray(cos), jnp.asarray(sin)


def kernel(x, c, ctx, c_ctx, ada_w, ada_b, norm_mix_g, norm_ffn_g, w_in_even, q_norm_g, k_norm_g,
           lam_q1, lam_k1, lam_q2, lam_k2, subln_g, conv_w, w_out_even, ffn_w_gate, ffn_w_up,
           ffn_w_down, sgu_w_in, sgu_ln_g, sgu_ln_b, sgu_w_s, sgu_b_s, sgu_w_out, router_w,
           moe_w_gate, moe_w_up, moe_w_down):
    b, s, d = x.shape
    assert d == D_MODEL and b < COND_ROWS and ada_w.shape[0] == 2
    assert s % TM_PROJ == 0 and s % TQ == 0 and s % TM_FFN == 0 and s % TM_SGU == 0 and s % TG == 0

    cond = jnp.zeros((COND_ROWS, d), F32).at[:b].set(c).at[b].set(c_ctx)
    mods = _ada_mods(cond, ada_w, ada_b)

    lambda_init = 0.8 - 0.6 * math.exp(-0.3 * 0)
    group = np.arange(QK_W) // ATTN_DIM
    bd = jnp.asarray(group[:, None] == group[None, :], dtype=BF16)
    reps = QK_W // ATTN_DIM
    qg = jnp.tile(q_norm_g[0], reps)[None, :]
    kg = jnp.tile(k_norm_g[0], reps)[None, :]
    cos, sin = _rope_tables(s)
    w_in = w_in_even[0].astype(BF16)
    g_mix0 = norm_mix_g[0][None, :]
    n_e, _, d_ffe = moe_w_gate[0].shape
    q, k, v, yconv, moe_wg = _inproj(x, mods, g_mix0, w_in, bd, qg, kg, cos, sin, conv_w[0],
                                     moe_w_gate[0].reshape(n_e * d, d_ffe))
    kc, vc = _ctxproj(ctx, mods, g_mix0, w_in, bd, kg, ctx_row=b)
    attn, w_out0, ffn_wg, ffn_wu = _attention(
        q, k, kc, v, vc, subln_g[0][None, :], lam_q1[0][None, :], lam_k1[0][None, :],
        lam_q2[0][None, :], lam_k2[0][None, :], lambda_init,
        [w_out_even[0], ffn_w_gate[0], ffn_w_up[0]])
    h, moe_wd, sgu_win, sgu_wout = _outproj_ffn(
        x, attn, yconv, mods, norm_ffn_g[0][None, :], w_out0, ffn_wg, ffn_wu,
        ffn_w_down[0].astype(BF16),
        [moe_w_down[0].reshape(n_e * d_ffe, d), sgu_w_in[0], sgu_w_out[0]])

    gw = d // SGU_GROUPS
    bias = jnp.repeat(sgu_b_s[0].T, gw, axis=1)
    rw_hi = router_w[0].astype(BF16)
    rw_lo = (router_w[0] - rw_hi.astype(F32)).astype(BF16)
    h, hf, logits, moe_wu = _sgu(h, mods, norm_mix_g[1][None, :], norm_ffn_g[1][None, :],
                                 sgu_win, sgu_ln_g[0][None, :],
                                 sgu_ln_b[0][None, :], sgu_w_s[0].astype(BF16), bias,
                                 sgu_wout, rw_hi, rw_lo,
                                 moe_w_up[0].reshape(n_e * d, d_ffe))

    t = b * s
    n_tiles = (t * TOP_K) // TM_EXP + N_EXPERTS + 2
    pos, gates, tile_e, tile_valid, pad_lo, pad_hi = _route(
        logits.reshape(t, N_EXPERTS), TM_EXP, n_tiles)
    src, dst = _row_tables(pos, pad_lo, pad_hi, t, n_tiles)
    y = _expert_ffn(tile_e, tile_valid, src, dst, hf, moe_wg.reshape(n_e, d, d_ffe),
                    moe_wu.reshape(n_e, d, d_ffe), moe_wd.reshape(n_e, d_ffe, d))
    out = _combine_rows(y, h.reshape(t, d), gates, mods, s)
    return out.reshape(b, s, d)
```

```python
import functools
import math

import jax
import jax.numpy as jnp
import numpy as np
from jax import lax
from jax.experimental import pallas as pl
from jax.experimental.pallas import tpu as pltpu

F32 = jnp.float32
BF16 = jnp.bfloat16

D_MODEL = 1024
GRID_W = 64
ATTN_HEADS = 4
ATTN_DIM = 64
ATTN_VDIM = 2 * ATTN_DIM
QK_W = ATTN_HEADS * 2 * ATTN_DIM
CONV_K = 3
ROPE_BASE = 10000.0
SGU_GROUPS = 4
CHUNK = 128
N_EXPERTS = 8
TOP_K = 2
EPS = 1e-6
LOG2_E = math.log2(math.e)
N_MODS = 6
COND_ROWS = 16

V7X_VMEM_LIMIT = 56 * 1024 * 1024
MXU_N = 256
TM_PROJ = 512
HALO = 16
TQ = 2048
ATTN_RB = 256
ATTN_SLOTS = 8
ATTN_KC = 256
TM_FFN = 512
TM_SGU = 1024
SGU_RB = 256
TM_EXP = 512
TF_EXP = 1792
TG = 512
ROW_DMA_PRIORITY = 1


def _params(n_axes):
    return pltpu.CompilerParams(
        dimension_semantics=("arbitrary",) * n_axes, vmem_limit_bytes=V7X_VMEM_LIMIT)


def _const_spec(shape):
    zeros = (0,) * len(shape)
    return pl.BlockSpec(shape, lambda *_: zeros, pipeline_mode=pl.Buffered(1))


def _dot(a, b):
    return jnp.dot(a, b, preferred_element_type=F32)


def _modulate(x, g, scale, shift):
    ms = jnp.mean(x * x, axis=-1, keepdims=True)
    y = x * lax.rsqrt(ms + EPS)
    return y * (g * (1.0 + scale)) + shift


def _split_bf16(x):
    hi = x.astype(BF16)
    lo = (x - hi.astype(F32)).astype(BF16)
    return hi, lo


assert D_MODEL == 8 * 128
SUBLANES = 8
LANES = 128


def _rows_to_tiles(ref, val, row0=0):
    n = val.shape[0]
    for c in range(SUBLANES):
        ref[pl.ds(row0 * SUBLANES + c, n, stride=SUBLANES), :] = val[:, c * LANES:(c + 1) * LANES]


def _tiles_to_rows(ref, n):
    return jnp.concatenate(
        [ref[pl.ds(c, n, stride=SUBLANES), :] for c in range(SUBLANES)], axis=1)


def _ada_kernel(cond_ref, w_ref, b_ref, o_ref):
    a = jax.nn.silu(cond_ref[...]).astype(BF16)
    o_ref[0] = _dot(a, w_ref[0].astype(BF16)) + b_ref[0]


def _ada_mods(cond, ada_w, ada_b):
    depth = ada_w.shape[0]
    d = D_MODEL
    out = pl.pallas_call(
        _ada_kernel,
        grid=(depth, N_MODS),
        in_specs=[
            pl.BlockSpec((COND_ROWS, d), lambda l, n: (0, 0)),
            pl.BlockSpec((1, d, d), lambda l, n: (l, 0, n)),
            pl.BlockSpec((1, 1, d), lambda l, n: (l, 0, n)),
        ],
        out_specs=pl.BlockSpec((1, COND_ROWS, d), lambda l, n: (l, 0, n)),
        out_shape=jax.ShapeDtypeStruct((depth, COND_ROWS, N_MODS * d), F32),
        compiler_params=_params(2),
        name="ada_mods",
    )(cond, ada_w, ada_b.reshape(depth, 1, N_MODS * d))
    return out.reshape(depth * COND_ROWS, 1, N_MODS * d)


def _mod_spec(layer, chunk, row_of):
    return pl.BlockSpec((1, 1, D_MODEL),
                        lambda *ids: (layer * COND_ROWS + row_of(*ids), 0, chunk))


def _group_rms(t, bd, gain):
    hi, lo = _split_bf16(t * t)
    ss = _dot(hi, bd) + _dot(lo, bd)
    return (t * lax.rsqrt(ss * (1.0 / ATTN_DIM) + EPS)) * gain


def _rope(t, cos, sin_signed, first_half):
    width = t.shape[-1]
    quarter = ATTN_DIM // 4
    ahead = pltpu.roll(t, width - quarter, axis=1)
    behind = pltpu.roll(t, quarter, axis=1)
    return t * cos + jnp.where(first_half, ahead, behind) * sin_signed


def _with_ones_columns(v):
    rows = v.shape[0]
    lane = lax.broadcasted_iota(jnp.int32, (rows, ATTN_VDIM), 1)
    ones_col = jnp.where(lane == 0, 1.0, 0.0).astype(v.dtype)
    parts = []
    for h in range(ATTN_HEADS):
        parts += [v[:, h * ATTN_VDIM:(h + 1) * ATTN_VDIM], ones_col]
    return jnp.concatenate(parts, axis=1)


def _side_cast(w, steps, index_of):
    rows, cols = w.shape
    assert rows % (steps * 16) == 0
    spec = pl.BlockSpec((rows // steps, cols), lambda *ids: (index_of(*ids), 0))
    return spec, jax.ShapeDtypeStruct(w.shape, BF16)


def _inproj_kernel(x_ref, xp_ref, xn_ref, sc_ref, sh_ref, g_ref, w_ref, bd_ref, qg_ref, kg_ref,
                   cos_ref, sin_ref, cw_ref, side_ref, q_ref, k_ref, v_ref, yc_ref, side_out_ref):
    side_out_ref[...] = side_ref[...].astype(BF16)
    i = pl.program_id(1)
    last = pl.num_programs(1) - 1
    tm = x_ref.shape[1]
    w = QK_W
    xe = jnp.concatenate([xp_ref[0], x_ref[0], xn_ref[0]], axis=0)
    he = _modulate(xe, g_ref[...], sc_ref[0], sh_ref[0]).astype(BF16)
    hm = he[HALO:HALO + tm]

    qkv = _dot(hm, w_ref[:, :3 * w])
    bd = bd_ref[...]
    cos = cos_ref[...]
    sin = sin_ref[...]
    lane = lax.broadcasted_iota(jnp.int32, (1, w), 1)
    first_half = (lane % (ATTN_DIM // 2)) < (ATTN_DIM // 4)
    q = _rope(_group_rms(qkv[:, :w], bd, qg_ref[...]), cos, sin, first_half)
    k = _rope(_group_rms(qkv[:, w:2 * w], bd, kg_ref[...]), cos, sin, first_half)
    q_ref[0] = (q * (ATTN_DIM ** -0.5 * LOG2_E)).astype(BF16)
    k_ref[0] = k.astype(BF16)
    v_ref[0] = _with_ones_columns(qkv[:, 2 * w:].astype(BF16))

    gcu = _dot(he, w_ref[:, 4 * w:])
    cu = gcu[:, :w] * gcu[:, w:]
    row = lax.broadcasted_iota(jnp.int32, (tm + 2 * HALO, 1), 0)
    first_row = jnp.where(i == 0, HALO, 0)
    end_row = jnp.where(i == last, HALO + tm, tm + 2 * HALO)
    cu = jnp.where((row >= first_row) & (row < end_row), cu, 0.0)
    prev = pltpu.roll(cu, 1, axis=0)[HALO:HALO + tm]
    nxt = pltpu.roll(cu, tm + 2 * HALO - 1, axis=0)[HALO:HALO + tm]
    cw = cw_ref[...]
    conv = prev * cw[0:1] + cu[HALO:HALO + tm] * cw[1:2] + nxt * cw[2:3]
    gate_b = _dot(hm, w_ref[:, 3 * w:4 * w])
    yc_ref[0] = (gate_b * conv).astype(BF16)


def _inproj(x, mods, g, w_in, bd, qg, kg, cos, sin, conv_w, side_w):
    b, s, d = x.shape
    tm = TM_PROJ
    nh = tm // HALO
    n_halo_blocks = s // HALO
    tok = pl.BlockSpec((1, tm, QK_W), lambda bi, i: (bi, i, 0))
    out_sds = jax.ShapeDtypeStruct((b, s, QK_W), BF16)
    side_spec, side_sds = _side_cast(side_w, b * (s // tm), lambda bi, i: bi * (s // tm) + i)
    return pl.pallas_call(
        _inproj_kernel,
        grid=(b, s // tm),
        in_specs=[
            pl.BlockSpec((1, tm, d), lambda bi, i: (bi, i, 0)),
            pl.BlockSpec((1, HALO, d), lambda bi, i: (bi, jnp.maximum(i * nh - 1, 0), 0)),
            pl.BlockSpec((1, HALO, d),
                         lambda bi, i: (bi, jnp.minimum((i + 1) * nh, n_halo_blocks - 1), 0)),
            _mod_spec(0, 1, lambda bi, i: bi),
            _mod_spec(0, 0, lambda bi, i: bi),
            _const_spec((1, d)),
            _const_spec(w_in.shape),
            _const_spec(bd.shape),
            _const_spec((1, QK_W)),
            _const_spec((1, QK_W)),
            pl.BlockSpec((tm, QK_W), lambda bi, i: (i, 0)),
            pl.BlockSpec((tm, QK_W), lambda bi, i: (i, 0)),
            _const_spec(conv_w.shape),
            side_spec,
        ],
        out_specs=[tok, tok, pl.BlockSpec((1, tm, 2 * QK_W), lambda bi, i: (bi, i, 0)), tok,
                   side_spec],
        out_shape=[out_sds, out_sds, jax.ShapeDtypeStruct((b, s, 2 * QK_W), BF16), out_sds,
                   side_sds],
        compiler_params=_params(2),
        name="inproj",
    )(x, x, x, mods, mods, g, w_in, bd, qg, kg, cos, sin, conv_w, side_w)


def _ctxproj_kernel(x_ref, sc_ref, sh_ref, g_ref, wk_ref, wv_ref, bd_ref, kg_ref, k_ref, v_ref):
    hc = _modulate(x_ref[0], g_ref[...], sc_ref[0], sh_ref[0]).astype(BF16)
    k_ref[0] = _group_rms(_dot(hc, wk_ref[...]), bd_ref[...], kg_ref[...]).astype(BF16)
    v_ref[0] = _with_ones_columns(_dot(hc, wv_ref[...]).astype(BF16))


def _ctxproj(ctx, mods, g, w_in, bd, kg, ctx_row):
    b, l, d = ctx.shape
    out = pl.BlockSpec((1, l, QK_W), lambda bi: (bi, 0, 0))
    out_sds = jax.ShapeDtypeStruct((b, l, QK_W), BF16)
    return pl.pallas_call(
        _ctxproj_kernel,
        grid=(b,),
        in_specs=[
            pl.BlockSpec((1, l, d), lambda bi: (bi, 0, 0)),
            _mod_spec(0, 1, lambda bi: ctx_row),
            _mod_spec(0, 0, lambda bi: ctx_row),
            _const_spec((1, d)),
            pl.BlockSpec((d, QK_W), lambda bi: (0, 1), pipeline_mode=pl.Buffered(1)),
            pl.BlockSpec((d, QK_W), lambda bi: (0, 2), pipeline_mode=pl.Buffered(1)),
            _const_spec(bd.shape),
            _const_spec((1, QK_W)),
        ],
        out_specs=[out, pl.BlockSpec((1, l, 2 * QK_W), lambda bi: (bi, 0, 0))],
        out_shape=[out_sds, jax.ShapeDtypeStruct((b, l, 2 * QK_W), BF16)],
        compiler_params=_params(1),
        name="ctxproj",
    )(ctx, mods, mods, g, w_in, w_in, bd, kg)


def _attn_kernel(q_ref, k_ref, kc_ref, v_ref, vc_ref, sg_ref, lq1_ref, lk1_ref, lq2_ref, lk2_ref,
                 *rest, lambda_init, n_side):
    side_refs, o_ref = rest[:n_side], rest[n_side]
    side_out_refs = rest[n_side + 1:2 * n_side + 1]
    s_ref, e_ref = rest[2 * n_side + 1:]
    for side_ref, side_out_ref in zip(side_refs, side_out_refs):
        side_out_ref[...] = side_ref[...].astype(BF16)
    tq = q_ref.shape[1]
    n_ctx = kc_ref.shape[1] // ATTN_KC
    n_lat = k_ref.shape[1] // ATTN_KC
    chunks = [(kc_ref, vc_ref, c) for c in range(n_ctx)] + [(k_ref, v_ref, c) for c in range(n_lat)]
    lane = lax.broadcasted_iota(jnp.int32, (1, ATTN_VDIM), 1)
    first = lane < ATTN_DIM
    lam = (jnp.exp(jnp.sum(lq1_ref[...] * lk1_ref[...], keepdims=True))
           - jnp.exp(jnp.sum(lq2_ref[...] * lk2_ref[...], keepdims=True)) + lambda_init)
    nt = (((1,), (1,)), ((), ()))

    for rb in range(tq // ATTN_RB):
        rows = slice(rb * ATTN_RB, (rb + 1) * ATTN_RB)
        q = q_ref[0, rows, :]
        zero = jnp.zeros_like(q)
        outs = []
        for part in range(2):
            qx = jnp.where(first, q, zero) if part == 0 else jnp.where(first, zero, q)
            slot = (rb * 2 + part) % ATTN_SLOTS
            mpart = None
            for ci, (kr, _, c) in enumerate(chunks):
                sc = lax.dot_general(qx, kr[0, c * ATTN_KC:(c + 1) * ATTN_KC, :], nt,
                                     preferred_element_type=F32)
                s_ref[slot, ci] = sc
                cm = sc[:, :ATTN_VDIM]
                for j in range(1, ATTN_KC // ATTN_VDIM):
                    cm = jnp.maximum(cm, sc[:, j * ATTN_VDIM:(j + 1) * ATTN_VDIM])
                mpart = cm if mpart is None else jnp.maximum(mpart, cm)
            m = jnp.max(mpart, axis=-1, keepdims=True)
            for ci in range(len(chunks)):
                e_ref[slot, :, ci * ATTN_KC:(ci + 1) * ATTN_KC] = (
                    jnp.exp2(s_ref[slot, ci] - m).astype(BF16))
            n_c = n_ctx * ATTN_KC
            acc = _dot(e_ref[slot, :, :n_c], vc_ref[0]) + _dot(e_ref[slot, :, n_c:], v_ref[0])
            outs.append(acc[:, :ATTN_VDIM] * (1.0 / acc[:, ATTN_VDIM:ATTN_VDIM + 1]))
        o = outs[0] - lam * outs[1]
        ms = jnp.mean(o * o, axis=-1, keepdims=True)
        y = (o * lax.rsqrt(ms + EPS)) * sg_ref[...]
        o_ref[0, rows, :] = (y * (1.0 - lambda_init)).astype(BF16)


def _attention(q, k, kc, v, vc, subln_g, lq1, lk1, lq2, lk2, lambda_init, side_ws):
    b, s, _ = q.shape
    l = kc.shape[1]
    hd = ATTN_VDIM
    va = 2 * ATTN_VDIM
    assert l % ATTN_KC == 0 and s % ATTN_KC == 0 and s == TQ
    n_chunks = (l + s) // ATTN_KC
    vec = _const_spec((1, ATTN_DIM))
    sides = [_side_cast(w, b * ATTN_HEADS // rep,
                        lambda bi, h, i, rep=rep: (bi * ATTN_HEADS + h) // rep)
             for w, rep in side_ws]
    side_ws = [w for w, _ in side_ws]
    return pl.pallas_call(
        functools.partial(_attn_kernel, lambda_init=lambda_init, n_side=len(side_ws)),
        grid=(b, ATTN_HEADS, s // TQ),
        in_specs=[
            pl.BlockSpec((1, TQ, hd), lambda bi, h, i: (bi, i, h)),
            pl.BlockSpec((1, s, hd), lambda bi, h, i: (bi, 0, h)),
            pl.BlockSpec((1, l, hd), lambda bi, h, i: (bi, 0, h)),
            pl.BlockSpec((1, s, va), lambda bi, h, i: (bi, 0, h)),
            pl.BlockSpec((1, l, va), lambda bi, h, i: (bi, 0, h)),
            _const_spec((1, hd)),
            vec, vec, vec, vec,
        ] + [spec for spec, _ in sides],
        out_specs=[pl.BlockSpec((1, TQ, hd), lambda bi, h, i: (bi, i, h))]
        + [spec for spec, _ in sides],
        out_shape=[jax.ShapeDtypeStruct((b, s, ATTN_HEADS * hd), BF16)] + [sds for _, sds in sides],
        scratch_shapes=[pltpu.VMEM((ATTN_SLOTS, n_chunks, ATTN_RB, ATTN_KC), F32),
                        pltpu.VMEM((ATTN_SLOTS, ATTN_RB, l + s), BF16)],
        compiler_params=_params(3),
        name="diff_attention",
    )(q, k, kc, v, vc, subln_g, lq1, lk1, lq2, lk2, *side_ws)


def _outproj_ffn_kernel(x_ref, at_ref, yc_ref, gm_ref, sf_ref, cf_ref, gf_ref, g_ref,
                        wo_ref, wg_ref, wu_ref, wd_ref, *rest):
    n_side = (len(rest) - 2) // 2
    o_ref, a_ref = rest[n_side], rest[-1]
    for side_ref, side_out_ref in zip(rest[:n_side], rest[n_side + 1:-1]):
        side_out_ref[...] = side_ref[...].astype(BF16)
    half = wo_ref.shape[0] // 2
    d_ff = wg_ref.shape[1]
    y = _dot(at_ref[0], wo_ref[:half]) + _dot(yc_ref[0], wo_ref[half:])
    h1 = x_ref[0] + gm_ref[0] * y
    hf = _modulate(h1, g_ref[...], cf_ref[0], sf_ref[0]).astype(BF16)
    for f in range(d_ff // MXU_N):
        cols = slice(f * MXU_N, (f + 1) * MXU_N)
        a = jax.nn.silu(_dot(hf, wg_ref[:, cols])) * _dot(hf, wu_ref[:, cols])
        a_ref[:, cols] = a.astype(BF16)
    o_ref[0] = h1 + gf_ref[0] * _dot(a_ref[...], wd_ref[...])


def _outproj_ffn(x, attn, yconv, mods, g, w_out, wg, wu, wd, side_ws):
    b, s, d = x.shape
    tm = TM_FFN
    sides = [_side_cast(w, b * (s // tm), lambda bi, i: bi * (s // tm) + i) for w in side_ws]
    return pl.pallas_call(
        _outproj_ffn_kernel,
        grid=(b, s // tm),
        in_specs=[
            pl.BlockSpec((1, tm, d), lambda bi, i: (bi, i, 0)),
            pl.BlockSpec((1, tm, QK_W), lambda bi, i: (bi, i, 0)),
            pl.BlockSpec((1, tm, QK_W), lambda bi, i: (bi, i, 0)),
            _mod_spec(0, 2, lambda bi, i: bi),
            _mod_spec(0, 3, lambda bi, i: bi),
            _mod_spec(0, 4, lambda bi, i: bi),
            _mod_spec(0, 5, lambda bi, i: bi),
            _const_spec((1, d)),
            _const_spec(w_out.shape),
            _const_spec(wg.shape),
            _const_spec(wu.shape),
            _const_spec(wd.shape),
        ] + [spec for spec, _ in sides],
        out_specs=[pl.BlockSpec((1, tm, d), lambda bi, i: (bi, i, 0))] + [spec for spec, _ in sides],
        out_shape=[jax.ShapeDtypeStruct((b, s, d), F32)] + [sds for _, sds in sides],
        scratch_shapes=[pltpu.VMEM((tm, wg.shape[1]), BF16)],
        compiler_params=_params(2),
        name="outproj_ffn",
    )(x, attn, yconv, mods, mods, mods, mods, g, w_out, wg, wu, wd, *side_ws)


def _sgu_kernel(x_ref, sm_ref, cm_ref, gm_ref, sf_ref, cf_ref, g_ref, gff_ref, win_ref, lng_ref,
                lnb_ref, ws_ref, bs_ref, wo_ref, rwh_ref, rwl_ref, side_ref,
                h_ref, hf_ref, lg_ref, side_out_ref, us_ref):
    side_out_ref[...] = side_ref[...].astype(BF16)
    tm, d = x_ref.shape[1:]
    gw = d // SGU_GROUPS
    for rb in range(tm // SGU_RB):
        r0 = rb * SGU_RB
        x = x_ref[0, r0:r0 + SGU_RB, :]
        hm = _modulate(x, g_ref[...], cm_ref[0], sm_ref[0]).astype(BF16)
        z = jax.nn.gelu(_dot(hm, win_ref[...]))
        v = z[:, d:]
        mu = jnp.mean(v, axis=-1, keepdims=True)
        vc = v - mu
        var = jnp.mean(vc * vc, axis=-1, keepdims=True)
        vn = ((vc * lax.rsqrt(var + EPS)) * lng_ref[...] + lnb_ref[...]).astype(BF16)
        for n in range(SGU_RB // CHUNK):
            rows = slice(n * CHUNK, (n + 1) * CHUNK)
            for gi in range(SGU_GROUPS):
                cols = slice(gi * gw, (gi + 1) * gw)
                s = _dot(ws_ref[gi], vn[rows, cols]) + bs_ref[:, cols]
                us_ref[r0 + n * CHUNK:r0 + (n + 1) * CHUNK, cols] = (z[rows, cols] * s).astype(BF16)
        h = x + gm_ref[0] * _dot(us_ref[r0:r0 + SGU_RB, :], wo_ref[...])
        h_ref[0, r0:r0 + SGU_RB, :] = h
        hf = _modulate(h, gff_ref[...], cf_ref[0], sf_ref[0])
        _rows_to_tiles(hf_ref, hf, row0=r0)
        hi, lo = _split_bf16(hf)
        lg_ref[0, r0:r0 + SGU_RB, :] = (
            _dot(hi, rwh_ref[...]) + (_dot(hi, rwl_ref[...]) + _dot(lo, rwh_ref[...])))


def _sgu(h, mods, g_mix, g_ffn, w_in, ln_g, ln_b, w_s, bias, w_out, rw_hi, rw_lo, side_w):
    b, s, d = h.shape
    tm = TM_SGU
    tok = pl.BlockSpec((1, tm, d), lambda bi, i: (bi, i, 0))
    side_spec, side_sds = _side_cast(side_w, b * (s // tm), lambda bi, i: bi * (s // tm) + i)
    return pl.pallas_call(
        _sgu_kernel,
        grid=(b, s // tm),
        in_specs=[
            tok,
            _mod_spec(1, 0, lambda bi, i: bi),
            _mod_spec(1, 1, lambda bi, i: bi),
            _mod_spec(1, 2, lambda bi, i: bi),
            _mod_spec(1, 3, lambda bi, i: bi),
            _mod_spec(1, 4, lambda bi, i: bi),
            _const_spec((1, d)),
            _const_spec((1, d)),
            _const_spec(w_in.shape),
            _const_spec((1, d)),
            _const_spec((1, d)),
            _const_spec(w_s.shape),
            _const_spec(bias.shape),
            _const_spec(w_out.shape),
            _const_spec(rw_hi.shape),
            _const_spec(rw_lo.shape),
            side_spec,
        ],
        out_specs=[tok,
                   pl.BlockSpec((tm * SUBLANES, LANES), lambda bi, i: (bi * (s // tm) + i, 0)),
                   pl.BlockSpec((1, tm, N_EXPERTS), lambda bi, i: (bi, i, 0)),
                   side_spec],
        out_shape=[jax.ShapeDtypeStruct((b, s, d), F32),
                   jax.ShapeDtypeStruct((b * s * SUBLANES, LANES), F32),
                   jax.ShapeDtypeStruct((b, s, N_EXPERTS), F32),
                   side_sds],
        scratch_shapes=[pltpu.VMEM((tm, d), BF16)],
        compiler_params=_params(2),
        name="sgu",
    )(h, mods, mods, mods, mods, mods, g_mix, g_ffn, w_in, ln_g, ln_b, w_s, bias, w_out,
      rw_hi, rw_lo, side_w)


def _route_kernel(lt_ref, pos_ref, gate_ref, cnt_ref, *, tile_rows):
    lt = lt_ref[...]
    e, t = lt.shape
    eid = lax.broadcasted_iota(jnp.int32, (e, t), 0).astype(F32)
    lane = lax.broadcasted_iota(jnp.int32, (e, t), 1)
    v1 = jnp.max(lt, axis=0, keepdims=True)
    i1 = jnp.min(jnp.where(lt == v1, eid, float(e)), axis=0, keepdims=True)
    first = eid == i1
    rest = jnp.where(first, -jnp.inf, lt)
    v2 = jnp.max(rest, axis=0, keepdims=True)
    i2 = jnp.min(jnp.where(rest == v2, eid, float(e)), axis=0, keepdims=True)
    second = eid == i2
    e2 = jnp.exp(v2 - v1)
    gate_ref[0:1, :] = 1.0 / (1.0 + e2)
    gate_ref[1:2, :] = e2 / (1.0 + e2)

    onehot = jnp.where(first, 1.0, jnp.where(second, 1.0, 0.0))
    csum = onehot
    d = 1
    while d < t:
        csum = csum + jnp.where(lane >= d, pltpu.roll(csum, d, axis=1), 0.0)
        d *= 2
    counts = csum[:, t - 1:t]
    padded = jnp.ceil(counts * (1.0 / tile_rows)) * tile_rows
    pad_b = jnp.broadcast_to(padded, (e, LANES))
    sub = lax.broadcasted_iota(jnp.int32, (e, LANES), 0)
    starts = jnp.zeros((e, LANES), F32)
    for k in range(1, e):
        starts = starts + jnp.where(sub >= k, pltpu.roll(pad_b, k, axis=0), 0.0)
    slot = starts[:, 0:1] + csum - onehot
    pos_ref[0:1, :] = jnp.sum(jnp.where(first, slot, 0.0), axis=0, keepdims=True).astype(jnp.int32)
    pos_ref[1:2, :] = jnp.sum(jnp.where(second, slot, 0.0), axis=0, keepdims=True).astype(jnp.int32)
    cnt_ref[...] = jnp.broadcast_to(counts, (e, LANES)).astype(jnp.int32)


def _route(logits, tile_rows, n_tiles):
    lt = logits.T
    t = lt.shape[1]
    whole = lambda shape: pl.BlockSpec(shape, lambda: (0,) * len(shape))
    pos2, gates2, cnt = pl.pallas_call(
        functools.partial(_route_kernel, tile_rows=tile_rows),
        in_specs=[whole(lt.shape)],
        out_specs=[whole((TOP_K, t)), whole((TOP_K, t)), whole((N_EXPERTS, LANES))],
        out_shape=[jax.ShapeDtypeStruct((TOP_K, t), jnp.int32),
                   jax.ShapeDtypeStruct((TOP_K, t), F32),
                   jax.ShapeDtypeStruct((N_EXPERTS, LANES), jnp.int32)],
        name="moe_route",
    )(lt)
    pos = pos2.reshape(-1)
    gates = gates2.T
    counts = cnt[:, 0]
    padded = ((counts + tile_rows - 1) // tile_rows) * tile_rows
    ends = jnp.cumsum(padded)
    starts = ends - padded
    tile_start = jnp.arange(n_tiles, dtype=jnp.int32) * tile_rows
    valid = tile_start < ends[-1]
    tile_e = jnp.sum(tile_start[:, None] >= ends[None, :], axis=-1).astype(jnp.int32)
    last_e = jnp.max(jnp.where(counts > 0, jnp.arange(N_EXPERTS, dtype=jnp.int32), 0))
    tile_e = jnp.where(valid, tile_e, last_e)
    total = jnp.full((1,), n_tiles * tile_rows, jnp.int32)
    pad_lo = jnp.concatenate([starts + counts, ends[-1:]]).astype(jnp.int32)
    pad_hi = jnp.concatenate([ends, total]).astype(jnp.int32)
    rows_used = jnp.clip(pad_lo[tile_e] - tile_start, 0, tile_rows)
    tile_state = jnp.where(valid, jnp.where(rows_used <= tile_rows // 2, HALF_TILE, FULL_TILE),
                           UNUSED_TILE).astype(jnp.int32)
    return pos, gates, tile_e, tile_state, pad_lo, pad_hi


UNUSED_TILE, FULL_TILE, HALF_TILE = 0, 1, 2


INVERT_UNROLL = 32
FILL_UNROLL = 8


def _invert_kernel(pos_ref, lo_ref, hi_ref, pair_ref):
    n_pairs = pos_ref.shape[0]
    assert n_pairs % INVERT_UNROLL == 0

    def fill(blk, lo):
        for u in range(FILL_UNROLL):
            pair_ref[lo + blk * FILL_UNROLL + u] = -1
        return lo

    def place(blk, c):
        for u in range(INVERT_UNROLL):
            i = blk * INVERT_UNROLL + u
            pair_ref[pos_ref[i]] = i
        return c

    for g in range(lo_ref.shape[0]):
        n_blocks = (hi_ref[g] - lo_ref[g] + FILL_UNROLL - 1) // FILL_UNROLL
        lax.fori_loop(0, n_blocks, fill, lo_ref[g])
    lax.fori_loop(0, n_pairs // INVERT_UNROLL, place, 0)


def _row_tables(pos, pad_lo, pad_hi, n_tokens, n_tiles):
    n_rows = n_tiles * TM_EXP
    smem = pl.BlockSpec(memory_space=pltpu.SMEM)
    pair = pl.pallas_call(
        _invert_kernel,
        in_specs=[smem, smem, smem],
        out_specs=smem,
        out_shape=jax.ShapeDtypeStruct((n_rows,), jnp.int32),
        name="moe_invert",
    )(pos, pad_lo, pad_hi)
    scratch_row = n_tokens * TOP_K + jnp.arange(n_rows, dtype=jnp.int32) % TM_EXP
    src = jnp.where(pair < 0, 0, jnp.where(pair >= n_tokens, pair - n_tokens, pair))
    dst = jnp.where(pair < 0, scratch_row, pair)
    return src * SUBLANES, jnp.concatenate([scratch_row[:TM_EXP], dst]) * SUBLANES


def _expert_kernel(te_ref, tv_ref, tf_ref, src_ref, dst_ref, hf_ref, wg_ref, wu_ref, wd_ref, y_ref,
                   x_ref, xb_ref, acc_ref, yt_ref, sem_in, sem_out):
    del te_ref, tf_ref
    tm = TM_EXP
    j = pl.program_id(0)
    step = pl.program_id(1)
    valid = tv_ref[j] > 0
    prev_valid = jnp.logical_and(j > 0, tv_ref[jnp.maximum(j - 1, 0)] > 0)
    prev2_valid = jnp.logical_and(j > 1, tv_ref[jnp.maximum(j - 2, 0)] > 0)
    slot = j % 2
    x_rows = lambda r: x_ref.at[pl.ds(r * SUBLANES, SUBLANES)]
    yt_rows = lambda sl, r: yt_ref.at[sl, pl.ds(r * SUBLANES, SUBLANES)]

    def start_gathers(tile):
        for r in range(tm):
            t8 = src_ref[tile * tm + r]
            pltpu.make_async_copy(hf_ref.at[pl.ds(pl.multiple_of(t8, SUBLANES), SUBLANES)],
                                  x_rows(r), sem_in).start(priority=ROW_DMA_PRIORITY)

    def gather_wait(r):
        return pltpu.make_async_copy(hf_ref.at[pl.ds(0, SUBLANES)], x_rows(r), sem_in)

    def start_scatters(tile, sl):
        for r in range(tm):
            d8 = dst_ref[(tile + 1) * tm + r]
            pltpu.make_async_copy(yt_rows(sl, r),
                                  y_ref.at[pl.ds(pl.multiple_of(d8, SUBLANES), SUBLANES)],
                                  sem_out).start(priority=ROW_DMA_PRIORITY)

    def scatter_wait(sl, r):
        return pltpu.make_async_copy(yt_rows(sl, r), y_ref.at[pl.ds(0, SUBLANES)], sem_out)

    def ffn_step(rows):
        xb = xb_ref[:rows]
        a = jax.nn.silu(_dot(xb, wg_ref[0])) * _dot(xb, wu_ref[0])
        return _dot(a.astype(BF16), wd_ref[0])

    half_tile = tv_ref[j] == HALF_TILE
    full_tile = tv_ref[j] == FULL_TILE
    hm = tm // 2

    @pl.when(jnp.logical_and(j == 0, step == 0))
    def _():
        yt_ref[...] = jnp.zeros_like(yt_ref)
        start_gathers(0)

    @pl.when(jnp.logical_and(step == 0, jnp.logical_or(j == 0, prev_valid)))
    def _():
        for r in range(tm):
            gather_wait(r).wait()

    @pl.when(jnp.logical_and(full_tile, step == 0))
    def _():
        xb_ref[...] = _tiles_to_rows(x_ref, tm).astype(BF16)
        start_gathers(j + 1)
        acc_ref[...] = ffn_step(tm)

    @pl.when(jnp.logical_and(half_tile, step == 0))
    def _():
        xb_ref[...] = _tiles_to_rows(x_ref, tm).astype(BF16)
        start_gathers(j + 1)
        acc_ref[:hm] = ffn_step(hm)

    @pl.when(jnp.logical_and(step == 1, jnp.logical_or(j == 1, prev2_valid)))
    def _():
        for r in range(tm):
            scatter_wait(slot, r).wait()

    @pl.when(jnp.logical_and(full_tile, step == 1))
    def _():
        start_scatters(j - 1, 1 - slot)
        _rows_to_tiles(yt_ref.at[slot], acc_ref[...] + ffn_step(tm))

    @pl.when(jnp.logical_and(half_tile, step == 1))
    def _():
        start_scatters(j - 1, 1 - slot)
        _rows_to_tiles(yt_ref.at[slot], acc_ref[:hm] + ffn_step(hm))

    @pl.when(jnp.logical_and(jnp.logical_and(jnp.logical_not(valid), prev_valid), step == 1))
    def _():
        start_scatters(j - 1, 1 - slot)


def _expert_ffn(tile_e, tile_valid, src, dst, hf, wg, wu, wd):
    n_tok = hf.shape[0] // SUBLANES
    d = D_MODEL
    n_tiles = tile_e.shape[0]
    d_ff = wg.shape[2]
    nf = d_ff // TF_EXP
    assert nf == 2
    j = jnp.arange(n_tiles, dtype=jnp.int32)
    first = j % 2
    n_valid = jnp.sum(tile_valid > 0)
    last_f = 1 - (n_valid - 1) % 2
    half = jnp.where(tile_valid[:, None] > 0, jnp.stack([first, 1 - first], axis=1), last_f)
    tile_f = half.reshape(-1).astype(jnp.int32)

    return pl.pallas_call(
        _expert_kernel,
        grid_spec=pltpu.PrefetchScalarGridSpec(
            num_scalar_prefetch=5,
            grid=(n_tiles, nf),
            in_specs=[
                pl.BlockSpec(memory_space=pl.ANY),
                pl.BlockSpec((1, d, TF_EXP), lambda j, s, te, tv, tf, *_: (te[j], 0, tf[2 * j + s])),
                pl.BlockSpec((1, d, TF_EXP), lambda j, s, te, tv, tf, *_: (te[j], 0, tf[2 * j + s])),
                pl.BlockSpec((1, TF_EXP, d), lambda j, s, te, tv, tf, *_: (te[j], tf[2 * j + s], 0)),
            ],
            out_specs=pl.BlockSpec(memory_space=pl.ANY),
            scratch_shapes=[pltpu.VMEM((TM_EXP * SUBLANES, LANES), F32),
                            pltpu.VMEM((TM_EXP, d), BF16),
                            pltpu.VMEM((TM_EXP, d), F32),
                            pltpu.VMEM((2, TM_EXP * SUBLANES, LANES), F32),
                            pltpu.SemaphoreType.DMA(()), pltpu.SemaphoreType.DMA(())],
        ),
        out_shape=jax.ShapeDtypeStruct(((TOP_K * n_tok + TM_EXP) * SUBLANES, LANES), F32),
        compiler_params=_params(2),
        name="moe_experts",
    )(tile_e, tile_valid, tile_f, src, dst, hf, wg, wu, wd)


def _combine_kernel(y0_ref, y1_ref, h_ref, gate_ref, gf_ref, o_ref):
    gates = gate_ref[...]
    n = h_ref.shape[0]
    mix = gates[:, 0:1] * _tiles_to_rows(y0_ref, n) + gates[:, 1:2] * _tiles_to_rows(y1_ref, n)
    o_ref[...] = h_ref[...] + gf_ref[0] * mix


def _combine_rows(y, h, gates, mods, seq):
    t, d = h.shape
    steps_per_batch = seq // TG
    second = t // TG
    tok = pl.BlockSpec((TG, d), lambda i: (i, 0))
    return pl.pallas_call(
        _combine_kernel,
        grid=(t // TG,),
        in_specs=[
            pl.BlockSpec((TG * SUBLANES, LANES), lambda i: (i, 0)),
            pl.BlockSpec((TG * SUBLANES, LANES), lambda i: (second + i, 0)),
            tok,
            pl.BlockSpec((TG, TOP_K), lambda i: (i, 0)),
            pl.BlockSpec((1, 1, d), lambda i: (COND_ROWS + i // steps_per_batch, 0, N_MODS - 1)),
        ],
        out_specs=tok,
        out_shape=jax.ShapeDtypeStruct((t, d), F32),
        compiler_params=_params(1),
        name="moe_combine",
    )(y, y, h, gates, mods)


def _rope_tables(seq):
    rows = seq // GRID_W
    row = np.repeat(np.arange(rows), GRID_W).astype(np.float32)
    col = np.tile(np.arange(GRID_W), rows).astype(np.float32)
    quarter = ATTN_DIM // 4
    inv = (np.float32(ROPE_BASE) ** (-np.arange(quarter, dtype=np.float32) / quarter)).astype(
        np.float32)
    ar = row[:, None] * inv
    ac = col[:, None] * inv
    ang = np.concatenate([ar, ar, ac, ac], axis=-1)
    sign = np.tile(np.concatenate([-np.ones(quarter, np.float32), np.ones(quarter, np.float32)]), 2)
    reps = QK_W // ATTN_DIM
    cos = np.tile(np.cos(ang).astype(np.float32), (1, reps))
    sin = np.tile((np.sin(ang) * sign).astype(np.float32), (1, reps))
    return jnp.asarray(cos), jnp.asarray(sin)


def kernel(x, c, ctx, c_ctx, ada_w, ada_b, norm_mix_g, norm_ffn_g, w_in_even, q_norm_g, k_norm_g,
           lam_q1, lam_k1, lam_q2, lam_k2, subln_g, conv_w, w_out_even, ffn_w_gate, ffn_w_up,
           ffn_w_down, sgu_w_in, sgu_ln_g, sgu_ln_b, sgu_w_s, sgu_b_s, sgu_w_out, router_w,
           moe_w_gate, moe_w_up, moe_w_down):
    b, s, d = x.shape
    assert d == D_MODEL and b < COND_ROWS and ada_w.shape[0] == 2 and conv_w.shape[1] == CONV_K
    assert s % TM_PROJ == 0 and s % TQ == 0 and s % TM_FFN == 0 and s % TM_SGU == 0 and s % TG == 0

    cond = jnp.zeros((COND_ROWS, d), F32).at[:b].set(c).at[b].set(c_ctx)
    mods = _ada_mods(cond, ada_w, ada_b)

    lambda_init = 0.8 - 0.6 * math.exp(-0.3 * 0)
    group = np.arange(QK_W) // ATTN_DIM
    bd = jnp.asarray(group[:, None] == group[None, :], dtype=BF16)
    reps = QK_W // ATTN_DIM
    qg = jnp.tile(q_norm_g[0], reps)[None, :]
    kg = jnp.tile(k_norm_g[0], reps)[None, :]
    cos, sin = _rope_tables(s)
    w_in = w_in_even[0].astype(BF16)
    g_mix0 = norm_mix_g[0][None, :]
    n_e, _, d_ffe = moe_w_gate[0].shape
    q, k, v, yconv, moe_wg = _inproj(x, mods, g_mix0, w_in, bd, qg, kg, cos, sin, conv_w[0],
                                     moe_w_gate[0].reshape(n_e * d, d_ffe))
    kc, vc = _ctxproj(ctx, mods, g_mix0, w_in, bd, kg, ctx_row=b)
    attn, w_out0, ffn_wg, ffn_wu, ffn_wd = _attention(
        q, k, kc, v, vc, subln_g[0][None, :], lam_q1[0][None, :], lam_k1[0][None, :],
        lam_q2[0][None, :], lam_k2[0][None, :], lambda_init,
        [(w_out_even[0], 1), (ffn_w_gate[0], 1), (ffn_w_up[0], 1), (ffn_w_down[0], 2)])
    h, moe_wd, sgu_win, sgu_wout = _outproj_ffn(
        x, attn, yconv, mods, norm_ffn_g[0][None, :], w_out0, ffn_wg, ffn_wu, ffn_wd,
        [moe_w_down[0].reshape(n_e * d_ffe, d), sgu_w_in[0], sgu_w_out[0]])

    gw = d // SGU_GROUPS
    bias = jnp.repeat(sgu_b_s[0].T, gw, axis=1)
    rw_hi = router_w[0].astype(BF16)
    rw_lo = (router_w[0] - rw_hi.astype(F32)).astype(BF16)
    h, hf, logits, moe_wu = _sgu(h, mods, norm_mix_g[1][None, :], norm_ffn_g[1][None, :],
                                 sgu_win, sgu_ln_g[0][None, :],
                                 sgu_ln_b[0][None, :], sgu_w_s[0].astype(BF16), bias,
                                 sgu_wout, rw_hi, rw_lo,
                                 moe_w_up[0].reshape(n_e * d, d_ffe))

    t = b * s
    n_tiles = (t * TOP_K) // TM_EXP + N_EXPERTS + 2
    pos, gates, tile_e, tile_valid, pad_lo, pad_hi = _route(
        logits.reshape(t, N_EXPERTS), TM_EXP, n_tiles)
    src, dst = _row_tables(pos, pad_lo, pad_hi, t, n_tiles)
    y = _expert_ffn(tile_e, tile_valid, src, dst, hf, moe_wg.reshape(n_e, d, d_ffe),
                    moe_wu.reshape(n_e, d, d_ffe), moe_wd.reshape(n_e, d_ffe, d))
    out = _combine_rows(y, h.reshape(t, d), gates, mods, s)
    return out.reshape(b, s, d)
```

```python
import functools
import math

import jax
import jax.numpy as jnp
import numpy as np
from jax import lax
from jax.experimental import pallas as pl
from jax.experimental.pallas import tpu as pltpu

F32 = jnp.float32
BF16 = jnp.bfloat16

D_MODEL = 1024
GRID_W = 64
ATTN_HEADS = 4
ATTN_DIM = 64
ATTN_VDIM = 2 * ATTN_DIM
QK_W = ATTN_HEADS * 2 * ATTN_DIM
CONV_K = 3
ROPE_BASE = 10000.0
SGU_GROUPS = 4
CHUNK = 128
N_EXPERTS = 8
TOP_K = 2
EPS = 1e-6
LOG2_E = math.log2(math.e)
N_MODS = 6
COND_ROWS = 16

V7X_VMEM_LIMIT = 56 * 1024 * 1024
MXU_N = 256
TM_PROJ = 512
HALO = 16
TQ = 2048
ATTN_RB = 256
ATTN_SLOTS = 8
ATTN_KC = 256
TM_FFN = 512
TM_SGU = 1024
SGU_RB = 256
TM_EXP = 512
TF_EXP = 1792
TG = 512
ROW_DMA_PRIORITY = 1


def _params(n_axes):
    return pltpu.CompilerParams(
        dimension_semantics=("arbitrary",) * n_axes, vmem_limit_bytes=V7X_VMEM_LIMIT)


def _const_spec(shape):
    zeros = (0,) * len(shape)
    return pl.BlockSpec(shape, lambda *_: zeros, pipeline_mode=pl.Buffered(1))


def _dot(a, b):
    return jnp.dot(a, b, preferred_element_type=F32)


def _modulate(x, g, scale, shift):
    ms = jnp.mean(x * x, axis=-1, keepdims=True)
    y = x * lax.rsqrt(ms + EPS)
    return (y * g) * (1.0 + scale) + shift


def _split_bf16(x):
    hi = x.astype(BF16)
    lo = (x - hi.astype(F32)).astype(BF16)
    return hi, lo


assert D_MODEL == 8 * 128
SUBLANES = 8
LANES = 128


def _rows_to_tiles(ref, val, row0=0):
    n = val.shape[0]
    for c in range(SUBLANES):
        ref[pl.ds(row0 * SUBLANES + c, n, stride=SUBLANES), :] = val[:, c * LANES:(c + 1) * LANES]


def _tiles_to_rows(ref, n):
    return jnp.concatenate(
        [ref[pl.ds(c, n, stride=SUBLANES), :] for c in range(SUBLANES)], axis=1)


def _ada_kernel(cond_ref, w_ref, b_ref, o_ref):
    a = jax.nn.silu(cond_ref[...]).astype(BF16)
    o_ref[0] = _dot(a, w_ref[0].astype(BF16)) + b_ref[0]


def _ada_mods(cond, ada_w, ada_b):
    depth = ada_w.shape[0]
    d = D_MODEL
    out = pl.pallas_call(
        _ada_kernel,
        grid=(depth, N_MODS),
        in_specs=[
            pl.BlockSpec((COND_ROWS, d), lambda l, n: (0, 0)),
            pl.BlockSpec((1, d, d), lambda l, n: (l, 0, n)),
            pl.BlockSpec((1, 1, d), lambda l, n: (l, 0, n)),
        ],
        out_specs=pl.BlockSpec((1, COND_ROWS, d), lambda l, n: (l, 0, n)),
        out_shape=jax.ShapeDtypeStruct((depth, COND_ROWS, N_MODS * d), F32),
        compiler_params=_params(2),
        name="ada_mods",
    )(cond, ada_w, ada_b.reshape(depth, 1, N_MODS * d))
    return out.reshape(depth * COND_ROWS, 1, N_MODS * d)


def _mod_spec(layer, chunk, row_of):
    return pl.BlockSpec((1, 1, D_MODEL),
                        lambda *ids: (layer * COND_ROWS + row_of(*ids), 0, chunk))


def _group_rms(t, bd, gain):
    hi, lo = _split_bf16(t * t)
    ss = _dot(hi, bd) + _dot(lo, bd)
    return (t * lax.rsqrt(ss * (1.0 / ATTN_DIM) + EPS)) * gain


def _rope(t, cos, sin_signed, first_half):
    width = t.shape[-1]
    quarter = ATTN_DIM // 4
    ahead = pltpu.roll(t, width - quarter, axis=1)
    behind = pltpu.roll(t, quarter, axis=1)
    return t * cos + jnp.where(first_half, ahead, behind) * sin_signed


def _with_ones_columns(v):
    rows = v.shape[0]
    lane = lax.broadcasted_iota(jnp.int32, (rows, ATTN_VDIM), 1)
    ones_col = jnp.where(lane == 0, 1.0, 0.0).astype(v.dtype)
    parts = []
    for h in range(ATTN_HEADS):
        parts += [v[:, h * ATTN_VDIM:(h + 1) * ATTN_VDIM], ones_col]
    return jnp.concatenate(parts, axis=1)


def _side_cast(w, steps, index_of):
    rows, cols = w.shape
    assert rows % (steps * 16) == 0
    spec = pl.BlockSpec((rows // steps, cols), lambda *ids: (index_of(*ids), 0))
    return spec, jax.ShapeDtypeStruct(w.shape, BF16)


def _inproj_kernel(x_ref, xp_ref, xn_ref, sc_ref, sh_ref, g_ref, w_ref, bd_ref, qg_ref, kg_ref,
                   cos_ref, sin_ref, cw_ref, side_ref, q_ref, k_ref, v_ref, yc_ref, side_out_ref):
    side_out_ref[...] = side_ref[...].astype(BF16)
    i = pl.program_id(1)
    last = pl.num_programs(1) - 1
    tm = x_ref.shape[1]
    w = QK_W
    xe = jnp.concatenate([xp_ref[0], x_ref[0], xn_ref[0]], axis=0)
    he = _modulate(xe, g_ref[...], sc_ref[0], sh_ref[0]).astype(BF16)
    hm = he[HALO:HALO + tm]

    qkv = _dot(hm, w_ref[:, :3 * w])
    bd = bd_ref[...]
    cos = cos_ref[...]
    sin = sin_ref[...]
    lane = lax.broadcasted_iota(jnp.int32, (1, w), 1)
    first_half = (lane % (ATTN_DIM // 2)) < (ATTN_DIM // 4)
    q = _rope(_group_rms(qkv[:, :w], bd, qg_ref[...]), cos, sin, first_half)
    k = _rope(_group_rms(qkv[:, w:2 * w], bd, kg_ref[...]), cos, sin, first_half)
    q_ref[0] = (q * (ATTN_DIM ** -0.5 * LOG2_E)).astype(BF16)
    k_ref[0] = k.astype(BF16)
    v_ref[0] = _with_ones_columns(qkv[:, 2 * w:].astype(BF16))

    gcu = _dot(he, w_ref[:, 4 * w:])
    cu = gcu[:, :w] * gcu[:, w:]
    row = lax.broadcasted_iota(jnp.int32, (tm + 2 * HALO, 1), 0)
    first_row = jnp.where(i == 0, HALO, 0)
    end_row = jnp.where(i == last, HALO + tm, tm + 2 * HALO)
    cu = jnp.where((row >= first_row) & (row < end_row), cu, 0.0)
    prev = pltpu.roll(cu, 1, axis=0)[HALO:HALO + tm]
    nxt = pltpu.roll(cu, tm + 2 * HALO - 1, axis=0)[HALO:HALO + tm]
    cw = cw_ref[...]
    conv = prev * cw[0:1] + cu[HALO:HALO + tm] * cw[1:2] + nxt * cw[2:3]
    gate_b = _dot(hm, w_ref[:, 3 * w:4 * w])
    yc_ref[0] = (gate_b * conv).astype(BF16)


def _inproj(x, mods, g, w_in, bd, qg, kg, cos, sin, conv_w, side_w):
    b, s, d = x.shape
    tm = TM_PROJ
    nh = tm // HALO
    n_halo_blocks = s // HALO
    tok = pl.BlockSpec((1, tm, QK_W), lambda bi, i: (bi, i, 0))
    out_sds = jax.ShapeDtypeStruct((b, s, QK_W), BF16)
    side_spec, side_sds = _side_cast(side_w, b * (s // tm), lambda bi, i: bi * (s // tm) + i)
    return pl.pallas_call(
        _inproj_kernel,
        grid=(b, s // tm),
        in_specs=[
            pl.BlockSpec((1, tm, d), lambda bi, i: (bi, i, 0)),
            pl.BlockSpec((1, HALO, d), lambda bi, i: (bi, jnp.maximum(i * nh - 1, 0), 0)),
            pl.BlockSpec((1, HALO, d),
                         lambda bi, i: (bi, jnp.minimum((i + 1) * nh, n_halo_blocks - 1), 0)),
            _mod_spec(0, 1, lambda bi, i: bi),
            _mod_spec(0, 0, lambda bi, i: bi),
            _const_spec((1, d)),
            _const_spec(w_in.shape),
            _const_spec(bd.shape),
            _const_spec((1, QK_W)),
            _const_spec((1, QK_W)),
            pl.BlockSpec((tm, QK_W), lambda bi, i: (i, 0)),
            pl.BlockSpec((tm, QK_W), lambda bi, i: (i, 0)),
            _const_spec(conv_w.shape),
            side_spec,
        ],
        out_specs=[tok, tok, pl.BlockSpec((1, tm, 2 * QK_W), lambda bi, i: (bi, i, 0)), tok,
                   side_spec],
        out_shape=[out_sds, out_sds, jax.ShapeDtypeStruct((b, s, 2 * QK_W), BF16), out_sds,
                   side_sds],
        compiler_params=_params(2),
        name="inproj",
    )(x, x, x, mods, mods, g, w_in, bd, qg, kg, cos, sin, conv_w, side_w)


def _ctxproj_kernel(x_ref, sc_ref, sh_ref, g_ref, wk_ref, wv_ref, bd_ref, kg_ref, k_ref, v_ref):
    hc = _modulate(x_ref[0], g_ref[...], sc_ref[0], sh_ref[0]).astype(BF16)
    k_ref[0] = _group_rms(_dot(hc, wk_ref[...]), bd_ref[...], kg_ref[...]).astype(BF16)
    v_ref[0] = _with_ones_columns(_dot(hc, wv_ref[...]).astype(BF16))


def _ctxproj(ctx, mods, g, w_in, bd, kg, ctx_row):
    b, l, d = ctx.shape
    out = pl.BlockSpec((1, l, QK_W), lambda bi: (bi, 0, 0))
    out_sds = jax.ShapeDtypeStruct((b, l, QK_W), BF16)
    return pl.pallas_call(
        _ctxproj_kernel,
        grid=(b,),
        in_specs=[
            pl.BlockSpec((1, l, d), lambda bi: (bi, 0, 0)),
            _mod_spec(0, 1, lambda bi: ctx_row),
            _mod_spec(0, 0, lambda bi: ctx_row),
            _const_spec((1, d)),
            pl.BlockSpec((d, QK_W), lambda bi: (0, 1), pipeline_mode=pl.Buffered(1)),
            pl.BlockSpec((d, QK_W), lambda bi: (0, 2), pipeline_mode=pl.Buffered(1)),
            _const_spec(bd.shape),
            _const_spec((1, QK_W)),
        ],
        out_specs=[out, pl.BlockSpec((1, l, 2 * QK_W), lambda bi: (bi, 0, 0))],
        out_shape=[out_sds, jax.ShapeDtypeStruct((b, l, 2 * QK_W), BF16)],
        compiler_params=_params(1),
        name="ctxproj",
    )(ctx, mods, mods, g, w_in, w_in, bd, kg)


def _attn_kernel(q_ref, k_ref, kc_ref, v_ref, vc_ref, sg_ref, lq1_ref, lk1_ref, lq2_ref, lk2_ref,
                 *rest, lambda_init, n_side):
    side_refs, o_ref = rest[:n_side], rest[n_side]
    side_out_refs = rest[n_side + 1:2 * n_side + 1]
    s_ref, e_ref = rest[2 * n_side + 1:]
    for side_ref, side_out_ref in zip(side_refs, side_out_refs):
        side_out_ref[...] = side_ref[...].astype(BF16)
    tq = q_ref.shape[1]
    n_ctx = kc_ref.shape[1] // ATTN_KC
    n_lat = k_ref.shape[1] // ATTN_KC
    chunks = [(kc_ref, vc_ref, c) for c in range(n_ctx)] + [(k_ref, v_ref, c) for c in range(n_lat)]
    lane = lax.broadcasted_iota(jnp.int32, (1, ATTN_VDIM), 1)
    first = lane < ATTN_DIM
    lam = (jnp.exp(jnp.sum(lq1_ref[...] * lk1_ref[...], keepdims=True))
           - jnp.exp(jnp.sum(lq2_ref[...] * lk2_ref[...], keepdims=True)) + lambda_init)
    nt = (((1,), (1,)), ((), ()))

    for rb in range(tq // ATTN_RB):
        rows = slice(rb * ATTN_RB, (rb + 1) * ATTN_RB)
        q = q_ref[0, rows, :]
        zero = jnp.zeros_like(q)
        outs = []
        for part in range(2):
            qx = jnp.where(first, q, zero) if part == 0 else jnp.where(first, zero, q)
            slot = (rb * 2 + part) % ATTN_SLOTS
            mpart = None
            for ci, (kr, _, c) in enumerate(chunks):
                sc = lax.dot_general(qx, kr[0, c * ATTN_KC:(c + 1) * ATTN_KC, :], nt,
                                     preferred_element_type=F32)
                s_ref[slot, ci] = sc
                cm = sc[:, :ATTN_VDIM]
                for j in range(1, ATTN_KC // ATTN_VDIM):
                    cm = jnp.maximum(cm, sc[:, j * ATTN_VDIM:(j + 1) * ATTN_VDIM])
                mpart = cm if mpart is None else jnp.maximum(mpart, cm)
            m = jnp.max(mpart, axis=-1, keepdims=True)
            for ci in range(len(chunks)):
                e_ref[slot, :, ci * ATTN_KC:(ci + 1) * ATTN_KC] = (
                    jnp.exp2(s_ref[slot, ci] - m).astype(BF16))
            n_c = n_ctx * ATTN_KC
            acc = _dot(e_ref[slot, :, :n_c], vc_ref[0]) + _dot(e_ref[slot, :, n_c:], v_ref[0])
            outs.append(acc[:, :ATTN_VDIM] * (1.0 / acc[:, ATTN_VDIM:ATTN_VDIM + 1]))
        o = outs[0] - lam * outs[1]
        ms = jnp.mean(o * o, axis=-1, keepdims=True)
        y = (o * lax.rsqrt(ms + EPS)) * sg_ref[...]
        o_ref[0, rows, :] = (y * (1.0 - lambda_init)).astype(BF16)


def _attention(q, k, kc, v, vc, subln_g, lq1, lk1, lq2, lk2, lambda_init, side_ws):
    b, s, _ = q.shape
    l = kc.shape[1]
    hd = ATTN_VDIM
    va = 2 * ATTN_VDIM
    assert l % ATTN_KC == 0 and s % ATTN_KC == 0 and s == TQ
    n_chunks = (l + s) // ATTN_KC
    vec = _const_spec((1, ATTN_DIM))
    sides = [_side_cast(w, b * ATTN_HEADS // rep,
                        lambda bi, h, i, rep=rep: (bi * ATTN_HEADS + h) // rep)
             for w, rep in side_ws]
    side_ws = [w for w, _ in side_ws]
    return pl.pallas_call(
        functools.partial(_attn_kernel, lambda_init=lambda_init, n_side=len(side_ws)),
        grid=(b, ATTN_HEADS, s // TQ),
        in_specs=[
            pl.BlockSpec((1, TQ, hd), lambda bi, h, i: (bi, i, h)),
            pl.BlockSpec((1, s, hd), lambda bi, h, i: (bi, 0, h)),
            pl.BlockSpec((1, l, hd), lambda bi, h, i: (bi, 0, h)),
            pl.BlockSpec((1, s, va), lambda bi, h, i: (bi, 0, h)),
            pl.BlockSpec((1, l, va), lambda bi, h, i: (bi, 0, h)),
            _const_spec((1, hd)),
            vec, vec, vec, vec,
        ] + [spec for spec, _ in sides],
        out_specs=[pl.BlockSpec((1, TQ, hd), lambda bi, h, i: (bi, i, h))]
        + [spec for spec, _ in sides],
        out_shape=[jax.ShapeDtypeStruct((b, s, ATTN_HEADS * hd), BF16)] + [sds for _, sds in sides],
        scratch_shapes=[pltpu.VMEM((ATTN_SLOTS, n_chunks, ATTN_RB, ATTN_KC), F32),
                        pltpu.VMEM((ATTN_SLOTS, ATTN_RB, l + s), BF16)],
        compiler_params=_params(3),
        name="diff_attention",
    )(q, k, kc, v, vc, subln_g, lq1, lk1, lq2, lk2, *side_ws)


def _outproj_ffn_kernel(x_ref, at_ref, yc_ref, gm_ref, sf_ref, cf_ref, gf_ref, g_ref,
                        wo_ref, wg_ref, wu_ref, wd_ref, *rest):
    n_side = (len(rest) - 2) // 2
    o_ref, a_ref = rest[n_side], rest[-1]
    for side_ref, side_out_ref in zip(rest[:n_side], rest[n_side + 1:-1]):
        side_out_ref[...] = side_ref[...].astype(BF16)
    half = wo_ref.shape[0] // 2
    d_ff = wg_ref.shape[1]
    y = _dot(at_ref[0], wo_ref[:half]) + _dot(yc_ref[0], wo_ref[half:])
    h1 = x_ref[0] + gm_ref[0] * y
    hf = _modulate(h1, g_ref[...], cf_ref[0], sf_ref[0]).astype(BF16)
    for f in range(d_ff // MXU_N):
        cols = slice(f * MXU_N, (f + 1) * MXU_N)
        a = jax.nn.silu(_dot(hf, wg_ref[:, cols])) * _dot(hf, wu_ref[:, cols])
        a_ref[:, cols] = a.astype(BF16)
    o_ref[0] = h1 + gf_ref[0] * _dot(a_ref[...], wd_ref[...])


def _outproj_ffn(x, attn, yconv, mods, g, w_out, wg, wu, wd, side_ws):
    b, s, d = x.shape
    tm = TM_FFN
    sides = [_side_cast(w, b * (s // tm), lambda bi, i: bi * (s // tm) + i) for w in side_ws]
    return pl.pallas_call(
        _outproj_ffn_kernel,
        grid=(b, s // tm),
        in_specs=[
            pl.BlockSpec((1, tm, d), lambda bi, i: (bi, i, 0)),
            pl.BlockSpec((1, tm, QK_W), lambda bi, i: (bi, i, 0)),
            pl.BlockSpec((1, tm, QK_W), lambda bi, i: (bi, i, 0)),
            _mod_spec(0, 2, lambda bi, i: bi),
            _mod_spec(0, 3, lambda bi, i: bi),
            _mod_spec(0, 4, lambda bi, i: bi),
            _mod_spec(0, 5, lambda bi, i: bi),
            _const_spec((1, d)),
            _const_spec(w_out.shape),
            _const_spec(wg.shape),
            _const_spec(wu.shape),
            _const_spec(wd.shape),
        ] + [spec for spec, _ in sides],
        out_specs=[pl.BlockSpec((1, tm, d), lambda bi, i: (bi, i, 0))] + [spec for spec, _ in sides],
        out_shape=[jax.ShapeDtypeStruct((b, s, d), F32)] + [sds for _, sds in sides],
        scratch_shapes=[pltpu.VMEM((tm, wg.shape[1]), BF16)],
        compiler_params=_params(2),
        name="outproj_ffn",
    )(x, attn, yconv, mods, mods, mods, mods, g, w_out, wg, wu, wd, *side_ws)


def _sgu_kernel(x_ref, sm_ref, cm_ref, gm_ref, sf_ref, cf_ref, g_ref, gff_ref, win_ref, lng_ref,
                lnb_ref, ws_ref, bs_ref, wo_ref, rwh_ref, rwl_ref, side_ref,
                h_ref, hf_ref, lg_ref, side_out_ref, us_ref):
    side_out_ref[...] = side_ref[...].astype(BF16)
    tm, d = x_ref.shape[1:]
    gw = d // SGU_GROUPS
    for rb in range(tm // SGU_RB):
        r0 = rb * SGU_RB
        x = x_ref[0, r0:r0 + SGU_RB, :]
        hm = _modulate(x, g_ref[...], cm_ref[0], sm_ref[0]).astype(BF16)
        z = jax.nn.gelu(_dot(hm, win_ref[...]))
        v = z[:, d:]
        mu = jnp.mean(v, axis=-1, keepdims=True)
        vc = v - mu
        var = jnp.mean(vc * vc, axis=-1, keepdims=True)
        vn = ((vc * lax.rsqrt(var + EPS)) * lng_ref[...] + lnb_ref[...]).astype(BF16)
        for n in range(SGU_RB // CHUNK):
            rows = slice(n * CHUNK, (n + 1) * CHUNK)
            for gi in range(SGU_GROUPS):
                cols = slice(gi * gw, (gi + 1) * gw)
                s = _dot(ws_ref[gi], vn[rows, cols]) + bs_ref[:, cols]
                us_ref[r0 + n * CHUNK:r0 + (n + 1) * CHUNK, cols] = (z[rows, cols] * s).astype(BF16)
        h = x + gm_ref[0] * _dot(us_ref[r0:r0 + SGU_RB, :], wo_ref[...])
        h_ref[0, r0:r0 + SGU_RB, :] = h
        hf = _modulate(h, gff_ref[...], cf_ref[0], sf_ref[0])
        _rows_to_tiles(hf_ref, hf, row0=r0)
        hi, lo = _split_bf16(hf)
        lg_ref[0, r0:r0 + SGU_RB, :] = (
            _dot(hi, rwh_ref[...]) + (_dot(hi, rwl_ref[...]) + _dot(lo, rwh_ref[...])))


def _sgu(h, mods, g_mix, g_ffn, w_in, ln_g, ln_b, w_s, bias, w_out, rw_hi, rw_lo, side_w):
    b, s, d = h.shape
    tm = TM_SGU
    tok = pl.BlockSpec((1, tm, d), lambda bi, i: (bi, i, 0))
    side_spec, side_sds = _side_cast(side_w, b * (s // tm), lambda bi, i: bi * (s // tm) + i)
    return pl.pallas_call(
        _sgu_kernel,
        grid=(b, s // tm),
        in_specs=[
            tok,
            _mod_spec(1, 0, lambda bi, i: bi),
            _mod_spec(1, 1, lambda bi, i: bi),
            _mod_spec(1, 2, lambda bi, i: bi),
            _mod_spec(1, 3, lambda bi, i: bi),
            _mod_spec(1, 4, lambda bi, i: bi),
            _const_spec((1, d)),
            _const_spec((1, d)),
            _const_spec(w_in.shape),
            _const_spec((1, d)),
            _const_spec((1, d)),
            _const_spec(w_s.shape),
            _const_spec(bias.shape),
            _const_spec(w_out.shape),
            _const_spec(rw_hi.shape),
            _const_spec(rw_lo.shape),
            side_spec,
        ],
        out_specs=[tok,
                   pl.BlockSpec((tm * SUBLANES, LANES), lambda bi, i: (bi * (s // tm) + i, 0)),
                   pl.BlockSpec((1, tm, N_EXPERTS), lambda bi, i: (bi, i, 0)),
                   side_spec],
        out_shape=[jax.ShapeDtypeStruct((b, s, d), F32),
                   jax.ShapeDtypeStruct((b * s * SUBLANES, LANES), F32),
                   jax.ShapeDtypeStruct((b, s, N_EXPERTS), F32),
                   side_sds],
        scratch_shapes=[pltpu.VMEM((tm, d), BF16)],
        compiler_params=_params(2),
        name="sgu",
    )(h, mods, mods, mods, mods, mods, g_mix, g_ffn, w_in, ln_g, ln_b, w_s, bias, w_out,
      rw_hi, rw_lo, side_w)


def _route_kernel(lt_ref, pos_ref, gate_ref, cnt_ref, *, tile_rows):
    lt = lt_ref[...]
    e, t = lt.shape
    eid = lax.broadcasted_iota(jnp.int32, (e, t), 0).astype(F32)
    lane = lax.broadcasted_iota(jnp.int32, (e, t), 1)
    v1 = jnp.max(lt, axis=0, keepdims=True)
    i1 = jnp.min(jnp.where(lt == v1, eid, float(e)), axis=0, keepdims=True)
    first = eid == i1
    rest = jnp.where(first, -jnp.inf, lt)
    v2 = jnp.max(rest, axis=0, keepdims=True)
    i2 = jnp.min(jnp.where(rest == v2, eid, float(e)), axis=0, keepdims=True)
    second = eid == i2
    e2 = jnp.exp(v2 - v1)
    gate_ref[0:1, :] = 1.0 / (1.0 + e2)
    gate_ref[1:2, :] = e2 / (1.0 + e2)

    onehot = jnp.where(first, 1.0, jnp.where(second, 1.0, 0.0))
    csum = onehot
    d = 1
    while d < t:
        csum = csum + jnp.where(lane >= d, pltpu.roll(csum, d, axis=1), 0.0)
        d *= 2
    counts = csum[:, t - 1:t]
    padded = jnp.ceil(counts * (1.0 / tile_rows)) * tile_rows
    pad_b = jnp.broadcast_to(padded, (e, LANES))
    sub = lax.broadcasted_iota(jnp.int32, (e, LANES), 0)
    starts = jnp.zeros((e, LANES), F32)
    for k in range(1, e):
        starts = starts + jnp.where(sub >= k, pltpu.roll(pad_b, k, axis=0), 0.0)
    slot = starts[:, 0:1] + csum - onehot
    pos_ref[0:1, :] = jnp.sum(jnp.where(first, slot, 0.0), axis=0, keepdims=True).astype(jnp.int32)
    pos_ref[1:2, :] = jnp.sum(jnp.where(second, slot, 0.0), axis=0, keepdims=True).astype(jnp.int32)
    cnt_ref[...] = jnp.broadcast_to(counts, (e, LANES)).astype(jnp.int32)


def _route(logits, tile_rows, n_tiles):
    lt = logits.T
    t = lt.shape[1]
    whole = lambda shape: pl.BlockSpec(shape, lambda: (0,) * len(shape))
    pos2, gates2, cnt = pl.pallas_call(
        functools.partial(_route_kernel, tile_rows=tile_rows),
        in_specs=[whole(lt.shape)],
        out_specs=[whole((TOP_K, t)), whole((TOP_K, t)), whole((N_EXPERTS, LANES))],
        out_shape=[jax.ShapeDtypeStruct((TOP_K, t), jnp.int32),
                   jax.ShapeDtypeStruct((TOP_K, t), F32),
                   jax.ShapeDtypeStruct((N_EXPERTS, LANES), jnp.int32)],
        name="moe_route",
    )(lt)
    pos = pos2.reshape(-1)
    gates = gates2.T
    counts = cnt[:, 0]
    padded = ((counts + tile_rows - 1) // tile_rows) * tile_rows
    ends = jnp.cumsum(padded)
    starts = ends - padded
    tile_start = jnp.arange(n_tiles, dtype=jnp.int32) * tile_rows
    valid = tile_start < ends[-1]
    tile_e = jnp.sum(tile_start[:, None] >= ends[None, :], axis=-1).astype(jnp.int32)
    last_e = jnp.max(jnp.where(counts > 0, jnp.arange(N_EXPERTS, dtype=jnp.int32), 0))
    tile_e = jnp.where(valid, tile_e, last_e)
    total = jnp.full((1,), n_tiles * tile_rows, jnp.int32)
    pad_lo = jnp.concatenate([starts + counts, ends[-1:]]).astype(jnp.int32)
    pad_hi = jnp.concatenate([ends, total]).astype(jnp.int32)
    rows_used = jnp.clip(pad_lo[tile_e] - tile_start, 0, tile_rows)
    tile_state = jnp.where(valid, jnp.where(rows_used <= tile_rows // 2, HALF_TILE, FULL_TILE),
                           UNUSED_TILE).astype(jnp.int32)
    return pos, gates, tile_e, tile_state, pad_lo, pad_hi


UNUSED_TILE, FULL_TILE, HALF_TILE = 0, 1, 2


INVERT_UNROLL = 32
FILL_UNROLL = 8


def _invert_kernel(pos_ref, lo_ref, hi_ref, pair_ref):
    n_pairs = pos_ref.shape[0]
    assert n_pairs % INVERT_UNROLL == 0

    def fill(blk, lo):
        for u in range(FILL_UNROLL):
            pair_ref[lo + blk * FILL_UNROLL + u] = -1
        return lo

    def place(blk, c):
        for u in range(INVERT_UNROLL):
            i = blk * INVERT_UNROLL + u
            pair_ref[pos_ref[i]] = i
        return c

    for g in range(lo_ref.shape[0]):
        n_blocks = (hi_ref[g] - lo_ref[g] + FILL_UNROLL - 1) // FILL_UNROLL
        lax.fori_loop(0, n_blocks, fill, lo_ref[g])
    lax.fori_loop(0, n_pairs // INVERT_UNROLL, place, 0)


def _row_tables(pos, pad_lo, pad_hi, n_tokens, n_tiles):
    n_rows = n_tiles * TM_EXP
    smem = pl.BlockSpec(memory_space=pltpu.SMEM)
    pair = pl.pallas_call(
        _invert_kernel,
        in_specs=[smem, smem, smem],
        out_specs=smem,
        out_shape=jax.ShapeDtypeStruct((n_rows,), jnp.int32),
        name="moe_invert",
    )(pos, pad_lo, pad_hi)
    scratch_row = n_tokens * TOP_K + jnp.arange(n_rows, dtype=jnp.int32) % TM_EXP
    src = jnp.where(pair < 0, 0, jnp.where(pair >= n_tokens, pair - n_tokens, pair))
    dst = jnp.where(pair < 0, scratch_row, pair)
    return src * SUBLANES, jnp.concatenate([scratch_row[:TM_EXP], dst]) * SUBLANES


def _expert_kernel(te_ref, tv_ref, tf_ref, src_ref, dst_ref, hf_ref, wg_ref, wu_ref, wd_ref, y_ref,
                   x_ref, xb_ref, acc_ref, yt_ref, sem_in, sem_out):
    del te_ref, tf_ref
    tm = TM_EXP
    j = pl.program_id(0)
    step = pl.program_id(1)
    valid = tv_ref[j] > 0
    prev_valid = jnp.logical_and(j > 0, tv_ref[jnp.maximum(j - 1, 0)] > 0)
    prev2_valid = jnp.logical_and(j > 1, tv_ref[jnp.maximum(j - 2, 0)] > 0)
    slot = j % 2
    x_rows = lambda r: x_ref.at[pl.ds(r * SUBLANES, SUBLANES)]
    yt_rows = lambda sl, r: yt_ref.at[sl, pl.ds(r * SUBLANES, SUBLANES)]

    def start_gathers(tile):
        for r in range(tm):
            t8 = src_ref[tile * tm + r]
            pltpu.make_async_copy(hf_ref.at[pl.ds(pl.multiple_of(t8, SUBLANES), SUBLANES)],
                                  x_rows(r), sem_in).start(priority=ROW_DMA_PRIORITY)

    def gather_wait(r):
        return pltpu.make_async_copy(hf_ref.at[pl.ds(0, SUBLANES)], x_rows(r), sem_in)

    def start_scatters(tile, sl):
        for r in range(tm):
            d8 = dst_ref[(tile + 1) * tm + r]
            pltpu.make_async_copy(yt_rows(sl, r),
                                  y_ref.at[pl.ds(pl.multiple_of(d8, SUBLANES), SUBLANES)],
                                  sem_out).start(priority=r % 2)

    def scatter_wait(sl, r):
        return pltpu.make_async_copy(yt_rows(sl, r), y_ref.at[pl.ds(0, SUBLANES)], sem_out)

    def ffn_step(rows):
        xb = xb_ref[:rows]
        a = jax.nn.silu(_dot(xb, wg_ref[0])) * _dot(xb, wu_ref[0])
        return _dot(a.astype(BF16), wd_ref[0])

    half_tile = tv_ref[j] == HALF_TILE
    full_tile = tv_ref[j] == FULL_TILE
    hm = tm // 2

    @pl.when(jnp.logical_and(j == 0, step == 0))
    def _():
        yt_ref[...] = jnp.zeros_like(yt_ref)
        start_gathers(0)

    @pl.when(jnp.logical_and(step == 0, jnp.logical_or(j == 0, prev_valid)))
    def _():
        for r in range(tm):
            gather_wait(r).wait()

    @pl.when(jnp.logical_and(full_tile, step == 0))
    def _():
        xb_ref[...] = _tiles_to_rows(x_ref, tm).astype(BF16)
        start_gathers(j + 1)
        acc_ref[...] = ffn_step(tm)

    @pl.when(jnp.logical_and(half_tile, step == 0))
    def _():
        xb_ref[...] = _tiles_to_rows(x_ref, tm).astype(BF16)
        start_gathers(j + 1)
        acc_ref[:hm] = ffn_step(hm)

    @pl.when(jnp.logical_and(step == 1, jnp.logical_or(j == 1, prev2_valid)))
    def _():
        for r in range(tm):
            scatter_wait(slot, r).wait()

    @pl.when(jnp.logical_and(full_tile, step == 1))
    def _():
        start_scatters(j - 1, 1 - slot)
        _rows_to_tiles(yt_ref.at[slot], acc_ref[...] + ffn_step(tm))

    @pl.when(jnp.logical_and(half_tile, step == 1))
    def _():
        start_scatters(j - 1, 1 - slot)
        _rows_to_tiles(yt_ref.at[slot], acc_ref[:hm] + ffn_step(hm))

    @pl.when(jnp.logical_and(jnp.logical_and(jnp.logical_not(valid), prev_valid), step == 1))
    def _():
        start_scatters(j - 1, 1 - slot)


def _expert_ffn(tile_e, tile_valid, src, dst, hf, wg, wu, wd):
    n_tok = hf.shape[0] // SUBLANES
    d = D_MODEL
    n_tiles = tile_e.shape[0]
    d_ff = wg.shape[2]
    nf = d_ff // TF_EXP
    assert nf == 2
    j = jnp.arange(n_tiles, dtype=jnp.int32)
    first = j % 2
    n_valid = jnp.sum(tile_valid > 0)
    last_f = 1 - (n_valid - 1) % 2
    half = jnp.where(tile_valid[:, None] > 0, jnp.stack([first, 1 - first], axis=1), last_f)
    tile_f = half.reshape(-1).astype(jnp.int32)

    return pl.pallas_call(
        _expert_kernel,
        grid_spec=pltpu.PrefetchScalarGridSpec(
            num_scalar_prefetch=5,
            grid=(n_tiles, nf),
            in_specs=[
                pl.BlockSpec(memory_space=pl.ANY),
                pl.BlockSpec((1, d, TF_EXP), lambda j, s, te, tv, tf, *_: (te[j], 0, tf[2 * j + s])),
                pl.BlockSpec((1, d, TF_EXP), lambda j, s, te, tv, tf, *_: (te[j], 0, tf[2 * j + s])),
                pl.BlockSpec((1, TF_EXP, d), lambda j, s, te, tv, tf, *_: (te[j], tf[2 * j + s], 0)),
            ],
            out_specs=pl.BlockSpec(memory_space=pl.ANY),
            scratch_shapes=[pltpu.VMEM((TM_EXP * SUBLANES, LANES), F32),
                            pltpu.VMEM((TM_EXP, d), BF16),
                            pltpu.VMEM((TM_EXP, d), F32),
                            pltpu.VMEM((2, TM_EXP * SUBLANES, LANES), F32),
                            pltpu.SemaphoreType.DMA(()), pltpu.SemaphoreType.DMA(())],
        ),
        out_shape=jax.ShapeDtypeStruct(((TOP_K * n_tok + TM_EXP) * SUBLANES, LANES), F32),
        compiler_params=_params(2),
        name="moe_experts",
    )(tile_e, tile_valid, tile_f, src, dst, hf, wg, wu, wd)


def _combine_kernel(y0_ref, y1_ref, h_ref, gate_ref, gf_ref, o_ref):
    gates = gate_ref[...]
    n = h_ref.shape[0]
    mix = gates[:, 0:1] * _tiles_to_rows(y0_ref, n) + gates[:, 1:2] * _tiles_to_rows(y1_ref, n)
    o_ref[...] = h_ref[...] + gf_ref[0] * mix


def _combine_rows(y, h, gates, mods, seq):
    t, d = h.shape
    steps_per_batch = seq // TG
    second = t // TG
    tok = pl.BlockSpec((TG, d), lambda i: (i, 0))
    return pl.pallas_call(
        _combine_kernel,
        grid=(t // TG,),
        in_specs=[
            pl.BlockSpec((TG * SUBLANES, LANES), lambda i: (i, 0)),
            pl.BlockSpec((TG * SUBLANES, LANES), lambda i: (second + i, 0)),
            tok,
            pl.BlockSpec((TG, TOP_K), lambda i: (i, 0)),
            pl.BlockSpec((1, 1, d), lambda i: (COND_ROWS + i // steps_per_batch, 0, N_MODS - 1)),
        ],
        out_specs=tok,
        out_shape=jax.ShapeDtypeStruct((t, d), F32),
        compiler_params=_params(1),
        name="moe_combine",
    )(y, y, h, gates, mods)


def _rope_tables(seq):
    rows = seq // GRID_W
    row = np.repeat(np.arange(rows), GRID_W).astype(np.float32)
    col = np.tile(np.arange(GRID_W), rows).astype(np.float32)
    quarter = ATTN_DIM // 4
    inv = (np.float32(ROPE_BASE) ** (-np.arange(quarter, dtype=np.float32) / quarter)).astype(
        np.float32)
    ar = row[:, None] * inv
    ac = col[:, None] * inv
    ang = np.concatenate([ar, ar, ac, ac], axis=-1)
    sign = np.tile(np.concatenate([-np.ones(quarter, np.float32), np.ones(quarter, np.float32)]), 2)
    reps = QK_W // ATTN_DIM
    cos = np.tile(np.cos(ang).astype(np.float32), (1, reps))
    sin = np.tile((np.sin(ang) * sign).astype(np.float32), (1, reps))
    return jnp.asarray(cos), jnp.asarray(sin)


def kernel(x, c, ctx, c_ctx, ada_w, ada_b, norm_mix_g, norm_ffn_g, w_in_even, q_norm_g, k_norm_g,
           lam_q1, lam_k1, lam_q2, lam_k2, subln_g, conv_w, w_out_even, ffn_w_gate, ffn_w_up,
           ffn_w_down, sgu_w_in, sgu_ln_g, sgu_ln_b, sgu_w_s, sgu_b_s, sgu_w_out, router_w,
           moe_w_gate, moe_w_up, moe_w_down):
    b, s, d = x.shape
    assert d == D_MODEL and b < COND_ROWS and ada_w.shape[0] == 2 and conv_w.shape[1] == CONV_K
    assert s % TM_PROJ == 0 and s % TQ == 0 and s % TM_FFN == 0 and s % TM_SGU == 0 and s % TG == 0

    cond = jnp.zeros((COND_ROWS, d), F32).at[:b].set(c).at[b].set(c_ctx)
    mods = _ada_mods(cond, ada_w, ada_b)

    lambda_init = 0.8 - 0.6 * math.exp(-0.3 * 0)
    group = np.arange(QK_W) // ATTN_DIM
    bd = jnp.asarray(group[:, None] == group[None, :], dtype=BF16)
    reps = QK_W // ATTN_DIM
    qg = jnp.tile(q_norm_g[0], reps)[None, :]
    kg = jnp.tile(k_norm_g[0], reps)[None, :]
    cos, sin = _rope_tables(s)
    w_in = w_in_even[0].astype(BF16)
    g_mix0 = norm_mix_g[0][None, :]
    n_e, _, d_ffe = moe_w_gate[0].shape
    q, k, v, yconv, moe_wg = _inproj(x, mods, g_mix0, w_in, bd, qg, kg, cos, sin, conv_w[0],
                                     moe_w_gate[0].reshape(n_e * d, d_ffe))
    kc, vc = _ctxproj(ctx, mods, g_mix0, w_in, bd, kg, ctx_row=b)
    attn, w_out0, ffn_wg, ffn_wu, ffn_wd = _attention(
        q, k, kc, v, vc, subln_g[0][None, :], lam_q1[0][None, :], lam_k1[0][None, :],
        lam_q2[0][None, :], lam_k2[0][None, :], lambda_init,
        [(w_out_even[0], 1), (ffn_w_gate[0], 1), (ffn_w_up[0], 1), (ffn_w_down[0], 2)])
    h, moe_wd, sgu_win, sgu_wout = _outproj_ffn(
        x, attn, yconv, mods, norm_ffn_g[0][None, :], w_out0, ffn_wg, ffn_wu, ffn_wd,
        [moe_w_down[0].reshape(n_e * d_ffe, d), sgu_w_in[0], sgu_w_out[0]])

    gw = d // SGU_GROUPS
    bias = jnp.repeat(sgu_b_s[0].T, gw, axis=1)
    rw_hi = router_w[0].astype(BF16)
    rw_lo = (router_w[0] - rw_hi.astype(F32)).astype(BF16)
    h, hf, logits, moe_wu = _sgu(h, mods, norm_mix_g[1][None, :], norm_ffn_g[1][None, :],
                                 sgu_win, sgu_ln_g[0][None, :],
                                 sgu_ln_b[0][None, :], sgu_w_s[0].astype(BF16), bias,
                                 sgu_wout, rw_hi, rw_lo,
                                 moe_w_up[0].reshape(n_e * d, d_ffe))

    t = b * s
    n_tiles = (t * TOP_K) // TM_EXP + N_EXPERTS + 2
    pos, gates, tile_e, tile_valid, pad_lo, pad_hi = _route(
        logits.reshape(t, N_EXPERTS), TM_EXP, n_tiles)
    src, dst = _row_tables(pos, pad_lo, pad_hi, t, n_tiles)
    y = _expert_ffn(tile_e, tile_valid, src, dst, hf, moe_wg.reshape(n_e, d, d_ffe),
                    moe_wu.reshape(n_e, d, d_ffe), moe_wd.reshape(n_e, d_ffe, d))
    out = _combine_rows(y, h.reshape(t, d), gates, mods, s)
    return out.reshape(b, s, d)
```
